```python
import math
import jax
import jax.numpy as jnp
from jax import lax
import numpy as np

D_MODEL = 2048
BATCH = 4
SEQ = 2048
DEPTH = 4
DEC_BATCH = 8
DEC_SEQ = 64
PAST_LEN = 4096

CHUNK = 64
Q_BLOCK = 128
N_MIXERS = 3
NA_LAYERS = (DEPTH + 2) // 3
NB_LAYERS = (DEPTH + 1) // 3
NC_LAYERS = DEPTH // 3
D_MIX = D_MODEL
ROPE_THETA = 10000.0
RMS_EPS = 1e-6
DH_A = 64
H_A = D_MIX // (2 * DH_A)
SUBLN_EPS = 1e-5
HD_B = 128
H_B = D_MIX // HD_B
KV_B = 4
H_IDX = 16
D_IDX = 64
TOPK_MAX = 256
B_SIZES = (H_B * HD_B, KV_B * HD_B, KV_B * HD_B, H_IDX * D_IDX, D_IDX, H_IDX, D_MIX)
B_IN = sum(B_SIZES)
HS_C = 64
H_C = D_MIX // HS_C
R_DECAY = 96
R_ICLR = 96
GN_EPS = 64e-5

kernel_name = 'hybrid_chunk_stream_diffattn_dsa_rwkv7'


def _rmsnorm(x, g, eps=RMS_EPS):
    xf = x.astype(jnp.float32)
    y = xf * lax.rsqrt(jnp.mean(xf * xf, axis=-1, keepdims=True) + eps)
    return (y * g.astype(jnp.float32)).astype(x.dtype)


def _rope(x, pos):
    dh = x.shape[-1]
    half = dh // 2
    inv = jnp.power(ROPE_THETA, -jnp.arange(half, dtype=jnp.float32) * (2.0 / dh))
    ang = pos.astype(jnp.float32)[:, None] * inv[None, :]
    shape = (ang.shape[0],) + (1,) * (x.ndim - 3) + (half,)
    cos = jnp.cos(ang).reshape(shape)
    sin = jnp.sin(ang).reshape(shape)
    xf = x.astype(jnp.float32)
    x1, x2 = xf[..., :half], xf[..., half:]
    return jnp.concatenate([x1 * cos - x2 * sin, x2 * cos + x1 * sin], axis=-1).astype(x.dtype)


def _over_query_blocks(fn, *qs):
    t = qs[0].shape[1]
    if t <= Q_BLOCK or t % Q_BLOCK:
        return fn(*qs)
    nb = t // Q_BLOCK
    blocks = tuple(jnp.moveaxis(a.reshape((a.shape[0], nb, Q_BLOCK) + a.shape[2:]), 1, 0) for a in qs)
    out = lax.map(lambda args: fn(*args), blocks)
    out = jnp.moveaxis(out, 0, 1)
    return out.reshape((out.shape[0], t) + out.shape[3:])


def _diff_attn(h, pos, k_past, v_past, w_in, lam_p, subln_g, lam_init):
    b, t, _ = h.shape
    q, k, v, g = jnp.split(h @ w_in, 4, axis=-1)
    q = _rope(q.reshape(b, t, H_A, 2, DH_A), pos)
    k = _rope(k.reshape(b, t, H_A, 2, DH_A), pos)
    k_rows = k.reshape(b, t, H_A, 2 * DH_A)
    v_rows = v.reshape(b, t, H_A, 2 * DH_A)
    if k_past is None:
        k_all, v_all = k_rows, v_rows
    else:
        k_all = jnp.concatenate([k_past.astype(k_rows.dtype), k_rows], axis=1)
        v_all = jnp.concatenate([v_past.astype(v_rows.dtype), v_rows], axis=1)
    s = k_all.shape[1]
    kpos = jnp.arange(s)
    k_all = k_all.reshape(b, s, H_A, 2, DH_A)
    k1, k2 = k_all[..., 0, :], k_all[..., 1, :]
    lp = lam_p.astype(jnp.float32)
    lam = jnp.exp(jnp.sum(lp[0] * lp[1])) - jnp.exp(jnp.sum(lp[2] * lp[3])) + lam_init

    def core(q1, q2, qp):
        mask = (kpos // CHUNK)[None, :] <= (qp[0] // CHUNK)[:, None]
        def attn_map(qq, kk):
            sc = jnp.einsum('bqhd,bkhd->bhqk', qq.astype(jnp.float32), kk.astype(jnp.float32)) * DH_A ** -0.5
            return jax.nn.softmax(jnp.where(mask, sc, -jnp.inf), axis=-1)
        p = attn_map(q1, k1) - lam * attn_map(q2, k2)
        return jnp.einsum('bhqk,bkhe->bqhe', p, v_all.astype(jnp.float32))

    o = _over_query_blocks(core, q[..., 0, :], q[..., 1, :], pos[None])
    o = o * lax.rsqrt(jnp.mean(o * o, axis=-1, keepdims=True) + SUBLN_EPS) * subln_g.astype(jnp.float32)
    o = (o * (1.0 - lam_init)).reshape(b, t, D_MIX)
    return (o * jax.nn.silu(g.astype(jnp.float32))).astype(h.dtype), k_rows, v_rows


def _dsa_attn(h, pos, k_past, v_past, ki_past, w_in):
    b, t, _ = h.shape
    offs = np.cumsum(B_SIZES)[:-1].tolist()
    q, k, v, qi, ki, wi, g = jnp.split(h @ w_in, offs, axis=-1)
    q = _rope(q.reshape(b, t, H_B, HD_B), pos)
    k_rows = _rope(k.reshape(b, t, KV_B, HD_B), pos)
    v_rows = v.reshape(b, t, KV_B, HD_B)
    qi = _rope(qi.reshape(b, t, H_IDX, D_IDX), pos)
    ki_rows = _rope(ki, pos)
    wi = wi * H_IDX ** -0.5
    if k_past is None:
        k_all, v_all, ki_all = k_rows, v_rows, ki_rows
    else:
        k_all = jnp.concatenate([k_past.astype(k_rows.dtype), k_rows], axis=1)
        v_all = jnp.concatenate([v_past.astype(v_rows.dtype), v_rows], axis=1)
        ki_all = jnp.concatenate([ki_past.astype(ki_rows.dtype), ki_rows], axis=1)
    s = k_all.shape[1]
    kpos = jnp.arange(s)
    n_sel = min(TOPK_MAX, s // 4)
    take = jax.vmap(lambda rows, ix: rows[ix])

    def core(q_, qi_, wi_, qp):
        qp = qp[0]
        tq = q_.shape[1]
        adm = (kpos // CHUNK)[None, :] <= (qp // CHUNK)[:, None]
        logits = jnp.einsum('bqhd,bsd->bqhs', qi_.astype(jnp.float32), ki_all.astype(jnp.float32)) * D_IDX ** -0.5
        score = jnp.einsum('bqh,bqhs->bqs', wi_.astype(jnp.float32), jax.nn.relu(logits))
        score = jnp.where(adm[None], score, -jnp.inf)
        _, idx = lax.top_k(score, n_sel)
        valid = (idx // CHUNK) <= (qp // CHUNK)[None, :, None]
        kg = take(k_all, idx).astype(jnp.float32)
        vg = take(v_all, idx).astype(jnp.float32)
        qg = q_.astype(jnp.float32).reshape(b, tq, KV_B, H_B // KV_B, HD_B)
        sc = jnp.einsum('bqgrd,bqngd->bqgrn', qg, kg) * HD_B ** -0.5
        p = jax.nn.softmax(jnp.where(valid[:, :, None, None, :], sc, -jnp.inf), axis=-1)
        return jnp.einsum('bqgrn,bqngd->bqgrd', p, vg).reshape(b, tq, D_MIX)

    o = _over_query_blocks(core, q, qi, wi, pos[None])
    return (o * jax.nn.silu(g.astype(jnp.float32))).astype(h.dtype), k_rows, v_rows, ki_rows


def _rwkv_scan(r, w, k, v, kk, a, s0):
    def step(st, inp):
        r_t, w_t, k_t, v_t, kk_t, a_t = inp
        sa = jnp.einsum('bhvk,bhk->bhv', st, -kk_t)
        st = st * w_t[:, :, None, :] + sa[..., None] * (kk_t * a_t)[:, :, None, :] + v_t[..., None] * k_t[:, :, None, :]
        return st, jnp.einsum('bhvk,bhk->bhv', st, r_t)
    xs = tuple(jnp.moveaxis(u, 1, 0) for u in (r, w, k, v, kk, a))
    s_new, o = lax.scan(step, s0, xs)
    return jnp.moveaxis(o, 0, 1), s_new


def _rwkv_mix(h, shift0, s0, mu, w_rkvg, w0, w_la, w_lb, a0, a_la, a_lb, k_k, k_a, r_k, ln_w, ln_b):
    b, t, d = h.shape
    hf = h.astype(jnp.float32)
    prev = jnp.concatenate([shift0.astype(jnp.float32)[:, None], hf[:, :-1]], axis=1)
    lerp = hf[None] + (prev - hf)[None] * mu.astype(jnp.float32)[:, None, None, :]
    r, k, v, g = jnp.einsum('nbtd,nde->nbte', lerp[:4], w_rkvg.astype(jnp.float32))
    w_log = -jax.nn.softplus(-(w0 + jnp.tanh(lerp[4] @ w_la) @ w_lb)) - 0.5
    decay = jnp.exp(-jnp.exp(w_log))
    a = jax.nn.sigmoid(a0 + (lerp[5] @ a_la) @ a_lb)
    heads = lambda u: u.reshape(b, t, H_C, HS_C)
    kk = heads(k * k_k)
    kk = kk / jnp.maximum(jnp.sqrt(jnp.sum(kk * kk, axis=-1, keepdims=True)), 1e-12)
    k = k * (1.0 + (a - 1.0) * k_a)
    r_h, k_h, v_h, w_h, a_h = heads(r), heads(k), heads(v), heads(decay), heads(a)
    o, s_new = _rwkv_scan(r_h, w_h, k_h, v_h, kk, a_h, s0.astype(jnp.float32))
    mean = jnp.mean(o, axis=-1, keepdims=True)
    var = jnp.mean(jnp.square(o - mean), axis=-1, keepdims=True)
    o = ((o - mean) * lax.rsqrt(var + GN_EPS)).reshape(b, t, d) * ln_w + ln_b
    bonus = jnp.sum(r_h * k_h * r_k, axis=-1, keepdims=True) * v_h
    o = o + bonus.reshape(b, t, d)
    return (o * jax.nn.silu(g)).astype(h.dtype), s_new, h[:, -1]


def setup_inputs(seed: int = 0) -> dict:
    key = jax.random.key(seed)
    ks = iter(jax.random.split(key, 40))
    f32 = jnp.float32
    d = D_MODEL
    nrm = lambda shape, scale: jax.random.normal(next(ks), shape, f32) * scale
    return {
        'x_prompt': nrm((BATCH, SEQ, d), 1.0),
        'x_sample': nrm((DEC_BATCH, DEC_SEQ, d), 1.0),
        'cache_a_k': nrm((NA_LAYERS, DEC_BATCH, PAST_LEN, H_A, 2 * DH_A), 1.0),
        'cache_a_v': nrm((NA_LAYERS, DEC_BATCH, PAST_LEN, H_A, 2 * DH_A), 1.0),
        'cache_b_k': nrm((NB_LAYERS, DEC_BATCH, PAST_LEN, KV_B, HD_B), 1.0),
        'cache_b_v': nrm((NB_LAYERS, DEC_BATCH, PAST_LEN, KV_B, HD_B), 1.0),
        'cache_b_kidx': nrm((NB_LAYERS, DEC_BATCH, PAST_LEN, D_IDX), 1.0),
        'state_c_wkv': nrm((NC_LAYERS, DEC_BATCH, H_C, HS_C, HS_C), 0.3),
        'state_c_shift': nrm((NC_LAYERS, DEC_BATCH, d), 1.0),
        'norm_g': 1.0 + nrm((DEPTH, d), 0.02),
        'final_g': 1.0 + nrm((d,), 0.02),
        'w_out': nrm((DEPTH, D_MIX, d), D_MIX ** -0.5),
        'a_w_in': nrm((NA_LAYERS, d, 4 * D_MIX), d ** -0.5),
        'a_lam': nrm((NA_LAYERS, 4, DH_A), 0.1),
        'a_subln_g': 1.0 + nrm((NA_LAYERS, 2 * DH_A), 0.02),
        'b_w_in': nrm((NB_LAYERS, d, B_IN), d ** -0.5),
        'c_mu': jax.random.uniform(next(ks), (NC_LAYERS, 6, d), f32),
        'c_w_rkvg': nrm((NC_LAYERS, 4, d, D_MIX), d ** -0.5),
        'c_w0': -1.0 + nrm((NC_LAYERS, D_MIX), 0.3),
        'c_w_la': nrm((NC_LAYERS, d, R_DECAY), d ** -0.5),
        'c_w_lb': nrm((NC_LAYERS, R_DECAY, D_MIX), 0.1 * R_DECAY ** -0.5),
        'c_a0': nrm((NC_LAYERS, D_MIX), 0.1),
        'c_a_la': nrm((NC_LAYERS, d, R_ICLR), d ** -0.5),
        'c_a_lb': nrm((NC_LAYERS, R_ICLR, D_MIX), 0.1 * R_ICLR ** -0.5),
        'c_k_k': 0.85 + nrm((NC_LAYERS, D_MIX), 0.02),
        'c_k_a': 1.0 + nrm((NC_LAYERS, D_MIX), 0.02),
        'c_r_k': nrm((NC_LAYERS, H_C, HS_C), 0.1),
        'c_ln_w': 1.0 + nrm((NC_LAYERS, D_MIX), 0.02),
        'c_ln_b': nrm((NC_LAYERS, D_MIX), 0.02),
    }


def reference(x_prompt, x_sample, cache_a_k, cache_a_v, cache_b_k, cache_b_v, cache_b_kidx, state_c_wkv, state_c_shift, norm_g, final_g, w_out, a_w_in, a_lam, a_subln_g, b_w_in, c_mu, c_w_rkvg, c_w0, c_w_la, c_w_lb, c_a0, c_a_la, c_a_lb, c_k_k, c_k_a, c_r_k, c_ln_w, c_ln_b):
    n_prompt = x_prompt.shape[1]
    past = cache_a_k.shape[2]
    n_new = x_sample.shape[1]
    pos_p = jnp.arange(n_prompt)
    pos_s = past + jnp.arange(n_new)
    xp, xs = x_prompt, x_sample
    a_kp, a_vp, a_ks, a_vs = [], [], [], []
    b_kp, b_vp, b_ip, b_ks, b_vs, b_is = [], [], [], [], [], []
    c_wp, c_hp, c_ws, c_hs = [], [], [], []
    for i in range(DEPTH):
        kind, j = i % N_MIXERS, i // N_MIXERS
        hp = _rmsnorm(xp, norm_g[i])
        hs = _rmsnorm(xs, norm_g[i])
        if kind == 0:
            lam_init = 0.8 - 0.6 * math.exp(-0.3 * i)
            op, kr, vr = _diff_attn(hp, pos_p, None, None, a_w_in[j], a_lam[j], a_subln_g[j], lam_init)
            a_kp.append(kr)
            a_vp.append(vr)
            os_, kr, vr = _diff_attn(hs, pos_s, cache_a_k[j], cache_a_v[j], a_w_in[j], a_lam[j], a_subln_g[j], lam_init)
            a_ks.append(kr)
            a_vs.append(vr)
        elif kind == 1:
            op, kr, vr, ir = _dsa_attn(hp, pos_p, None, None, None, b_w_in[j])
            b_kp.append(kr)
            b_vp.append(vr)
            b_ip.append(ir)
            os_, kr, vr, ir = _dsa_attn(hs, pos_s, cache_b_k[j], cache_b_v[j], cache_b_kidx[j], b_w_in[j])
            b_ks.append(kr)
            b_vs.append(vr)
            b_is.append(ir)
        else:
            cp = (c_mu[j], c_w_rkvg[j], c_w0[j], c_w_la[j], c_w_lb[j], c_a0[j], c_a_la[j], c_a_lb[j], c_k_k[j], c_k_a[j], c_r_k[j], c_ln_w[j], c_ln_b[j])
            bp = xp.shape[0]
            op, sw, sh = _rwkv_mix(hp, jnp.zeros((bp, D_MODEL), hp.dtype), jnp.zeros((bp, H_C, HS_C, HS_C), jnp.float32), *cp)
            c_wp.append(sw)
            c_hp.append(sh)
            os_, sw, sh = _rwkv_mix(hs, state_c_shift[j], state_c_wkv[j], *cp)
            c_ws.append(sw)
            c_hs.append(sh)
        xp = xp + op @ w_out[i]
        xs = xs + os_ @ w_out[i]
    y_prompt = _rmsnorm(xp, final_g)
    y_sample = _rmsnorm(xs, final_g)
    return (y_prompt, y_sample,
            jnp.stack(a_kp), jnp.stack(a_vp), jnp.stack(a_ks), jnp.stack(a_vs),
            jnp.stack(b_kp), jnp.stack(b_vp), jnp.stack(b_ip), jnp.stack(b_ks), jnp.stack(b_vs), jnp.stack(b_is),
            jnp.stack(c_wp), jnp.stack(c_hp), jnp.stack(c_ws), jnp.stack(c_hs))
```

```python
import functools
import math

import jax
import jax.numpy as jnp
from jax import lax
from jax.experimental import pallas as pl
from jax.experimental.pallas import tpu as pltpu

F32 = jnp.float32
BF16 = jnp.bfloat16
I32 = jnp.int32

D_MODEL = 2048
BATCH = 4
SEQ = 2048
DEPTH = 4
DEC_BATCH = 8
DEC_SEQ = 64
PAST_LEN = 4096
CHUNK = 64
N_MIXERS = 3
ROPE_THETA = 10000.0
RMS_EPS = 1e-6
DH_A = 64
H_A = 16
SUBLN_EPS = 1e-5
HD_B = 128
H_B = 16
KV_B = 4
H_IDX = 16
D_IDX = 64
TOPK_MAX = 256
HS_C = 64
H_C = 32
R_DECAY = 96
R_ICLR = 96
GN_EPS = 64e-5

N_PROMPT = BATCH * SEQ
N_SAMPLE = DEC_BATCH * DEC_SEQ
M_ROWS = N_PROMPT + N_SAMPLE
LANES = 128
VMEM_LIMIT = 56 * 1024 * 1024
NEG_INF = float("-inf")
INT_MIN = -2 ** 31

B_Q0, B_K0, B_V0, B_QI0, B_G0, B_KI0 = 0, 2048, 2560, 3072, 4096, 6144
B_COLS = 6272

assert PAST_LEN % CHUNK == 0 and DEC_SEQ == CHUNK
LOG2_CHUNK = CHUNK.bit_length() - 1
assert 1 << LOG2_CHUNK == CHUNK


def _chunk_of(pos):
    return jnp.right_shift(pos, LOG2_CHUNK)


def _cparams(sem):
    return pltpu.CompilerParams(dimension_semantics=sem, vmem_limit_bytes=VMEM_LIMIT)


def _sigmoid(x):
    return 1.0 / (1.0 + jnp.exp(-x))


def _silu(x):
    return x * _sigmoid(x)


def _rms(x, g):
    return x * lax.rsqrt(jnp.mean(x * x, axis=-1, keepdims=True) + RMS_EPS) * g


def _dot_nt(a, b, precision=None):
    return lax.dot_general(a, b, (((1,), (1,)), ((), ())), precision=precision,
                           preferred_element_type=F32)


def _norm_proj_body(x_ref, g_ref, w_ref, o_ref, h_scr):
    @pl.when(pl.program_id(1) == 0)
    def _():
        h_scr[...] = _rms(x_ref[...], g_ref[...]).astype(BF16)
    o_ref[...] = jnp.dot(h_scr[...], w_ref[...], preferred_element_type=F32)


def _norm_proj(x, g, w, tm, tn):
    m, d = x.shape
    n = w.shape[1]
    return pl.pallas_call(
        _norm_proj_body,
        grid=(m // tm, n // tn),
        in_specs=[pl.BlockSpec((tm, d), lambda i, j: (i, 0)),
                  pl.BlockSpec((1, d), lambda i, j: (0, 0)),
                  pl.BlockSpec((d, tn), lambda i, j: (0, j))],
        out_specs=pl.BlockSpec((tm, tn), lambda i, j: (i, j)),
        out_shape=jax.ShapeDtypeStruct((m, n), F32),
        scratch_shapes=[pltpu.VMEM((tm, d), BF16)],
        compiler_params=_cparams(("parallel", "arbitrary")),
        name="norm_proj",
    )(x, g.reshape(1, d), w)


def _proj_out_body(og_ref, x_ref, w_ref, o_ref):
    o_ref[...] = x_ref[...] + jnp.dot(og_ref[...].astype(BF16), w_ref[...],
                                      preferred_element_type=F32)


def _proj_out_final_body(og_ref, x_ref, w_ref, g_ref, o_ref):
    xn = x_ref[...] + jnp.dot(og_ref[...].astype(BF16), w_ref[...],
                              preferred_element_type=F32)
    o_ref[...] = _rms(xn, g_ref[...])


def _proj_out(og, x, w, final_g=None, tm=256):
    m, d = x.shape
    row = pl.BlockSpec((tm, d), lambda i: (i, 0))
    full = pl.BlockSpec((d, d), lambda i: (0, 0))
    if final_g is None:
        body, extra, extra_specs = _proj_out_body, (), []
    else:
        body, extra = _proj_out_final_body, (final_g.reshape(1, d),)
        extra_specs = [pl.BlockSpec((1, d), lambda i: (0, 0))]
    return pl.pallas_call(
        body,
        grid=(m // tm,),
        in_specs=[row, row, full] + extra_specs,
        out_specs=row,
        out_shape=jax.ShapeDtypeStruct((m, d), F32),
        compiler_params=_cparams(("parallel",)),
        name="proj_out",
    )(og, x, w, *extra)


def _rope_tables():
    pos = jnp.concatenate([jnp.tile(jnp.arange(SEQ), BATCH),
                           jnp.tile(PAST_LEN + jnp.arange(DEC_SEQ), DEC_BATCH)]).astype(F32)

    def table(dh, reps):
        half = dh // 2
        inv = jnp.power(ROPE_THETA, -jnp.arange(half, dtype=F32) * (2.0 / dh))
        ang = pos[:, None] * inv[None, :]
        cos, sin = jnp.cos(ang), jnp.sin(ang)
        return (jnp.tile(jnp.concatenate([cos, cos], axis=1), (1, reps)),
                jnp.tile(jnp.concatenate([-sin, sin], axis=1), (1, reps)))

    cos1, sn1 = table(64, 2)
    cos2, sn2 = table(128, 1)
    return cos1, sn1, cos2, sn2


def _rope64(x, cos, sn):
    lane = lax.broadcasted_iota(I32, x.shape, 1)
    partner = jnp.where((lane & 63) < 32, pltpu.roll(x, 96, 1), pltpu.roll(x, 32, 1))
    return x * cos + partner * sn


def _rope128(x, cos, sn):
    return x * cos + pltpu.roll(x, 64, 1) * sn


def _rope_a_body(q_ref, k_ref, cos_ref, sn_ref, qo_ref, ko_ref):
    cos, sn = cos_ref[...], sn_ref[...]
    for h in range(H_A):
        sl = slice(h * LANES, (h + 1) * LANES)
        qo_ref[:, sl] = _rope64(q_ref[:, sl], cos, sn) * (DH_A ** -0.5)
        ko_ref[:, sl] = _rope64(k_ref[:, sl], cos, sn)


def _rope_a(qkvg, cos1, sn1, tm=256):
    m = qkvg.shape[0]
    blk = lambda c: pl.BlockSpec((tm, D_MODEL), lambda i, c=c: (i, c))
    tab = pl.BlockSpec((tm, LANES), lambda i: (i, 0))
    return pl.pallas_call(
        _rope_a_body,
        grid=(m // tm,),
        in_specs=[blk(0), blk(1), tab, tab],
        out_specs=[blk(0), blk(0)],
        out_shape=[jax.ShapeDtypeStruct((m, D_MODEL), F32)] * 2,
        compiler_params=_cparams(("parallel",)),
        name="rope_a",
    )(qkvg, qkvg, cos1, sn1)


def _stack_maps(q):
    lane = lax.broadcasted_iota(I32, q.shape, 1)
    return jnp.concatenate([jnp.where(lane < DH_A, q, 0.0),
                            jnp.where(lane >= DH_A, q, 0.0)], axis=0).astype(BF16)


def _flash_step(qs, kt, vt, mask, carry):
    m, l, acc = carry
    s = _dot_nt(qs, kt)
    if mask is not None:
        s = jnp.where(mask, s, NEG_INF)
    m_new = jnp.maximum(m, jnp.max(s, axis=-1, keepdims=True))
    alpha = jnp.exp(m - m_new)
    p = jnp.exp(s - m_new)
    l = alpha * l + jnp.sum(p, axis=-1, keepdims=True)
    acc = alpha * acc + jnp.dot(p.astype(BF16), vt, preferred_element_type=F32)
    return m_new, l, acc


def _flash_init(rows):
    return (jnp.full((rows, 1), -1e30, F32), jnp.zeros((rows, 1), F32),
            jnp.zeros((rows, LANES), F32))


def _diff_epilogue(carry, tq, lam_ref, sg_ref, g_ref, o_ref, lam_init):
    _, l, acc = carry
    o1 = acc[:tq] / l[:tq]
    o2 = acc[tq:] / l[tq:]
    lp = lam_ref[...]
    lam = (jnp.exp(jnp.sum(lp[0:1] * lp[1:2], axis=-1, keepdims=True))
           - jnp.exp(jnp.sum(lp[2:3] * lp[3:4], axis=-1, keepdims=True)) + lam_init)
    o = o1 - lam * o2
    o = o * lax.rsqrt(jnp.mean(o * o, axis=-1, keepdims=True) + SUBLN_EPS) * sg_ref[...]
    o = o * (1.0 - lam_init)
    o_ref[...] = o * _silu(g_ref[...])


def _attn_a_prompt_body(q_ref, k_ref, v_ref, g_ref, lam_ref, sg_ref, o_ref, *, tq, lam_init):
    qi = pl.program_id(2)
    qs = _stack_maps(q_ref[...])
    rows = 2 * tq
    qchunk = _chunk_of(qi * tq + (lax.broadcasted_iota(I32, (rows, tq), 0) & (tq - 1)))
    kcol = lax.broadcasted_iota(I32, (rows, tq), 1)

    def step(kt, carry):
        off = pl.multiple_of(kt * tq, tq)
        kb = k_ref[pl.ds(off, tq), :].astype(BF16)
        vb = v_ref[pl.ds(off, tq), :].astype(BF16)
        mask = _chunk_of(kt * tq + kcol) <= qchunk
        return _flash_step(qs, kb, vb, mask, carry)

    carry = lax.fori_loop(0, qi + 1, step, _flash_init(rows))
    _diff_epilogue(carry, tq, lam_ref, sg_ref, g_ref, o_ref, lam_init)


def _attn_a_prompt(q_rot, k_rot, qkvg, lam_p, subln_g, lam_init, tq=256):
    nq = SEQ // tq
    return pl.pallas_call(
        functools.partial(_attn_a_prompt_body, tq=tq, lam_init=lam_init),
        grid=(BATCH, H_A, nq),
        in_specs=[pl.BlockSpec((tq, LANES), lambda b, h, i: (b * nq + i, h)),
                  pl.BlockSpec((SEQ, LANES), lambda b, h, i: (b, h)),
                  pl.BlockSpec((SEQ, LANES), lambda b, h, i: (b, 2 * H_A + h)),
                  pl.BlockSpec((tq, LANES), lambda b, h, i: (b * nq + i, 3 * H_A + h)),
                  pl.BlockSpec((4, DH_A), lambda b, h, i: (0, 0)),
                  pl.BlockSpec((1, LANES), lambda b, h, i: (0, 0))],
        out_specs=pl.BlockSpec((tq, LANES), lambda b, h, i: (b * nq + i, h)),
        out_shape=jax.ShapeDtypeStruct((N_PROMPT, D_MODEL), F32),
        compiler_params=_cparams(("parallel", "parallel", "arbitrary")),
        name="attn_a_prompt",
    )(q_rot, k_rot, qkvg, qkvg, lam_p, subln_g.reshape(1, LANES))


def _attn_a_sample_body(q_ref, kp_ref, vp_ref, k_ref, v_ref, g_ref, lam_ref, sg_ref, o_ref,
                        *, tk, lam_init):
    tq = DEC_SEQ
    qs = _stack_maps(q_ref[...])
    carry = _flash_init(2 * tq)

    def step(kt, carry):
        off = pl.multiple_of(kt * tk, tk)
        kb = kp_ref[pl.ds(off, tk), :].astype(BF16)
        vb = vp_ref[pl.ds(off, tk), :].astype(BF16)
        return _flash_step(qs, kb, vb, None, carry)

    carry = lax.fori_loop(0, PAST_LEN // tk, step, carry)
    carry = _flash_step(qs, k_ref[...].astype(BF16), v_ref[...].astype(BF16), None, carry)
    _diff_epilogue(carry, tq, lam_ref, sg_ref, g_ref, o_ref, lam_init)


def _attn_a_sample(q_rot, k_rot, qkvg, k_past, v_past, lam_p, subln_g, lam_init, tk=512):
    r0 = N_PROMPT // DEC_SEQ
    new = lambda c0: pl.BlockSpec((DEC_SEQ, LANES), lambda b, h, c0=c0: (r0 + b, c0 + h))
    past = pl.BlockSpec((PAST_LEN, LANES), lambda b, h: (b, h))
    return pl.pallas_call(
        functools.partial(_attn_a_sample_body, tk=tk, lam_init=lam_init),
        grid=(DEC_BATCH, H_A),
        in_specs=[new(0), past, past, new(0), new(2 * H_A), new(3 * H_A),
                  pl.BlockSpec((4, DH_A), lambda b, h: (0, 0)),
                  pl.BlockSpec((1, LANES), lambda b, h: (0, 0))],
        out_specs=pl.BlockSpec((DEC_SEQ, LANES), lambda b, h: (b, h)),
        out_shape=jax.ShapeDtypeStruct((N_SAMPLE, D_MODEL), F32),
        compiler_params=_cparams(("parallel", "parallel")),
        name="attn_a_sample",
    )(q_rot, k_past, v_past, k_rot, qkvg, qkvg, lam_p, subln_g.reshape(1, LANES))


def _rope_b_body(q_ref, k_ref, qi_ref, sm_ref, c1_ref, s1_ref, c2_ref, s2_ref,
                 qo_ref, ko_ref, qio_ref, kid_ref, wi_ref):
    c1, s1, c2, s2 = c1_ref[...], s1_ref[...], c2_ref[...], s2_ref[...]
    for h in range(H_B):
        sl = slice(h * LANES, (h + 1) * LANES)
        qo_ref[:, sl] = _rope128(q_ref[:, sl], c2, s2) * (HD_B ** -0.5)
    for h in range(KV_B):
        sl = slice(h * LANES, (h + 1) * LANES)
        ko_ref[:, sl] = _rope128(k_ref[:, sl], c2, s2)
    for h in range(H_IDX * D_IDX // LANES):
        sl = slice(h * LANES, (h + 1) * LANES)
        qio_ref[:, sl] = _rope64(qi_ref[:, sl], c1, s1) * (D_IDX ** -0.5)
    sm = sm_ref[...]
    lane = lax.broadcasted_iota(I32, sm.shape, 1)
    kr = _rope64(sm, c1, s1)
    kid_ref[...] = jnp.where(lane < D_IDX, kr, pltpu.roll(kr, D_IDX, 1))
    wi_ref[...] = pltpu.roll(sm, D_IDX, 1) * (H_IDX ** -0.5)


def _rope_b(proj, cos1, sn1, cos2, sn2, tm=256):
    m = proj.shape[0]
    tab = pl.BlockSpec((tm, LANES), lambda i: (i, 0))
    blk = lambda w, c: pl.BlockSpec((tm, w), lambda i, c=c: (i, c))
    widths = (H_B * HD_B, KV_B * HD_B, H_IDX * D_IDX, LANES, LANES)
    return pl.pallas_call(
        _rope_b_body,
        grid=(m // tm,),
        in_specs=[blk(2048, 0), blk(512, B_K0 // 512), blk(1024, B_QI0 // 1024),
                  blk(LANES, B_KI0 // LANES), tab, tab, tab, tab],
        out_specs=[blk(w, 0) for w in widths],
        out_shape=[jax.ShapeDtypeStruct((m, w), F32) for w in widths],
        compiler_params=_cparams(("parallel",)),
        name="rope_b",
    )(proj, proj, proj, proj, cos1, sn1, cos2, sn2)


def _count(pred):
    return jnp.sum(jnp.where(pred, 1.0, 0.0), axis=-1, keepdims=True)


def _select_topk(scores, poss, n_sel):
    keys = []
    for s in scores:
        bits = lax.bitcast_convert_type(s + 0.0, I32)
        keys.append(jnp.where(bits < 0, bits ^ 0x7FFFFFFF, bits))
    tq = scores[0].shape[0]
    n_sel = float(n_sel)

    def thr_step(i, t):
        cand_bits = t | lax.shift_left(jnp.int32(1), 31 - i)
        cand = cand_bits ^ INT_MIN
        cnt = sum(_count(k >= cand) for k in keys)
        return jnp.where(cnt >= n_sel, cand_bits, t)

    t = lax.fori_loop(0, 32, thr_step, jnp.zeros((tq, 1), I32))
    thr = t ^ INT_MIN
    need = n_sel - sum(_count(k > thr) for k in keys)

    def tie_step(i, j):
        cand = j | lax.shift_left(jnp.int32(1), 12 - i)
        cnt = sum(_count((k == thr) & (p < cand)) for k, p in zip(keys, poss))
        return jnp.where(cnt < need, cand, j)

    j = lax.fori_loop(0, 13, tie_step, jnp.zeros((tq, 1), I32))
    return [(k > thr) | ((k == thr) & (p <= j)) for k, p in zip(keys, poss)]


def _dsa_body(*refs, tq, seg_lens, seg_pos0, q_pos0, causal, n_sel):
    nseg = len(seg_lens)
    qi_ref, wi_ref, q_ref, g_ref = refs[:4]
    kid_refs = refs[4:4 + nseg]
    k_refs = refs[4 + nseg:4 + 2 * nseg]
    v_refs = refs[4 + 2 * nseg:4 + 3 * nseg]
    o_ref = refs[4 + 3 * nseg]
    bias_refs = refs[5 + 3 * nseg:]
    assert max(seg_pos0[i] + seg_lens[i] for i in range(nseg)) <= 8192
    n_rep = H_B // KV_B

    @pl.when(pl.program_id(2) == 0)
    def _():
        wi = wi_ref[...]
        qpos = q_pos0 + pl.program_id(1) * tq
        scores, poss, adms = [], [], []
        for si in range(nseg):
            sl = seg_lens[si]
            kid = kid_refs[si][...].astype(BF16)
            sc = jnp.zeros((tq, sl), F32)
            for p in range(H_IDX // 2):
                qp = qi_ref[:, p * LANES:(p + 1) * LANES]
                lane = lax.broadcasted_iota(I32, qp.shape, 1)
                qs = jnp.concatenate([jnp.where(lane < D_IDX, qp, 0.0),
                                      jnp.where(lane >= D_IDX, qp, 0.0)], axis=0).astype(BF16)
                lg = jnp.maximum(_dot_nt(qs, kid), 0.0)
                sc = sc + wi[:, 2 * p:2 * p + 1] * lg[:tq] + wi[:, 2 * p + 1:2 * p + 2] * lg[tq:]
            kpos = seg_pos0[si] + lax.broadcasted_iota(I32, (tq, sl), 1)
            if causal:
                qrow = qpos + lax.broadcasted_iota(I32, (tq, sl), 0)
                adm = _chunk_of(kpos) <= _chunk_of(qrow)
                sc = jnp.where(adm, sc, NEG_INF)
            else:
                adm = None
            scores.append(sc)
            poss.append(kpos)
            adms.append(adm)
        sels = _select_topk(scores, poss, n_sel)
        for si in range(nseg):
            bias = jnp.where(sels[si], 0.0, NEG_INF)
            if adms[si] is not None:
                bias = jnp.where(adms[si], bias, NEG_INF)
            bias_refs[si][...] = bias

    q = q_ref[...]
    qg = jnp.concatenate([q[:, r * LANES:(r + 1) * LANES] for r in range(n_rep)],
                         axis=0).astype(BF16)
    ss = []
    for si in range(nseg):
        b = bias_refs[si][...]
        s = _dot_nt(qg, k_refs[si][...].astype(BF16))
        ss.append(s + jnp.concatenate([b] * n_rep, axis=0))
    m = functools.reduce(jnp.maximum, [jnp.max(s, axis=-1, keepdims=True) for s in ss])
    l = 0.0
    acc = 0.0
    for si in range(nseg):
        p = jnp.exp(ss[si] - m)
        l = l + jnp.sum(p, axis=-1, keepdims=True)
        acc = acc + jnp.dot(p.astype(BF16), v_refs[si][...].astype(BF16),
                            preferred_element_type=F32)
    o = acc / l
    o = jnp.concatenate([o[r * tq:(r + 1) * tq] for r in range(n_rep)], axis=1)
    o_ref[...] = o * _silu(g_ref[...])


def _dsa_prompt(q_rot, k_rot, qi_rot, kid, wi, proj, tq=128):
    nq = SEQ // tq
    gw = KV_B * HD_B
    row = lambda w, c0: pl.BlockSpec((tq, w), lambda b, i, g, c0=c0: (b * nq + i, c0 + g))
    row0 = lambda w: pl.BlockSpec((tq, w), lambda b, i, g: (b * nq + i, 0))
    seq = lambda c0: pl.BlockSpec((SEQ, LANES), lambda b, i, g, c0=c0: (b, c0 + g))
    n_sel = min(TOPK_MAX, SEQ // 4)
    return pl.pallas_call(
        functools.partial(_dsa_body, tq=tq, seg_lens=(SEQ,), seg_pos0=(0,), q_pos0=0,
                          causal=True, n_sel=n_sel),
        grid=(BATCH, nq, KV_B),
        in_specs=[row0(H_IDX * D_IDX), row0(LANES), row(gw, 0), row(gw, B_G0 // gw),
                  pl.BlockSpec((SEQ, LANES), lambda b, i, g: (b, 0)),
                  seq(0), seq(B_V0 // LANES)],
        out_specs=row(gw, 0),
        out_shape=jax.ShapeDtypeStruct((N_PROMPT, D_MODEL), F32),
        scratch_shapes=[pltpu.VMEM((tq, SEQ), F32)],
        compiler_params=_cparams(("parallel", "parallel", "arbitrary")),
        name="dsa_prompt",
    )(qi_rot, wi, q_rot, proj, kid, k_rot, proj)


def _dsa_sample(q_rot, k_rot, qi_rot, kid, wi, proj, k_past, v_past, kid_past):
    tq = DEC_SEQ
    r0 = N_PROMPT // tq
    gw = KV_B * HD_B
    row = lambda w, c0: pl.BlockSpec((tq, w), lambda b, i, g, c0=c0: (r0 + b, c0 + g))
    row0 = lambda w: pl.BlockSpec((tq, w), lambda b, i, g: (r0 + b, 0))
    past = pl.BlockSpec((PAST_LEN, LANES), lambda b, i, g: (b, g))
    s_all = PAST_LEN + DEC_SEQ
    n_sel = min(TOPK_MAX, s_all // 4)
    return pl.pallas_call(
        functools.partial(_dsa_body, tq=tq, seg_lens=(PAST_LEN, DEC_SEQ),
                          seg_pos0=(0, PAST_LEN), q_pos0=PAST_LEN, causal=False, n_sel=n_sel),
        grid=(DEC_BATCH, 1, KV_B),
        in_specs=[row0(H_IDX * D_IDX), row0(LANES), row(gw, 0), row(gw, B_G0 // gw),
                  pl.BlockSpec((PAST_LEN, LANES), lambda b, i, g: (b, 0)), row0(LANES),
                  past, row(LANES, 0),
                  past, row(LANES, B_V0 // LANES)],
        out_specs=pl.BlockSpec((tq, gw), lambda b, i, g: (b, g)),
        out_shape=jax.ShapeDtypeStruct((N_SAMPLE, D_MODEL), F32),
        scratch_shapes=[pltpu.VMEM((tq, PAST_LEN), F32), pltpu.VMEM((tq, DEC_SEQ), F32)],
        compiler_params=_cparams(("parallel", "arbitrary", "arbitrary")),
        name="dsa_sample",
    )(qi_rot, wi, q_rot, proj, kid_past, kid, k_past, k_rot, v_past, proj)


C_TM = 64
C_TILES_PER_SEQ = SEQ // C_TM
C_PROMPT_TILES = N_PROMPT // C_TM


def _c_seq_id(i):
    return jnp.where(i < C_PROMPT_TILES, i // C_TILES_PER_SEQ, i - C_PROMPT_TILES + BATCH)


def _c_prep_body(x_ref, g_ref, mu_ref, sh_ref, l_ref, hl_ref, carry):
    i = pl.program_id(0)

    @pl.when(i == 0)
    def _():
        carry[...] = jnp.zeros_like(carry)

    h = _rms(x_ref[...], g_ref[...])
    start = jnp.logical_or(i >= C_PROMPT_TILES, i % C_TILES_PER_SEQ == 0)
    first = jnp.where(start, sh_ref[0], carry[...])
    row = lax.broadcasted_iota(I32, h.shape, 0)
    prev = jnp.where(row == 0, first, pltpu.roll(h, 1, 0))
    last = h[C_TM - 1:C_TM, :]
    carry[...] = last
    hl_ref[0] = last
    d = prev - h
    for n in range(6):
        l_ref[n] = (h + d * mu_ref[n:n + 1, :]).astype(BF16)


def _c_prep(x, g, mu, shift0):
    m, d = x.shape
    nseq = shift0.shape[0]
    return pl.pallas_call(
        _c_prep_body,
        grid=(m // C_TM,),
        in_specs=[pl.BlockSpec((C_TM, d), lambda i: (i, 0)),
                  pl.BlockSpec((1, d), lambda i: (0, 0)),
                  pl.BlockSpec((6, d), lambda i: (0, 0)),
                  pl.BlockSpec((1, 1, d), lambda i: (_c_seq_id(i), 0, 0))],
        out_specs=[pl.BlockSpec((6, C_TM, d), lambda i: (0, i, 0)),
                   pl.BlockSpec((1, 1, d), lambda i: (_c_seq_id(i), 0, 0))],
        out_shape=[jax.ShapeDtypeStruct((6, m, d), BF16),
                   jax.ShapeDtypeStruct((nseq, 1, d), F32)],
        scratch_shapes=[pltpu.VMEM((1, d), F32)],
        compiler_params=_cparams(("arbitrary",)),
        name="c_prep",
    )(x, g.reshape(1, d), mu, shift0.reshape(nseq, 1, d))


def _bmm_body(l_ref, w_ref, o_ref):
    o_ref[0] = jnp.dot(l_ref[0], w_ref[0], preferred_element_type=F32)


def _c_bmm(lerp, w, tm=512, tn=1024):
    _, m, d = lerp.shape
    nb, _, n = w.shape
    return pl.pallas_call(
        _bmm_body,
        grid=(nb, m // tm, n // tn),
        in_specs=[pl.BlockSpec((1, tm, d), lambda b, i, j: (b, i, 0)),
                  pl.BlockSpec((1, d, tn), lambda b, i, j: (b, 0, j))],
        out_specs=pl.BlockSpec((1, tm, tn), lambda b, i, j: (b, i, j)),
        out_shape=jax.ShapeDtypeStruct((nb, m, n), F32),
        compiler_params=_cparams(("parallel", "parallel", "arbitrary")),
        name="c_bmm",
    )(lerp, w)


def _c_lora_body(l4_ref, l5_ref, wla_ref, wlb_ref, ala_ref, alb_ref, w0_ref, a0_ref,
                 wl_ref, al_ref):
    tw = jnp.tanh(jnp.dot(l4_ref[0], wla_ref[...], preferred_element_type=F32))
    wl_ref[...] = w0_ref[...] + jnp.dot(tw.astype(BF16), wlb_ref[...],
                                        preferred_element_type=F32)
    ta = jnp.dot(l5_ref[0], ala_ref[...], preferred_element_type=F32)
    al_ref[...] = a0_ref[...] + jnp.dot(ta.astype(BF16), alb_ref[...],
                                        preferred_element_type=F32)


def _c_lora(lerp, w_la, w_lb, a_la, a_lb, w0, a0, tm=512):
    _, m, d = lerp.shape
    pad_in = lambda w: jnp.pad(w, ((0, 0), (0, LANES - w.shape[1]))).astype(BF16)
    pad_out = lambda w: jnp.pad(w, ((0, LANES - w.shape[0]), (0, 0))).astype(BF16)
    lin = lambda n: pl.BlockSpec((1, tm, d), lambda i, n=n: (n, i, 0))
    win = pl.BlockSpec((d, LANES), lambda i: (0, 0))
    wout = pl.BlockSpec((LANES, d), lambda i: (0, 0))
    vec = pl.BlockSpec((1, d), lambda i: (0, 0))
    out = pl.BlockSpec((tm, d), lambda i: (i, 0))
    return pl.pallas_call(
        _c_lora_body,
        grid=(m // tm,),
        in_specs=[lin(4), lin(5), win, wout, win, wout, vec, vec],
        out_specs=[out, out],
        out_shape=[jax.ShapeDtypeStruct((m, d), F32)] * 2,
        compiler_params=_cparams(("parallel",)),
        name="c_lora",
    )(lerp, lerp, pad_in(w_la), pad_out(w_lb), pad_in(a_la), pad_out(a_lb),
      w0.reshape(1, d), a0.reshape(1, d))


C_HB = 4
HI = lax.Precision.HIGHEST


def _rwkv_body(*refs, has_init):
    if has_init:
        (r_ref, k_ref, v_ref, g_ref, wl_ref, al_ref, kk_ref, ka_ref, rk_ref, lnw_ref, lnb_ref,
         s0_ref, og_ref, so_ref, s_scr) = refs
    else:
        (r_ref, k_ref, v_ref, g_ref, wl_ref, al_ref, kk_ref, ka_ref, rk_ref, lnw_ref, lnb_ref,
         og_ref, so_ref, s_scr) = refs
    c = pl.program_id(2)
    n = CHUNK
    hs = HS_C

    @pl.when(c == 0)
    def _():
        if has_init:
            s_scr[...] = s0_ref[0]
        else:
            s_scr[...] = jnp.zeros_like(s_scr)

    r, k, v, g = r_ref[0], k_ref[0], v_ref[0], g_ref[0]
    z = -wl_ref[...]
    softplus = jnp.maximum(z, 0.0) + jnp.log(1.0 + jnp.exp(-jnp.abs(z)))
    logw = -jnp.exp(-softplus - 0.5)
    a = _sigmoid(al_ref[...])
    kkr = k * kk_ref[...]
    k2 = k * (1.0 + (a - 1.0) * ka_ref[...])
    bonus_in = r * k2 * rk_ref[...]

    ti = lax.broadcasted_iota(I32, (n, n), 0)
    tj = lax.broadcasted_iota(I32, (n, n), 1)
    tri_incl = jnp.where(tj <= ti, 1.0, 0.0)
    eye = jnp.where(tj == ti, 1.0, 0.0)
    gi = lax.broadcasted_iota(I32, (2 * n, 2 * n), 0)
    gj = lax.broadcasted_iota(I32, (2 * n, 2 * n), 1) & (n - 1)
    gmask = ((gi < n) & (gj < gi)) | ((gi >= n) & (gj <= gi - n))
    lvl_masks = []
    bs = 1
    while bs < n:
        sh = bs.bit_length()
        lvl_masks.append((jnp.right_shift(ti, sh) == jnp.right_shift(tj, sh))
                         & ((ti & bs) != 0) & ((tj & bs) == 0))
        bs *= 2

    outs = []
    for j in range(C_HB):
        sl = slice(j * hs, (j + 1) * hs)
        rj, k2j, vj, aj, lw = r[:, sl], k2[:, sl], v[:, sl], a[:, sl], logw[:, sl]
        kkj = kkr[:, sl]
        kkj = kkj / jnp.maximum(jnp.sqrt(jnp.sum(kkj * kkj, axis=-1, keepdims=True)), 1e-12)
        cs = jnp.dot(tri_incl, lw, precision=HI, preferred_element_type=F32)
        ec, eci, ecp = jnp.exp(cs), jnp.exp(-cs), jnp.exp(cs - lw)
        p_, q_ = kkj * ecp, kkj * aj * eci
        kt, rt = k2j * eci, rj * ec
        pr = jnp.concatenate([p_, rt], axis=0)
        qk = jnp.concatenate([q_, kt], axis=0)
        gm = jnp.where(gmask, _dot_nt(pr, qk, HI), 0.0)
        s0 = s_scr[j]
        ps = _dot_nt(pr, s0, HI)
        zv = jnp.concatenate([jnp.zeros_like(vj), vj], axis=0)
        rhs = ps[:n] + jnp.dot(gm[:n], zv, precision=HI, preferred_element_type=F32)
        a_qp = gm[:n, :n]
        x = eye
        for msk in lvl_masks:
            ax = jnp.dot(jnp.where(msk, a_qp, 0.0), x, precision=HI, preferred_element_type=F32)
            x = x - jnp.dot(x, ax, precision=HI, preferred_element_type=F32)
        u = jnp.dot(x, rhs, precision=HI, preferred_element_type=F32)
        o = ps[n:] + jnp.dot(gm[n:], jnp.concatenate([-u, vj], axis=0), precision=HI,
                             preferred_element_type=F32)
        vu = jnp.concatenate([vj, u], axis=0)
        kq = jnp.concatenate([kt, -q_], axis=0)
        ds = lax.dot_general(vu, kq, (((0,), (0,)), ((), ())), precision=HI,
                             preferred_element_type=F32)
        s_scr[j] = (s0 + ds) * ec[n - 1:n, :]
        mean = jnp.mean(o, axis=-1, keepdims=True)
        var = jnp.mean(jnp.square(o - mean), axis=-1, keepdims=True)
        on = (o - mean) * lax.rsqrt(var + GN_EPS) * lnw_ref[:, sl] + lnb_ref[:, sl]
        bonus = jnp.sum(bonus_in[:, sl], axis=-1, keepdims=True) * vj
        outs.append(on + bonus)
    og_ref[...] = jnp.concatenate(outs, axis=1) * _silu(g)

    @pl.when(c == pl.num_programs(2) - 1)
    def _():
        so_ref[0] = s_scr[...]


def _rwkv(rkvg, wl, al, k_k, k_a, r_k, ln_w, ln_b, s0, *, nseq, nchunk, row0):
    w = C_HB * HS_C
    tok = lambda n: pl.BlockSpec((1, CHUNK, w), lambda s, h, c, n=n: (n, row0 + s * nchunk + c, h))
    tok2 = pl.BlockSpec((CHUNK, w), lambda s, h, c: (row0 + s * nchunk + c, h))
    vec = pl.BlockSpec((1, w), lambda s, h, c: (0, h))
    st = pl.BlockSpec((1, C_HB, HS_C, HS_C), lambda s, h, c: (s, h, 0, 0))
    has_init = s0 is not None
    ins = [rkvg, rkvg, rkvg, rkvg, wl, al] + [p.reshape(1, D_MODEL) for p in (k_k, k_a, r_k, ln_w, ln_b)]
    specs = [tok(0), tok(1), tok(2), tok(3), tok2, tok2] + [vec] * 5
    if has_init:
        ins.append(s0)
        specs.append(st)
    return pl.pallas_call(
        functools.partial(_rwkv_body, has_init=has_init),
        grid=(nseq, H_C // C_HB, nchunk),
        in_specs=specs,
        out_specs=[pl.BlockSpec((CHUNK, w), lambda s, h, c: (s * nchunk + c, h)), st],
        out_shape=[jax.ShapeDtypeStruct((nseq * nchunk * CHUNK, D_MODEL), F32),
                   jax.ShapeDtypeStruct((nseq, H_C, HS_C, HS_C), F32)],
        scratch_shapes=[pltpu.VMEM((C_HB, HS_C, HS_C), F32)],
        compiler_params=_cparams(("parallel", "parallel", "arbitrary")),
        name="rwkv_scan",
    )(*ins)


def kernel(x_prompt, x_sample, cache_a_k, cache_a_v, cache_b_k, cache_b_v, cache_b_kidx, state_c_wkv, state_c_shift, norm_g, final_g, w_out, a_w_in, a_lam, a_subln_g, b_w_in, c_mu, c_w_rkvg, c_w0, c_w_la, c_w_lb, c_a0, c_a_la, c_a_lb, c_k_k, c_k_a, c_r_k, c_ln_w, c_ln_b):
    x = jnp.concatenate([x_prompt.reshape(N_PROMPT, D_MODEL),
                         x_sample.reshape(N_SAMPLE, D_MODEL)], axis=0)
    cos1, sn1, cos2, sn2 = _rope_tables()
    split = lambda arr: (arr[:N_PROMPT], arr[N_PROMPT:])
    outs = {n: [] for n in ("akp", "avp", "aks", "avs", "bkp", "bvp", "bip", "bks", "bvs", "bis",
                            "cwp", "chp", "cws", "chs")}
    y = None
    for i in range(DEPTH):
        kind, j = i % N_MIXERS, i // N_MIXERS
        if kind == 0:
            lam_init = 0.8 - 0.6 * math.exp(-0.3 * i)
            qkvg = _norm_proj(x, norm_g[i], a_w_in[j].astype(BF16), 512, 1024)
            q_rot, k_rot = _rope_a(qkvg, cos1, sn1)
            og_p = _attn_a_prompt(q_rot, k_rot, qkvg, a_lam[j], a_subln_g[j], lam_init)
            og_s = _attn_a_sample(q_rot, k_rot, qkvg,
                                  cache_a_k[j].reshape(DEC_BATCH * PAST_LEN, D_MODEL),
                                  cache_a_v[j].reshape(DEC_BATCH * PAST_LEN, D_MODEL),
                                  a_lam[j], a_subln_g[j], lam_init)
            og = jnp.concatenate([og_p, og_s], axis=0)
            kp, ks = split(k_rot)
            vp, vs = split(qkvg[:, 2 * D_MODEL:3 * D_MODEL])
            outs["akp"].append(kp.reshape(BATCH, SEQ, H_A, 2 * DH_A))
            outs["avp"].append(vp.reshape(BATCH, SEQ, H_A, 2 * DH_A))
            outs["aks"].append(ks.reshape(DEC_BATCH, DEC_SEQ, H_A, 2 * DH_A))
            outs["avs"].append(vs.reshape(DEC_BATCH, DEC_SEQ, H_A, 2 * DH_A))
        elif kind == 1:
            w = b_w_in[j]
            w = jnp.concatenate([w[:, :4096], w[:, 4176:], w[:, 4096:4176],
                                 jnp.zeros((D_MODEL, B_COLS - w.shape[1]), w.dtype)], axis=1)
            proj = _norm_proj(x, norm_g[i], w.astype(BF16), 512, 896)
            q_rot, k_rot, qi_rot, kid, wi = _rope_b(proj, cos1, sn1, cos2, sn2)
            og_p = _dsa_prompt(q_rot, k_rot, qi_rot, kid, wi, proj)
            kidx_past = cache_b_kidx[j].reshape(DEC_BATCH * PAST_LEN, D_IDX)
            og_s = _dsa_sample(q_rot, k_rot, qi_rot, kid, wi, proj,
                               cache_b_k[j].reshape(DEC_BATCH * PAST_LEN, KV_B * HD_B),
                               cache_b_v[j].reshape(DEC_BATCH * PAST_LEN, KV_B * HD_B),
                               jnp.concatenate([kidx_past, kidx_past], axis=1))
            og = jnp.concatenate([og_p, og_s], axis=0)
            kp, ks = split(k_rot)
            vp, vs = split(proj[:, B_V0:B_V0 + KV_B * HD_B])
            ip, is_ = split(kid[:, :D_IDX])
            outs["bkp"].append(kp.reshape(BATCH, SEQ, KV_B, HD_B))
            outs["bvp"].append(vp.reshape(BATCH, SEQ, KV_B, HD_B))
            outs["bip"].append(ip.reshape(BATCH, SEQ, D_IDX))
            outs["bks"].append(ks.reshape(DEC_BATCH, DEC_SEQ, KV_B, HD_B))
            outs["bvs"].append(vs.reshape(DEC_BATCH, DEC_SEQ, KV_B, HD_B))
            outs["bis"].append(is_.reshape(DEC_BATCH, DEC_SEQ, D_IDX))
        else:
            shift0 = jnp.concatenate([jnp.zeros((BATCH, D_MODEL), F32), state_c_shift[j]], axis=0)
            lerp, hlast = _c_prep(x, norm_g[i], c_mu[j], shift0)
            rkvg = _c_bmm(lerp, c_w_rkvg[j].astype(BF16))
            wl, al = _c_lora(lerp, c_w_la[j], c_w_lb[j], c_a_la[j], c_a_lb[j], c_w0[j], c_a0[j])
            par = (c_k_k[j], c_k_a[j], c_r_k[j], c_ln_w[j], c_ln_b[j])
            og_p, st_p = _rwkv(rkvg, wl, al, *par, None, nseq=BATCH, nchunk=SEQ // CHUNK, row0=0)
            og_s, st_s = _rwkv(rkvg, wl, al, *par, state_c_wkv[j], nseq=DEC_BATCH, nchunk=1,
                               row0=N_PROMPT // CHUNK)
            og = jnp.concatenate([og_p, og_s], axis=0)
            outs["cwp"].append(st_p)
            outs["chp"].append(hlast[:BATCH, 0])
            outs["cws"].append(st_s)
            outs["chs"].append(hlast[BATCH:, 0])
        if i == DEPTH - 1:
            y = _proj_out(og, x, w_out[i].astype(BF16), final_g)
        else:
            x = _proj_out(og, x, w_out[i].astype(BF16))
    yp, ys = split(y)
    st = lambda n: jnp.stack(outs[n])
    return (yp.reshape(BATCH, SEQ, D_MODEL), ys.reshape(DEC_BATCH, DEC_SEQ, D_MODEL),
            st("akp"), st("avp"), st("aks"), st("avs"),
            st("bkp"), st("bvp"), st("bip"), st("bks"), st("bvs"), st("bis"),
            st("cwp"), st("chp"), st("cws"), st("chs"))
```

```python
import functools
import math

import jax
import jax.numpy as jnp
from jax import lax
from jax.experimental import pallas as pl
from jax.experimental.pallas import tpu as pltpu

F32 = jnp.float32
BF16 = jnp.bfloat16
I32 = jnp.int32

D_MODEL = 2048
BATCH = 4
SEQ = 2048
DEPTH = 4
DEC_BATCH = 8
DEC_SEQ = 64
PAST_LEN = 4096
CHUNK = 64
N_MIXERS = 3
ROPE_THETA = 10000.0
RMS_EPS = 1e-6
DH_A = 64
H_A = 16
SUBLN_EPS = 1e-5
HD_B = 128
H_B = 16
KV_B = 4
H_IDX = 16
D_IDX = 64
TOPK_MAX = 256
HS_C = 64
H_C = 32
R_DECAY = 96
R_ICLR = 96
GN_EPS = 64e-5

N_PROMPT = BATCH * SEQ
N_SAMPLE = DEC_BATCH * DEC_SEQ
M_ROWS = N_PROMPT + N_SAMPLE
LANES = 128
VMEM_LIMIT = 56 * 1024 * 1024
NEG_INF = float("-inf")
INT_MIN = -2 ** 31

B_Q0, B_K0, B_V0, B_QI0, B_G0, B_KI0 = 0, 2048, 2560, 3072, 4096, 6144
B_COLS = 6272

assert PAST_LEN % CHUNK == 0 and DEC_SEQ == CHUNK
LOG2_CHUNK = CHUNK.bit_length() - 1
assert 1 << LOG2_CHUNK == CHUNK


def _chunk_of(pos):
    return jnp.right_shift(pos, LOG2_CHUNK)


def _cparams(sem):
    return pltpu.CompilerParams(dimension_semantics=sem, vmem_limit_bytes=VMEM_LIMIT)


def _sigmoid(x):
    return 1.0 / (1.0 + jnp.exp(-x))


def _silu(x):
    return x * _sigmoid(x)


def _rms(x, g):
    return x * lax.rsqrt(jnp.mean(x * x, axis=-1, keepdims=True) + RMS_EPS) * g


def _dot_nt(a, b, precision=None):
    return lax.dot_general(a, b, (((1,), (1,)), ((), ())), precision=precision,
                           preferred_element_type=F32)


def _norm_proj_body(x_ref, g_ref, w_ref, o_ref, h_scr):
    @pl.when(pl.program_id(1) == 0)
    def _():
        h_scr[...] = _rms(x_ref[...], g_ref[...]).astype(BF16)
    o_ref[...] = jnp.dot(h_scr[...], w_ref[...], preferred_element_type=F32)


def _norm_proj(x, g, w, tm, tn):
    m, d = x.shape
    n = w.shape[1]
    return pl.pallas_call(
        _norm_proj_body,
        grid=(m // tm, n // tn),
        in_specs=[pl.BlockSpec((tm, d), lambda i, j: (i, 0)),
                  pl.BlockSpec((1, d), lambda i, j: (0, 0)),
                  pl.BlockSpec((d, tn), lambda i, j: (0, j))],
        out_specs=pl.BlockSpec((tm, tn), lambda i, j: (i, j)),
        out_shape=jax.ShapeDtypeStruct((m, n), F32),
        scratch_shapes=[pltpu.VMEM((tm, d), BF16)],
        compiler_params=_cparams(("parallel", "arbitrary")),
        name="norm_proj",
    )(x, g.reshape(1, d), w)


def _proj_out_body(og_ref, x_ref, w_ref, o_ref):
    o_ref[...] = x_ref[...] + jnp.dot(og_ref[...].astype(BF16), w_ref[...],
                                      preferred_element_type=F32)


def _proj_out_final_body(og_ref, x_ref, w_ref, g_ref, o_ref):
    xn = x_ref[...] + jnp.dot(og_ref[...].astype(BF16), w_ref[...],
                              preferred_element_type=F32)
    o_ref[...] = _rms(xn, g_ref[...])


def _proj_out(og, x, w, final_g=None, tm=256):
    m, d = x.shape
    row = pl.BlockSpec((tm, d), lambda i: (i, 0))
    full = pl.BlockSpec((d, d), lambda i: (0, 0))
    if final_g is None:
        body, extra, extra_specs = _proj_out_body, (), []
    else:
        body, extra = _proj_out_final_body, (final_g.reshape(1, d),)
        extra_specs = [pl.BlockSpec((1, d), lambda i: (0, 0))]
    return pl.pallas_call(
        body,
        grid=(m // tm,),
        in_specs=[row, row, full] + extra_specs,
        out_specs=row,
        out_shape=jax.ShapeDtypeStruct((m, d), F32),
        compiler_params=_cparams(("parallel",)),
        name="proj_out",
    )(og, x, w, *extra)


def _rope_tables():
    pos = jnp.concatenate([jnp.tile(jnp.arange(SEQ), BATCH),
                           jnp.tile(PAST_LEN + jnp.arange(DEC_SEQ), DEC_BATCH)]).astype(F32)

    def table(dh, reps):
        half = dh // 2
        inv = jnp.power(ROPE_THETA, -jnp.arange(half, dtype=F32) * (2.0 / dh))
        ang = pos[:, None] * inv[None, :]
        cos, sin = jnp.cos(ang), jnp.sin(ang)
        return (jnp.tile(jnp.concatenate([cos, cos], axis=1), (1, reps)),
                jnp.tile(jnp.concatenate([-sin, sin], axis=1), (1, reps)))

    cos1, sn1 = table(64, 2)
    cos2, sn2 = table(128, 1)
    return cos1, sn1, cos2, sn2


def _rope64(x, cos, sn):
    lane = lax.broadcasted_iota(I32, x.shape, 1)
    partner = jnp.where((lane & 63) < 32, pltpu.roll(x, 96, 1), pltpu.roll(x, 32, 1))
    return x * cos + partner * sn


def _rope128(x, cos, sn):
    return x * cos + pltpu.roll(x, 64, 1) * sn


def _rope_a_body(q_ref, k_ref, cos_ref, sn_ref, qo_ref, ko_ref):
    cos, sn = cos_ref[...], sn_ref[...]
    for h in range(H_A):
        sl = slice(h * LANES, (h + 1) * LANES)
        qo_ref[:, sl] = _rope64(q_ref[:, sl], cos, sn) * (DH_A ** -0.5)
        ko_ref[:, sl] = _rope64(k_ref[:, sl], cos, sn)


def _rope_a(qkvg, cos1, sn1, tm=256):
    m = qkvg.shape[0]
    blk = lambda c: pl.BlockSpec((tm, D_MODEL), lambda i, c=c: (i, c))
    tab = pl.BlockSpec((tm, LANES), lambda i: (i, 0))
    return pl.pallas_call(
        _rope_a_body,
        grid=(m // tm,),
        in_specs=[blk(0), blk(1), tab, tab],
        out_specs=[blk(0), blk(0)],
        out_shape=[jax.ShapeDtypeStruct((m, D_MODEL), F32)] * 2,
        compiler_params=_cparams(("parallel",)),
        name="rope_a",
    )(qkvg, qkvg, cos1, sn1)


def _stack_maps(q):
    lane = lax.broadcasted_iota(I32, q.shape, 1)
    return jnp.concatenate([jnp.where(lane < DH_A, q, 0.0),
                            jnp.where(lane >= DH_A, q, 0.0)], axis=0).astype(BF16)


def _flash_step(qs, kt, vt, mask, carry):
    m, l, acc = carry
    s = _dot_nt(qs, kt)
    if mask is not None:
        s = jnp.where(mask, s, NEG_INF)
    m_new = jnp.maximum(m, jnp.max(s, axis=-1, keepdims=True))
    alpha = jnp.exp(m - m_new)
    p = jnp.exp(s - m_new)
    l = alpha * l + jnp.sum(p, axis=-1, keepdims=True)
    acc = alpha * acc + jnp.dot(p.astype(BF16), vt, preferred_element_type=F32)
    return m_new, l, acc


def _flash_init(rows):
    return (jnp.full((rows, 1), -1e30, F32), jnp.zeros((rows, 1), F32),
            jnp.zeros((rows, LANES), F32))


def _diff_epilogue(carry, tq, lam_ref, sg_ref, g_ref, o_ref, lam_init):
    _, l, acc = carry
    o1 = acc[:tq] / l[:tq]
    o2 = acc[tq:] / l[tq:]
    lp = lam_ref[...]
    lam = (jnp.exp(jnp.sum(lp[0:1] * lp[1:2], axis=-1, keepdims=True))
           - jnp.exp(jnp.sum(lp[2:3] * lp[3:4], axis=-1, keepdims=True)) + lam_init)
    o = o1 - lam * o2
    o = o * lax.rsqrt(jnp.mean(o * o, axis=-1, keepdims=True) + SUBLN_EPS) * sg_ref[...]
    o = o * (1.0 - lam_init)
    o_ref[...] = o * _silu(g_ref[...])


def _attn_a_prompt_body(q_ref, k_ref, v_ref, g_ref, lam_ref, sg_ref, o_ref, *, tq, lam_init):
    qi = pl.program_id(2)
    qs = _stack_maps(q_ref[...])
    rows = 2 * tq
    qchunk = _chunk_of(qi * tq + (lax.broadcasted_iota(I32, (rows, tq), 0) & (tq - 1)))
    kcol = lax.broadcasted_iota(I32, (rows, tq), 1)

    def step(kt, carry):
        off = pl.multiple_of(kt * tq, tq)
        kb = k_ref[pl.ds(off, tq), :].astype(BF16)
        vb = v_ref[pl.ds(off, tq), :].astype(BF16)
        mask = _chunk_of(kt * tq + kcol) <= qchunk
        return _flash_step(qs, kb, vb, mask, carry)

    carry = lax.fori_loop(0, qi + 1, step, _flash_init(rows))
    _diff_epilogue(carry, tq, lam_ref, sg_ref, g_ref, o_ref, lam_init)


def _attn_a_prompt(q_rot, k_rot, qkvg, lam_p, subln_g, lam_init, tq=256):
    nq = SEQ // tq
    return pl.pallas_call(
        functools.partial(_attn_a_prompt_body, tq=tq, lam_init=lam_init),
        grid=(BATCH, H_A, nq),
        in_specs=[pl.BlockSpec((tq, LANES), lambda b, h, i: (b * nq + i, h)),
                  pl.BlockSpec((SEQ, LANES), lambda b, h, i: (b, h)),
                  pl.BlockSpec((SEQ, LANES), lambda b, h, i: (b, 2 * H_A + h)),
                  pl.BlockSpec((tq, LANES), lambda b, h, i: (b * nq + i, 3 * H_A + h)),
                  pl.BlockSpec((4, DH_A), lambda b, h, i: (0, 0)),
                  pl.BlockSpec((1, LANES), lambda b, h, i: (0, 0))],
        out_specs=pl.BlockSpec((tq, LANES), lambda b, h, i: (b * nq + i, h)),
        out_shape=jax.ShapeDtypeStruct((M_ROWS, D_MODEL), F32),
        compiler_params=_cparams(("parallel", "parallel", "arbitrary")),
        name="attn_a_prompt",
    )(q_rot, k_rot, qkvg, qkvg, lam_p, subln_g.reshape(1, LANES))


def _attn_a_sample_body(q_ref, kp_ref, vp_ref, k_ref, v_ref, g_ref, lam_ref, sg_ref, _og_prompt,
                        o_ref, *, tk, lam_init):
    tq = DEC_SEQ
    qs = _stack_maps(q_ref[...])
    carry = _flash_init(2 * tq)

    def step(kt, carry):
        off = pl.multiple_of(kt * tk, tk)
        kb = kp_ref[pl.ds(off, tk), :].astype(BF16)
        vb = vp_ref[pl.ds(off, tk), :].astype(BF16)
        return _flash_step(qs, kb, vb, None, carry)

    carry = lax.fori_loop(0, PAST_LEN // tk, step, carry)
    carry = _flash_step(qs, k_ref[...].astype(BF16), v_ref[...].astype(BF16), None, carry)
    _diff_epilogue(carry, tq, lam_ref, sg_ref, g_ref, o_ref, lam_init)


def _attn_a_sample(q_rot, k_rot, qkvg, k_past, v_past, layer, lam_p, subln_g, lam_init, og_prompt,
                   tk=512):
    r0 = N_PROMPT // DEC_SEQ
    new = lambda c0: pl.BlockSpec((DEC_SEQ, LANES), lambda b, h, c0=c0: (r0 + b, c0 + h))
    past = pl.BlockSpec((PAST_LEN, LANES), lambda b, h: (layer * DEC_BATCH + b, h))
    return pl.pallas_call(
        functools.partial(_attn_a_sample_body, tk=tk, lam_init=lam_init),
        grid=(DEC_BATCH, H_A),
        in_specs=[new(0), past, past, new(0), new(2 * H_A), new(3 * H_A),
                  pl.BlockSpec((4, DH_A), lambda b, h: (0, 0)),
                  pl.BlockSpec((1, LANES), lambda b, h: (0, 0)),
                  pl.BlockSpec(memory_space=pl.ANY)],
        out_specs=new(0),
        out_shape=jax.ShapeDtypeStruct((M_ROWS, D_MODEL), F32),
        input_output_aliases={8: 0},
        compiler_params=_cparams(("parallel", "parallel")),
        name="attn_a_sample",
    )(q_rot, k_past, v_past, k_rot, qkvg, qkvg, lam_p, subln_g.reshape(1, LANES), og_prompt)


def _rope_b_body(q_ref, k_ref, qi_ref, sm_ref, c1_ref, s1_ref, c2_ref, s2_ref,
                 qo_ref, ko_ref, qio_ref, kid_ref, wi_ref):
    c1, s1, c2, s2 = c1_ref[...], s1_ref[...], c2_ref[...], s2_ref[...]
    for h in range(H_B):
        sl = slice(h * LANES, (h + 1) * LANES)
        qo_ref[:, sl] = _rope128(q_ref[:, sl], c2, s2) * (HD_B ** -0.5)
    for h in range(KV_B):
        sl = slice(h * LANES, (h + 1) * LANES)
        ko_ref[:, sl] = _rope128(k_ref[:, sl], c2, s2)
    for h in range(H_IDX * D_IDX // LANES):
        sl = slice(h * LANES, (h + 1) * LANES)
        qio_ref[:, sl] = _rope64(qi_ref[:, sl], c1, s1) * (D_IDX ** -0.5)
    sm = sm_ref[...]
    lane = lax.broadcasted_iota(I32, sm.shape, 1)
    kr = _rope64(sm, c1, s1)
    kid_ref[...] = jnp.where(lane < D_IDX, kr, pltpu.roll(kr, D_IDX, 1))
    wi_ref[...] = pltpu.roll(sm, D_IDX, 1) * (H_IDX ** -0.5)


def _rope_b(proj, cos1, sn1, cos2, sn2, tm=256):
    m = proj.shape[0]
    tab = pl.BlockSpec((tm, LANES), lambda i: (i, 0))
    blk = lambda w, c: pl.BlockSpec((tm, w), lambda i, c=c: (i, c))
    widths = (H_B * HD_B, KV_B * HD_B, H_IDX * D_IDX, LANES, LANES)
    return pl.pallas_call(
        _rope_b_body,
        grid=(m // tm,),
        in_specs=[blk(2048, 0), blk(512, B_K0 // 512), blk(1024, B_QI0 // 1024),
                  blk(LANES, B_KI0 // LANES), tab, tab, tab, tab],
        out_specs=[blk(w, 0) for w in widths],
        out_shape=[jax.ShapeDtypeStruct((m, w), F32) for w in widths],
        compiler_params=_cparams(("parallel",)),
        name="rope_b",
    )(proj, proj, proj, proj, cos1, sn1, cos2, sn2)


def _count(pred):
    return jnp.sum(jnp.where(pred, 1.0, 0.0), axis=-1, keepdims=True)


def _select_topk(scores, poss, n_sel):
    keys = []
    for s in scores:
        bits = lax.bitcast_convert_type(s + 0.0, I32)
        keys.append(jnp.where(bits < 0, bits ^ 0x7FFFFFFF, bits))
    tq = scores[0].shape[0]
    n_sel = float(n_sel)

    def thr_step(i, t):
        cand_bits = t | lax.shift_left(jnp.int32(1), 31 - i)
        cand = cand_bits ^ INT_MIN
        cnt = sum(_count(k >= cand) for k in keys)
        return jnp.where(cnt >= n_sel, cand_bits, t)

    t = lax.fori_loop(0, 32, thr_step, jnp.zeros((tq, 1), I32))
    thr = t ^ INT_MIN
    need = n_sel - sum(_count(k > thr) for k in keys)

    def tie_step(i, j):
        cand = j | lax.shift_left(jnp.int32(1), 12 - i)
        cnt = sum(_count((k == thr) & (p < cand)) for k, p in zip(keys, poss))
        return jnp.where(cnt < need, cand, j)

    j = lax.fori_loop(0, 13, tie_step, jnp.zeros((tq, 1), I32))
    return [(k > thr) | ((k == thr) & (p <= j)) for k, p in zip(keys, poss)]


def _dsa_body(*refs, tq, seg_lens, seg_pos0, q_pos0, causal, n_sel, n_aliased):
    nseg = len(seg_lens)
    qi_ref, wi_ref, q_ref, g_ref = refs[:4]
    kid_refs = refs[4:4 + nseg]
    k_refs = refs[4 + nseg:4 + 2 * nseg]
    v_refs = refs[4 + 2 * nseg:4 + 3 * nseg]
    n_in = 4 + 3 * nseg + n_aliased
    o_ref = refs[n_in]
    bias_refs = refs[n_in + 1:]
    assert max(seg_pos0[i] + seg_lens[i] for i in range(nseg)) <= 8192
    n_rep = H_B // KV_B

    @pl.when(pl.program_id(2) == 0)
    def _():
        wi = wi_ref[...]
        qpos = q_pos0 + pl.program_id(1) * tq
        scores, poss, adms = [], [], []
        for si in range(nseg):
            sl = seg_lens[si]
            kid = kid_refs[si][...].astype(BF16)
            sc = jnp.zeros((tq, sl), F32)
            for p in range(H_IDX // 2):
                qp = qi_ref[:, p * LANES:(p + 1) * LANES]
                lane = lax.broadcasted_iota(I32, qp.shape, 1)
                qs = jnp.concatenate([jnp.where(lane < D_IDX, qp, 0.0),
                                      jnp.where(lane >= D_IDX, qp, 0.0)], axis=0).astype(BF16)
                lg = jnp.maximum(_dot_nt(qs, kid), 0.0)
                sc = sc + wi[:, 2 * p:2 * p + 1] * lg[:tq] + wi[:, 2 * p + 1:2 * p + 2] * lg[tq:]
            kpos = seg_pos0[si] + lax.broadcasted_iota(I32, (tq, sl), 1)
            if causal:
                qrow = qpos + lax.broadcasted_iota(I32, (tq, sl), 0)
                adm = _chunk_of(kpos) <= _chunk_of(qrow)
                sc = jnp.where(adm, sc, NEG_INF)
            else:
                adm = None
            scores.append(sc)
            poss.append(kpos)
            adms.append(adm)
        sels = _select_topk(scores, poss, n_sel)
        for si in range(nseg):
            bias = jnp.where(sels[si], 0.0, NEG_INF)
            if adms[si] is not None:
                bias = jnp.where(adms[si], bias, NEG_INF)
            bias_refs[si][...] = bias

    q = q_ref[...]
    qg = jnp.concatenate([q[:, r * LANES:(r + 1) * LANES] for r in range(n_rep)],
                         axis=0).astype(BF16)
    ss = []
    for si in range(nseg):
        b = bias_refs[si][...]
        s = _dot_nt(qg, k_refs[si][...].astype(BF16))
        ss.append(s + jnp.concatenate([b] * n_rep, axis=0))
    m = functools.reduce(jnp.maximum, [jnp.max(s, axis=-1, keepdims=True) for s in ss])
    l = 0.0
    acc = 0.0
    for si in range(nseg):
        p = jnp.exp(ss[si] - m)
        l = l + jnp.sum(p, axis=-1, keepdims=True)
        acc = acc + jnp.dot(p.astype(BF16), v_refs[si][...].astype(BF16),
                            preferred_element_type=F32)
    o = acc / l
    o = jnp.concatenate([o[r * tq:(r + 1) * tq] for r in range(n_rep)], axis=1)
    o_ref[...] = o * _silu(g_ref[...])


def _dsa_prompt(q_rot, k_rot, qi_rot, kid, wi, proj, tq=128):
    nq = SEQ // tq
    gw = KV_B * HD_B
    row = lambda w, c0: pl.BlockSpec((tq, w), lambda b, i, g, c0=c0: (b * nq + i, c0 + g))
    row0 = lambda w: pl.BlockSpec((tq, w), lambda b, i, g: (b * nq + i, 0))
    seq = lambda c0: pl.BlockSpec((SEQ, LANES), lambda b, i, g, c0=c0: (b, c0 + g))
    n_sel = min(TOPK_MAX, SEQ // 4)
    return pl.pallas_call(
        functools.partial(_dsa_body, tq=tq, seg_lens=(SEQ,), seg_pos0=(0,), q_pos0=0,
                          causal=True, n_sel=n_sel, n_aliased=0),
        grid=(BATCH, nq, KV_B),
        in_specs=[row0(H_IDX * D_IDX), row0(LANES), row(gw, 0), row(gw, B_G0 // gw),
                  pl.BlockSpec((SEQ, LANES), lambda b, i, g: (b, 0)),
                  seq(0), seq(B_V0 // LANES)],
        out_specs=row(gw, 0),
        out_shape=jax.ShapeDtypeStruct((M_ROWS, D_MODEL), F32),
        scratch_shapes=[pltpu.VMEM((tq, SEQ), F32)],
        compiler_params=_cparams(("parallel", "parallel", "arbitrary")),
        name="dsa_prompt",
    )(qi_rot, wi, q_rot, proj, kid, k_rot, proj)


def _dsa_sample(q_rot, k_rot, qi_rot, kid, wi, proj, k_past, v_past, kid_past, layer, og_prompt):
    tq = DEC_SEQ
    r0 = N_PROMPT // tq
    gw = KV_B * HD_B
    row = lambda w, c0: pl.BlockSpec((tq, w), lambda b, i, g, c0=c0: (r0 + b, c0 + g))
    row0 = lambda w: pl.BlockSpec((tq, w), lambda b, i, g: (r0 + b, 0))
    past = pl.BlockSpec((PAST_LEN, LANES), lambda b, i, g: (layer * DEC_BATCH + b, g))
    s_all = PAST_LEN + DEC_SEQ
    n_sel = min(TOPK_MAX, s_all // 4)
    return pl.pallas_call(
        functools.partial(_dsa_body, tq=tq, seg_lens=(PAST_LEN, DEC_SEQ),
                          seg_pos0=(0, PAST_LEN), q_pos0=PAST_LEN, causal=False, n_sel=n_sel,
                          n_aliased=1),
        grid=(DEC_BATCH, 1, KV_B),
        in_specs=[row0(H_IDX * D_IDX), row0(LANES), row(gw, 0), row(gw, B_G0 // gw),
                  pl.BlockSpec((PAST_LEN, LANES), lambda b, i, g: (layer * DEC_BATCH + b, 0)),
                  row0(LANES),
                  past, row(LANES, 0),
                  past, row(LANES, B_V0 // LANES),
                  pl.BlockSpec(memory_space=pl.ANY)],
        out_specs=row(gw, 0),
        out_shape=jax.ShapeDtypeStruct((M_ROWS, D_MODEL), F32),
        input_output_aliases={10: 0},
        scratch_shapes=[pltpu.VMEM((tq, PAST_LEN), F32), pltpu.VMEM((tq, DEC_SEQ), F32)],
        compiler_params=_cparams(("parallel", "arbitrary", "arbitrary")),
        name="dsa_sample",
    )(qi_rot, wi, q_rot, proj, kid_past, kid, k_past, k_rot, v_past, proj, og_prompt)


C_TM = 64
C_TILES_PER_SEQ = SEQ // C_TM
C_PROMPT_TILES = N_PROMPT // C_TM


def _c_seq_id(i):
    return jnp.where(i < C_PROMPT_TILES, i // C_TILES_PER_SEQ, i - C_PROMPT_TILES + BATCH)


def _c_prep_body(x_ref, g_ref, mu_ref, sh_ref, l_ref, hl_ref, carry):
    i = pl.program_id(0)

    @pl.when(i == 0)
    def _():
        carry[...] = jnp.zeros_like(carry)

    h = _rms(x_ref[...], g_ref[...])
    start = jnp.logical_or(i >= C_PROMPT_TILES, i % C_TILES_PER_SEQ == 0)
    first = jnp.where(start, sh_ref[0], carry[...])
    row = lax.broadcasted_iota(I32, h.shape, 0)
    prev = jnp.where(row == 0, first, pltpu.roll(h, 1, 0))
    last = h[C_TM - 1:C_TM, :]
    carry[...] = last
    hl_ref[0] = last
    d = prev - h
    for n in range(6):
        l_ref[n] = (h + d * mu_ref[n:n + 1, :]).astype(BF16)


def _c_prep(x, g, mu, shift0):
    m, d = x.shape
    nseq = shift0.shape[0]
    return pl.pallas_call(
        _c_prep_body,
        grid=(m // C_TM,),
        in_specs=[pl.BlockSpec((C_TM, d), lambda i: (i, 0)),
                  pl.BlockSpec((1, d), lambda i: (0, 0)),
                  pl.BlockSpec((6, d), lambda i: (0, 0)),
                  pl.BlockSpec((1, 1, d), lambda i: (_c_seq_id(i), 0, 0))],
        out_specs=[pl.BlockSpec((6, C_TM, d), lambda i: (0, i, 0)),
                   pl.BlockSpec((1, 1, d), lambda i: (_c_seq_id(i), 0, 0))],
        out_shape=[jax.ShapeDtypeStruct((6, m, d), BF16),
                   jax.ShapeDtypeStruct((nseq, 1, d), F32)],
        scratch_shapes=[pltpu.VMEM((1, d), F32)],
        compiler_params=_cparams(("arbitrary",)),
        name="c_prep",
    )(x, g.reshape(1, d), mu, shift0.reshape(nseq, 1, d))


def _bmm_body(l_ref, w_ref, o_ref):
    o_ref[0] = jnp.dot(l_ref[0], w_ref[0], preferred_element_type=F32)


def _c_bmm(lerp, w, tm=512, tn=1024):
    _, m, d = lerp.shape
    nb, _, n = w.shape
    return pl.pallas_call(
        _bmm_body,
        grid=(nb, m // tm, n // tn),
        in_specs=[pl.BlockSpec((1, tm, d), lambda b, i, j: (b, i, 0)),
                  pl.BlockSpec((1, d, tn), lambda b, i, j: (b, 0, j))],
        out_specs=pl.BlockSpec((1, tm, tn), lambda b, i, j: (b, i, j)),
        out_shape=jax.ShapeDtypeStruct((nb, m, n), F32),
        compiler_params=_cparams(("parallel", "parallel", "arbitrary")),
        name="c_bmm",
    )(lerp, w)


def _c_lora_body(l4_ref, l5_ref, wla_ref, wlb_ref, ala_ref, alb_ref, w0_ref, a0_ref,
                 wl_ref, al_ref):
    tw = jnp.tanh(jnp.dot(l4_ref[0], wla_ref[...], preferred_element_type=F32))
    wl_ref[...] = w0_ref[...] + jnp.dot(tw.astype(BF16), wlb_ref[...],
                                        preferred_element_type=F32)
    ta = jnp.dot(l5_ref[0], ala_ref[...], preferred_element_type=F32)
    al_ref[...] = a0_ref[...] + jnp.dot(ta.astype(BF16), alb_ref[...],
                                        preferred_element_type=F32)


def _c_lora(lerp, w_la, w_lb, a_la, a_lb, w0, a0, tm=512):
    _, m, d = lerp.shape
    pad_in = lambda w: jnp.pad(w, ((0, 0), (0, LANES - w.shape[1]))).astype(BF16)
    pad_out = lambda w: jnp.pad(w, ((0, LANES - w.shape[0]), (0, 0))).astype(BF16)
    lin = lambda n: pl.BlockSpec((1, tm, d), lambda i, n=n: (n, i, 0))
    win = pl.BlockSpec((d, LANES), lambda i: (0, 0))
    wout = pl.BlockSpec((LANES, d), lambda i: (0, 0))
    vec = pl.BlockSpec((1, d), lambda i: (0, 0))
    out = pl.BlockSpec((tm, d), lambda i: (i, 0))
    return pl.pallas_call(
        _c_lora_body,
        grid=(m // tm,),
        in_specs=[lin(4), lin(5), win, wout, win, wout, vec, vec],
        out_specs=[out, out],
        out_shape=[jax.ShapeDtypeStruct((m, d), F32)] * 2,
        compiler_params=_cparams(("parallel",)),
        name="c_lora",
    )(lerp, lerp, pad_in(w_la), pad_out(w_lb), pad_in(a_la), pad_out(a_lb),
      w0.reshape(1, d), a0.reshape(1, d))


C_HB = 8
DIMS_NN = (((1,), (0,)), ((), ()))
DIMS_NT = (((1,), (1,)), ((), ()))
DIMS_TN = (((0,), (0,)), ((), ()))


def _split(a):
    hi = a.astype(BF16)
    return hi, (a - hi.astype(F32)).astype(BF16)


def _dot3(a, b, dims):
    f = lambda x, y: lax.dot_general(x, y, dims, preferred_element_type=F32)
    return f(a[0], b[0]) + (f(a[0], b[1]) + f(a[1], b[0]))


def _cumsum_rows(x):
    row = lax.broadcasted_iota(I32, x.shape, 0)
    sh = 1
    while sh < x.shape[0]:
        x = x + jnp.where(row >= sh, pltpu.roll(x, sh, 0), 0.0)
        sh *= 2
    return x


def _rwkv_body(*refs, has_init):
    if has_init:
        (r_ref, k_ref, v_ref, g_ref, wl_ref, al_ref, kk_ref, ka_ref, rk_ref, lnw_ref, lnb_ref,
         s0_ref, _og_prompt, og_ref, so_ref, s_scr) = refs
    else:
        (r_ref, k_ref, v_ref, g_ref, wl_ref, al_ref, kk_ref, ka_ref, rk_ref, lnw_ref, lnb_ref,
         og_ref, so_ref, s_scr) = refs
    c = pl.program_id(2)
    n = CHUNK
    hs = HS_C

    @pl.when(c == 0)
    def _():
        if has_init:
            s_scr[...] = s0_ref[0]
        else:
            s_scr[...] = jnp.zeros_like(s_scr)

    r, k, v, g = r_ref[0], k_ref[0], v_ref[0], g_ref[0]
    z = -wl_ref[...]
    softplus = jnp.maximum(z, 0.0) + jnp.log(1.0 + jnp.exp(-jnp.abs(z)))
    logw = -jnp.exp(-softplus - 0.5)
    a = _sigmoid(al_ref[...])
    kkr = k * kk_ref[...]
    k2 = k * (1.0 + (a - 1.0) * ka_ref[...])
    bonus_in = r * k2 * rk_ref[...]

    cs_all = _cumsum_rows(logw)
    ec_all, eci_all, ecp_all = jnp.exp(cs_all), jnp.exp(-cs_all), jnp.exp(cs_all - logw)

    ti = lax.broadcasted_iota(I32, (n, n), 0)
    tj = lax.broadcasted_iota(I32, (n, n), 1)
    eye = jnp.where(tj == ti, 1.0, 0.0)
    gi = lax.broadcasted_iota(I32, (2 * n, 2 * n), 0)
    gj = lax.broadcasted_iota(I32, (2 * n, 2 * n), 1) & (n - 1)
    gmask = ((gi < n) & (gj < gi)) | ((gi >= n) & (gj <= gi - n))
    lvl_masks = []
    bs = 1
    while bs < n:
        sh = bs.bit_length()
        lvl_masks.append((jnp.right_shift(ti, sh) == jnp.right_shift(tj, sh))
                         & ((ti & bs) != 0) & ((tj & bs) == 0))
        bs *= 2

    heads = range(C_HB)
    sls = [slice(j * hs, (j + 1) * hs) for j in heads]
    vs = [v[:, sl] for sl in sls]
    zero = jnp.zeros_like(vs[0])
    p_, q_, kt, rt = [], [], [], []
    for sl in sls:
        kkj = kkr[:, sl]
        kkj = kkj / jnp.maximum(jnp.sqrt(jnp.sum(kkj * kkj, axis=-1, keepdims=True)), 1e-12)
        p_.append(kkj * ecp_all[:, sl])
        q_.append(kkj * a[:, sl] * eci_all[:, sl])
        kt.append(k2[:, sl] * eci_all[:, sl])
        rt.append(r[:, sl] * ec_all[:, sl])
    gm = [jnp.where(gmask, _dot3(_split(jnp.concatenate([p_[j], rt[j]], axis=0)),
                                 _split(jnp.concatenate([q_[j], kt[j]], axis=0)), DIMS_NT), 0.0)
          for j in heads]
    av = [_dot3(_split(gm[j][:n]), _split(jnp.concatenate([zero, vs[j]], axis=0)), DIMS_NN)
          for j in heads]
    a_qp = [gm[j][:n, :n] for j in heads]
    x = [eye - jnp.where(lvl_masks[0], a_qp[j], 0.0) for j in heads]
    for msk in lvl_masks[1:]:
        xs = [_split(x[j]) for j in heads]
        ax = [_dot3(_split(jnp.where(msk, a_qp[j], 0.0)), xs[j], DIMS_NN) for j in heads]
        x = [x[j] - _dot3(xs[j], _split(ax[j]), DIMS_NN) for j in heads]
    xpw = [_dot3(_split(x[j]), _split(jnp.concatenate([p_[j], av[j]], axis=1)), DIMS_NN)
           for j in heads]
    low = [jnp.concatenate([zero, vs[j]], axis=1) for j in heads]
    tb = [_dot3(_split(jnp.concatenate([xpw[j], low[j]], axis=0)),
                _split(jnp.concatenate([-q_[j], kt[j]], axis=0)), DIMS_TN) for j in heads]
    ro = [_dot3(_split(gm[j][n:]), _split(jnp.concatenate([-xpw[j], low[j]], axis=0)), DIMS_NN)
          for j in heads]
    s0 = [_split(s_scr[j]) for j in heads]
    o = [_dot3(_split(rt[j] + ro[j][:, :hs]), s0[j], DIMS_NT) + ro[j][:, hs:] for j in heads]
    for j in heads:
        ecl = ec_all[n - 1:n, sls[j]]
        s_scr[j] = _dot3(s0[j], _split((eye + tb[j][:hs]) * ecl), DIMS_NN) + tb[j][hs:] * ecl
    outs = []
    for j in heads:
        mean = jnp.mean(o[j], axis=-1, keepdims=True)
        var = jnp.mean(jnp.square(o[j] - mean), axis=-1, keepdims=True)
        on = (o[j] - mean) * lax.rsqrt(var + GN_EPS) * lnw_ref[:, sls[j]] + lnb_ref[:, sls[j]]
        bonus = jnp.sum(bonus_in[:, sls[j]], axis=-1, keepdims=True) * vs[j]
        outs.append(on + bonus)
    og_ref[...] = jnp.concatenate(outs, axis=1) * _silu(g)

    @pl.when(c == pl.num_programs(2) - 1)
    def _():
        so_ref[0] = s_scr[...]


def _rwkv(rkvg, wl, al, k_k, k_a, r_k, ln_w, ln_b, s0, og_prompt, *, nseq, nchunk, row0):
    w = C_HB * HS_C
    tok = lambda n: pl.BlockSpec((1, CHUNK, w), lambda s, h, c, n=n: (n, row0 + s * nchunk + c, h))
    tok2 = pl.BlockSpec((CHUNK, w), lambda s, h, c: (row0 + s * nchunk + c, h))
    vec = pl.BlockSpec((1, w), lambda s, h, c: (0, h))
    st = pl.BlockSpec((1, C_HB, HS_C, HS_C), lambda s, h, c: (s, h, 0, 0))
    has_init = s0 is not None
    ins = [rkvg, rkvg, rkvg, rkvg, wl, al] + [p.reshape(1, D_MODEL) for p in (k_k, k_a, r_k, ln_w, ln_b)]
    specs = [tok(0), tok(1), tok(2), tok(3), tok2, tok2] + [vec] * 5
    aliases = {}
    if has_init:
        ins += [s0, og_prompt]
        specs += [st, pl.BlockSpec(memory_space=pl.ANY)]
        aliases = {len(ins) - 1: 0}
    return pl.pallas_call(
        functools.partial(_rwkv_body, has_init=has_init),
        grid=(nseq, H_C // C_HB, nchunk),
        in_specs=specs,
        out_specs=[tok2, st],
        out_shape=[jax.ShapeDtypeStruct((M_ROWS, D_MODEL), F32),
                   jax.ShapeDtypeStruct((nseq, H_C, HS_C, HS_C), F32)],
        input_output_aliases=aliases,
        scratch_shapes=[pltpu.VMEM((C_HB, HS_C, HS_C), F32)],
        compiler_params=_cparams(("parallel", "parallel", "arbitrary")),
        name="rwkv_scan",
    )(*ins)


def kernel(x_prompt, x_sample, cache_a_k, cache_a_v, cache_b_k, cache_b_v, cache_b_kidx, state_c_wkv, state_c_shift, norm_g, final_g, w_out, a_w_in, a_lam, a_subln_g, b_w_in, c_mu, c_w_rkvg, c_w0, c_w_la, c_w_lb, c_a0, c_a_la, c_a_lb, c_k_k, c_k_a, c_r_k, c_ln_w, c_ln_b):
    x = jnp.concatenate([x_prompt.reshape(N_PROMPT, D_MODEL),
                         x_sample.reshape(N_SAMPLE, D_MODEL)], axis=0)
    cos1, sn1, cos2, sn2 = _rope_tables()
    split = lambda arr: (arr[:N_PROMPT], arr[N_PROMPT:])
    outs = {n: [] for n in ("akp", "avp", "aks", "avs", "bkp", "bvp", "bip", "bks", "bvs", "bis",
                            "cwp", "chp", "cws", "chs")}
    y = None
    for i in range(DEPTH):
        kind, j = i % N_MIXERS, i // N_MIXERS
        if kind == 0:
            lam_init = 0.8 - 0.6 * math.exp(-0.3 * i)
            qkvg = _norm_proj(x, norm_g[i], a_w_in[j].astype(BF16), 512, 1024)
            q_rot, k_rot = _rope_a(qkvg, cos1, sn1)
            og_p = _attn_a_prompt(q_rot, k_rot, qkvg, a_lam[j], a_subln_g[j], lam_init)
            og = _attn_a_sample(q_rot, k_rot, qkvg,
                                cache_a_k.reshape(-1, D_MODEL), cache_a_v.reshape(-1, D_MODEL), j,
                                a_lam[j], a_subln_g[j], lam_init, og_p)
            kp, ks = split(k_rot)
            vp, vs = split(qkvg[:, 2 * D_MODEL:3 * D_MODEL])
            outs["akp"].append(kp.reshape(BATCH, SEQ, H_A, 2 * DH_A))
            outs["avp"].append(vp.reshape(BATCH, SEQ, H_A, 2 * DH_A))
            outs["aks"].append(ks.reshape(DEC_BATCH, DEC_SEQ, H_A, 2 * DH_A))
            outs["avs"].append(vs.reshape(DEC_BATCH, DEC_SEQ, H_A, 2 * DH_A))
        elif kind == 1:
            w = b_w_in[j]
            w = jnp.concatenate([w[:, :4096], w[:, 4176:], w[:, 4096:4176],
                                 jnp.zeros((D_MODEL, B_COLS - w.shape[1]), w.dtype)], axis=1)
            proj = _norm_proj(x, norm_g[i], w.astype(BF16), 512, 896)
            q_rot, k_rot, qi_rot, kid, wi = _rope_b(proj, cos1, sn1, cos2, sn2)
            og_p = _dsa_prompt(q_rot, k_rot, qi_rot, kid, wi, proj)
            kidx_past = cache_b_kidx.reshape(-1, D_IDX)
            og = _dsa_sample(q_rot, k_rot, qi_rot, kid, wi, proj,
                             cache_b_k.reshape(-1, KV_B * HD_B), cache_b_v.reshape(-1, KV_B * HD_B),
                             jnp.concatenate([kidx_past, kidx_past], axis=1), j, og_p)
            kp, ks = split(k_rot)
            vp, vs = split(proj[:, B_V0:B_V0 + KV_B * HD_B])
            ip, is_ = split(kid[:, :D_IDX])
            outs["bkp"].append(kp.reshape(BATCH, SEQ, KV_B, HD_B))
            outs["bvp"].append(vp.reshape(BATCH, SEQ, KV_B, HD_B))
            outs["bip"].append(ip.reshape(BATCH, SEQ, D_IDX))
            outs["bks"].append(ks.reshape(DEC_BATCH, DEC_SEQ, KV_B, HD_B))
            outs["bvs"].append(vs.reshape(DEC_BATCH, DEC_SEQ, KV_B, HD_B))
            outs["bis"].append(is_.reshape(DEC_BATCH, DEC_SEQ, D_IDX))
        else:
            shift0 = jnp.concatenate([jnp.zeros((BATCH, D_MODEL), F32), state_c_shift[j]], axis=0)
            lerp, hlast = _c_prep(x, norm_g[i], c_mu[j], shift0)
            rkvg = _c_bmm(lerp, c_w_rkvg[j].astype(BF16))
            wl, al = _c_lora(lerp, c_w_la[j], c_w_lb[j], c_a_la[j], c_a_lb[j], c_w0[j], c_a0[j])
            par = (c_k_k[j], c_k_a[j], c_r_k[j], c_ln_w[j], c_ln_b[j])
            og_p, st_p = _rwkv(rkvg, wl, al, *par, None, None, nseq=BATCH, nchunk=SEQ // CHUNK,
                               row0=0)
            og, st_s = _rwkv(rkvg, wl, al, *par, state_c_wkv[j], og_p, nseq=DEC_BATCH, nchunk=1,
                             row0=N_PROMPT // CHUNK)
            outs["cwp"].append(st_p)
            outs["chp"].append(hlast[:BATCH, 0])
            outs["cws"].append(st_s)
            outs["chs"].append(hlast[BATCH:, 0])
        if i == DEPTH - 1:
            y = _proj_out(og, x, w_out[i].astype(BF16), final_g)
        else:
            x = _proj_out(og, x, w_out[i].astype(BF16))
    yp, ys = split(y)
    st = lambda n: jnp.stack(outs[n])
    return (yp.reshape(BATCH, SEQ, D_MODEL), ys.reshape(DEC_BATCH, DEC_SEQ, D_MODEL),
            st("akp"), st("avp"), st("aks"), st("avs"),
            st("bkp"), st("bvp"), st("bip"), st("bks"), st("bvs"), st("bis"),
            st("cwp"), st("chp"), st("cws"), st("chs"))
```

```python
import functools
import math

import jax
import jax.numpy as jnp
from jax import lax
from jax.experimental import pallas as pl
from jax.experimental.pallas import tpu as pltpu

F32 = jnp.float32
BF16 = jnp.bfloat16
I32 = jnp.int32

D_MODEL = 2048
BATCH = 4
SEQ = 2048
DEPTH = 4
DEC_BATCH = 8
DEC_SEQ = 64
PAST_LEN = 4096
CHUNK = 64
N_MIXERS = 3
ROPE_THETA = 10000.0
RMS_EPS = 1e-6
DH_A = 64
H_A = 16
SUBLN_EPS = 1e-5
HD_B = 128
H_B = 16
KV_B = 4
H_IDX = 16
D_IDX = 64
TOPK_MAX = 256
HS_C = 64
H_C = 32
R_DECAY = 96
R_ICLR = 96
GN_EPS = 64e-5

N_PROMPT = BATCH * SEQ
N_SAMPLE = DEC_BATCH * DEC_SEQ
M_ROWS = N_PROMPT + N_SAMPLE
LANES = 128
VMEM_LIMIT = 56 * 1024 * 1024
NEG_INF = float("-inf")
INT_MIN = -2 ** 31

B_Q0, B_K0, B_V0, B_QI0, B_G0, B_KI0 = 0, 2048, 2560, 3072, 4096, 6144
B_COLS = 6272

assert PAST_LEN % CHUNK == 0 and DEC_SEQ == CHUNK
LOG2_CHUNK = CHUNK.bit_length() - 1
assert 1 << LOG2_CHUNK == CHUNK


def _chunk_of(pos):
    return jnp.right_shift(pos, LOG2_CHUNK)


def _cparams(sem):
    return pltpu.CompilerParams(dimension_semantics=sem, vmem_limit_bytes=VMEM_LIMIT)


def _sigmoid(x):
    return 1.0 / (1.0 + jnp.exp(-x))


def _silu(x):
    return x * _sigmoid(x)


def _rms(x, g):
    return x * lax.rsqrt(jnp.mean(x * x, axis=-1, keepdims=True) + RMS_EPS) * g


def _dot_nt(a, b, precision=None):
    return lax.dot_general(a, b, (((1,), (1,)), ((), ())), precision=precision,
                           preferred_element_type=F32)


def _norm_proj_body(x_ref, g_ref, w_ref, o_ref, h_scr):
    @pl.when(pl.program_id(1) == 0)
    def _():
        h_scr[...] = _rms(x_ref[...], g_ref[...]).astype(BF16)
    o_ref[...] = jnp.dot(h_scr[...], w_ref[...], preferred_element_type=F32)


def _norm_proj(x, g, w, tm, tn):
    m, d = x.shape
    n = w.shape[1]
    return pl.pallas_call(
        _norm_proj_body,
        grid=(m // tm, n // tn),
        in_specs=[pl.BlockSpec((tm, d), lambda i, j: (i, 0)),
                  pl.BlockSpec((1, d), lambda i, j: (0, 0)),
                  pl.BlockSpec((d, tn), lambda i, j: (0, j))],
        out_specs=pl.BlockSpec((tm, tn), lambda i, j: (i, j)),
        out_shape=jax.ShapeDtypeStruct((m, n), F32),
        scratch_shapes=[pltpu.VMEM((tm, d), BF16)],
        compiler_params=_cparams(("parallel", "arbitrary")),
        name="norm_proj",
    )(x, g.reshape(1, d), w)


def _proj_out_body(og_ref, x_ref, w_ref, o_ref):
    o_ref[...] = x_ref[...] + jnp.dot(og_ref[...].astype(BF16), w_ref[...],
                                      preferred_element_type=F32)


def _proj_out_final_body(og_ref, x_ref, w_ref, g_ref, o_ref):
    xn = x_ref[...] + jnp.dot(og_ref[...].astype(BF16), w_ref[...],
                              preferred_element_type=F32)
    o_ref[...] = _rms(xn, g_ref[...])


def _proj_out(og, x, w, final_g=None, tm=256):
    m, d = x.shape
    row = pl.BlockSpec((tm, d), lambda i: (i, 0))
    full = pl.BlockSpec((d, d), lambda i: (0, 0))
    if final_g is None:
        body, extra, extra_specs = _proj_out_body, (), []
    else:
        body, extra = _proj_out_final_body, (final_g.reshape(1, d),)
        extra_specs = [pl.BlockSpec((1, d), lambda i: (0, 0))]
    return pl.pallas_call(
        body,
        grid=(m // tm,),
        in_specs=[row, row, full] + extra_specs,
        out_specs=row,
        out_shape=jax.ShapeDtypeStruct((m, d), F32),
        compiler_params=_cparams(("parallel",)),
        name="proj_out",
    )(og, x, w, *extra)


def _rope_tables():
    pos = jnp.concatenate([jnp.tile(jnp.arange(SEQ), BATCH),
                           jnp.tile(PAST_LEN + jnp.arange(DEC_SEQ), DEC_BATCH)]).astype(F32)

    def table(dh, reps):
        half = dh // 2
        inv = jnp.power(ROPE_THETA, -jnp.arange(half, dtype=F32) * (2.0 / dh))
        ang = pos[:, None] * inv[None, :]
        cos, sin = jnp.cos(ang), jnp.sin(ang)
        return (jnp.tile(jnp.concatenate([cos, cos], axis=1), (1, reps)),
                jnp.tile(jnp.concatenate([-sin, sin], axis=1), (1, reps)))

    cos1, sn1 = table(64, 2)
    cos2, sn2 = table(128, 1)
    return cos1, sn1, cos2, sn2


def _rope64(x, cos, sn):
    lane = lax.broadcasted_iota(I32, x.shape, 1)
    partner = jnp.where((lane & 63) < 32, pltpu.roll(x, 96, 1), pltpu.roll(x, 32, 1))
    return x * cos + partner * sn


def _rope128(x, cos, sn):
    return x * cos + pltpu.roll(x, 64, 1) * sn


def _rope_a_body(q_ref, k_ref, cos_ref, sn_ref, qo_ref, ko_ref):
    cos, sn = cos_ref[...], sn_ref[...]
    for h in range(H_A):
        sl = slice(h * LANES, (h + 1) * LANES)
        qo_ref[:, sl] = _rope64(q_ref[:, sl], cos, sn) * (DH_A ** -0.5)
        ko_ref[:, sl] = _rope64(k_ref[:, sl], cos, sn)


def _rope_a(qkvg, cos1, sn1, tm=256):
    m = qkvg.shape[0]
    blk = lambda c: pl.BlockSpec((tm, D_MODEL), lambda i, c=c: (i, c))
    tab = pl.BlockSpec((tm, LANES), lambda i: (i, 0))
    return pl.pallas_call(
        _rope_a_body,
        grid=(m // tm,),
        in_specs=[blk(0), blk(1), tab, tab],
        out_specs=[blk(0), blk(0)],
        out_shape=[jax.ShapeDtypeStruct((m, D_MODEL), F32)] * 2,
        compiler_params=_cparams(("parallel",)),
        name="rope_a",
    )(qkvg, qkvg, cos1, sn1)


def _stack_maps_t(q):
    lane = lax.broadcasted_iota(I32, q.shape, 1)
    qs = jnp.concatenate([jnp.where(lane < DH_A, q, 0.0), jnp.where(lane >= DH_A, q, 0.0)], axis=0)
    return qs.T.astype(BF16)


def _flash_t_step(s, vt, carry):
    m, l, acc = carry
    m_new = jnp.maximum(m, jnp.max(s, axis=0, keepdims=True))
    alpha = jnp.exp(m - m_new)
    p = jnp.exp(s - m_new)
    l = alpha * l + jnp.sum(p, axis=0, keepdims=True)
    pv = lax.dot_general(vt, p.astype(BF16), DIMS_TN, preferred_element_type=F32)
    return m_new, l, alpha * acc + pv


def _flash_t_init(n):
    return (jnp.full((1, n), -1e30, F32), jnp.zeros((1, n), F32), jnp.zeros((LANES, n), F32))


def _diff_epilogue(l, acc, tq, lam_ref, sg, g, lam_init):
    o = (acc / l).T
    lp = lam_ref[...]
    lam = (jnp.exp(jnp.sum(lp[0:1] * lp[1:2], axis=-1, keepdims=True))
           - jnp.exp(jnp.sum(lp[2:3] * lp[3:4], axis=-1, keepdims=True)) + lam_init)
    o = o[:tq] - lam * o[tq:]
    o = o * lax.rsqrt(jnp.mean(o * o, axis=-1, keepdims=True) + SUBLN_EPS) * sg
    o = o * (1.0 - lam_init)
    return o * _silu(g)


A_HP = 2


def _attn_a_prompt_body(q_ref, k_ref, v_ref, g_ref, lam_ref, sg_ref, o_ref, *, tq, lam_init):
    qi = pl.program_id(2)
    hsl = [slice(h * LANES, (h + 1) * LANES) for h in range(A_HP)]
    qst = [_stack_maps_t(q_ref[:, sl]) for sl in hsl]
    n = 2 * tq
    krow = lax.broadcasted_iota(I32, (tq, n), 0)
    qcol = lax.broadcasted_iota(I32, (tq, n), 1) & (tq - 1)
    diag_mask = _chunk_of(krow) <= _chunk_of(qcol)

    def tile(kt, carries, masked):
        off = pl.multiple_of(kt * tq, tq)
        ss = [jnp.dot(k_ref[pl.ds(off, tq), sl].astype(BF16), qst[h], preferred_element_type=F32)
              for h, sl in enumerate(hsl)]
        out = []
        for h, sl in enumerate(hsl):
            s = jnp.where(diag_mask, ss[h], NEG_INF) if masked else ss[h]
            out.append(_flash_t_step(s, v_ref[pl.ds(off, tq), sl].astype(BF16), carries[h]))
        return tuple(out)

    carries = lax.fori_loop(0, qi, lambda kt, c: tile(kt, c, False),
                            tuple(_flash_t_init(n) for _ in hsl))
    carries = tile(qi, carries, True)
    for h, sl in enumerate(hsl):
        _, l, acc = carries[h]
        o_ref[:, sl] = _diff_epilogue(l, acc, tq, lam_ref, sg_ref[...], g_ref[:, sl], lam_init)


def _attn_a_prompt(q_rot, k_rot, qkvg, lam_p, subln_g, lam_init, tq=256):
    nq = SEQ // tq
    w = A_HP * LANES
    nh = H_A // A_HP
    return pl.pallas_call(
        functools.partial(_attn_a_prompt_body, tq=tq, lam_init=lam_init),
        grid=(BATCH, nh, nq),
        in_specs=[pl.BlockSpec((tq, w), lambda b, h, i: (b * nq + i, h)),
                  pl.BlockSpec((SEQ, w), lambda b, h, i: (b, h)),
                  pl.BlockSpec((SEQ, w), lambda b, h, i: (b, 2 * nh + h)),
                  pl.BlockSpec((tq, w), lambda b, h, i: (b * nq + i, 3 * nh + h)),
                  pl.BlockSpec((4, DH_A), lambda b, h, i: (0, 0)),
                  pl.BlockSpec((1, LANES), lambda b, h, i: (0, 0))],
        out_specs=pl.BlockSpec((tq, w), lambda b, h, i: (b * nq + i, h)),
        out_shape=jax.ShapeDtypeStruct((M_ROWS, D_MODEL), F32),
        compiler_params=_cparams(("parallel", "parallel", "arbitrary")),
        name="attn_a_prompt",
    )(q_rot, k_rot, qkvg, qkvg, lam_p, subln_g.reshape(1, LANES))


def _attn_a_sample_body(q_ref, kp_ref, vp_ref, k_ref, v_ref, g_ref, lam_ref, sg_ref, _og_prompt,
                        o_ref, qst_scr, ml_scr, acc_scr, *, tk, lam_init):
    kt = pl.program_id(1)
    tq = DEC_SEQ
    hsl = [slice(h * LANES, (h + 1) * LANES) for h in range(H_A)]

    @pl.when(kt == 0)
    def _():
        for h, sl in enumerate(hsl):
            qst_scr[h] = _stack_maps_t(q_ref[:, sl])
        m0, l0, acc0 = _flash_t_init(2 * tq)
        for h in range(H_A):
            ml_scr[h, 0:1, :] = m0
            ml_scr[h, 1:2, :] = l0
            acc_scr[h] = acc0

    def load(h):
        return ml_scr[h, 0:1, :], ml_scr[h, 1:2, :], acc_scr[h]

    def store(h, carry):
        ml_scr[h, 0:1, :], ml_scr[h, 1:2, :], acc_scr[h] = carry

    def head_pair(i, _):
        hs = (2 * i, 2 * i + 1)
        ss = [jnp.dot(kp_ref[pl.ds(h, tk, stride=H_A), :].astype(BF16), qst_scr[h],
                      preferred_element_type=F32) for h in hs]
        for h, s in zip(hs, ss):
            vb = vp_ref[pl.ds(h, tk, stride=H_A), :].astype(BF16)
            store(h, _flash_t_step(s, vb, load(h)))
        return 0

    lax.fori_loop(0, H_A // 2, head_pair, 0)

    @pl.when(kt == pl.num_programs(1) - 1)
    def _():
        for h, sl in enumerate(hsl):
            s = jnp.dot(k_ref[:, sl].astype(BF16), qst_scr[h], preferred_element_type=F32)
            _, l, acc = _flash_t_step(s, v_ref[:, sl].astype(BF16), load(h))
            o_ref[:, sl] = _diff_epilogue(l, acc, tq, lam_ref, sg_ref[...], g_ref[:, sl], lam_init)


def _attn_a_sample(q_rot, k_rot, qkvg, k_past, v_past, layer, lam_p, subln_g, lam_init, og_prompt,
                   tk=512):
    r0 = N_PROMPT // DEC_SEQ
    nkt = PAST_LEN // tk
    new = lambda c: pl.BlockSpec((DEC_SEQ, D_MODEL), lambda b, t, c=c: (r0 + b, c))
    past = pl.BlockSpec((tk * H_A, LANES), lambda b, t: ((layer * DEC_BATCH + b) * nkt + t, 0))
    return pl.pallas_call(
        functools.partial(_attn_a_sample_body, tk=tk, lam_init=lam_init),
        grid=(DEC_BATCH, nkt),
        in_specs=[new(0), past, past, new(0), new(2), new(3),
                  pl.BlockSpec((4, DH_A), lambda b, t: (0, 0)),
                  pl.BlockSpec((1, LANES), lambda b, t: (0, 0)),
                  pl.BlockSpec(memory_space=pl.ANY)],
        out_specs=new(0),
        out_shape=jax.ShapeDtypeStruct((M_ROWS, D_MODEL), F32),
        input_output_aliases={8: 0},
        scratch_shapes=[pltpu.VMEM((H_A, LANES, 2 * DEC_SEQ), BF16),
                        pltpu.VMEM((H_A, 8, 2 * DEC_SEQ), F32),
                        pltpu.VMEM((H_A, LANES, 2 * DEC_SEQ), F32)],
        compiler_params=_cparams(("parallel", "arbitrary")),
        name="attn_a_sample",
    )(q_rot, k_past, v_past, k_rot, qkvg, qkvg, lam_p, subln_g.reshape(1, LANES), og_prompt)


def _rope_b_body(q_ref, k_ref, qi_ref, sm_ref, c1_ref, s1_ref, c2_ref, s2_ref,
                 qo_ref, ko_ref, qio_ref, kid_ref, wi_ref):
    c1, s1, c2, s2 = c1_ref[...], s1_ref[...], c2_ref[...], s2_ref[...]
    for h in range(H_B):
        sl = slice(h * LANES, (h + 1) * LANES)
        qo_ref[:, sl] = _rope128(q_ref[:, sl], c2, s2) * (HD_B ** -0.5)
    for h in range(KV_B):
        sl = slice(h * LANES, (h + 1) * LANES)
        ko_ref[:, sl] = _rope128(k_ref[:, sl], c2, s2)
    for h in range(H_IDX * D_IDX // LANES):
        sl = slice(h * LANES, (h + 1) * LANES)
        qio_ref[:, sl] = _rope64(qi_ref[:, sl], c1, s1) * (D_IDX ** -0.5)
    sm = sm_ref[...]
    lane = lax.broadcasted_iota(I32, sm.shape, 1)
    kr = _rope64(sm, c1, s1)
    kid_ref[...] = jnp.where(lane < D_IDX, kr, pltpu.roll(kr, D_IDX, 1))
    wi_ref[...] = pltpu.roll(sm, D_IDX, 1) * (H_IDX ** -0.5)


def _rope_b(proj, cos1, sn1, cos2, sn2, tm=256):
    m = proj.shape[0]
    tab = pl.BlockSpec((tm, LANES), lambda i: (i, 0))
    blk = lambda w, c: pl.BlockSpec((tm, w), lambda i, c=c: (i, c))
    widths = (H_B * HD_B, KV_B * HD_B, H_IDX * D_IDX, LANES, LANES)
    return pl.pallas_call(
        _rope_b_body,
        grid=(m // tm,),
        in_specs=[blk(2048, 0), blk(512, B_K0 // 512), blk(1024, B_QI0 // 1024),
                  blk(LANES, B_KI0 // LANES), tab, tab, tab, tab],
        out_specs=[blk(w, 0) for w in widths],
        out_shape=[jax.ShapeDtypeStruct((m, w), F32) for w in widths],
        compiler_params=_cparams(("parallel",)),
        name="rope_b",
    )(proj, proj, proj, proj, cos1, sn1, cos2, sn2)


POS_BITS = 13
DSA_BUCKET = 512


def _count(pred):
    return jnp.sum(jnp.where(pred, 1.0, 0.0), axis=-1, keepdims=True)


def _select_topk(scores, poss, n_sel):
    keys = []
    for s in scores:
        bits = lax.bitcast_convert_type(s + 0.0, I32)
        keys.append(jnp.where(bits < 0, bits ^ 0x7FFFFFFF, bits))
    tq = scores[0].shape[0]
    n_sel = float(n_sel)

    def thr_step(i, t):
        cand_bits = t | lax.shift_left(jnp.int32(1), 31 - i)
        cand = cand_bits ^ INT_MIN
        cnt = sum(_count(k >= cand) for k in keys)
        return jnp.where(cnt >= n_sel, cand_bits, t)

    t = lax.fori_loop(0, 32, thr_step, jnp.zeros((tq, 1), I32))
    thr = t ^ INT_MIN
    need = n_sel - sum(_count(k > thr) for k in keys)
    n_tied = sum(_count(k == thr) for k in keys)

    def tie_step(i, j):
        cand = j | lax.shift_left(jnp.int32(1), POS_BITS - 1 - i)
        cnt = sum(_count((k == thr) & (p < cand)) for k, p in zip(keys, poss))
        return jnp.where(cnt < need, cand, j)

    j = lax.cond(jnp.max(n_tied - need) > 0.0,
                 lambda: lax.fori_loop(0, POS_BITS, tie_step, jnp.zeros((tq, 1), I32)),
                 lambda: jnp.full((tq, 1), (1 << POS_BITS) - 1, I32))
    return [(k > thr) | ((k == thr) & (p <= j)) for k, p in zip(keys, poss)]


def _dsa_body(*refs, tq, seg_lens, seg_pos0, q_pos0, causal, n_sel, n_aliased, bucket):
    nseg = len(seg_lens)
    n_in = 4 + 3 * nseg + n_aliased
    assert max(seg_pos0[i] + seg_lens[i] for i in range(nseg)) <= 1 << POS_BITS
    if bucket is None:
        _dsa_run(refs, tq, seg_lens, seg_pos0, q_pos0, causal, n_sel, n_in)
        return
    assert causal and nseg == 1 and seg_pos0[0] == 0 and q_pos0 == 0 and bucket % tq == 0
    last = (pl.program_id(1) * tq) // bucket
    for v in range(seg_lens[0] // bucket):
        pl.when(last == v)(functools.partial(_dsa_run, refs, tq, ((v + 1) * bucket,), seg_pos0,
                                             q_pos0, causal, n_sel, n_in))


def _dsa_run(refs, tq, seg_lens, seg_pos0, q_pos0, causal, n_sel, n_in):
    nseg = len(seg_lens)
    qi_ref, wi_ref, q_ref, g_ref = refs[:4]
    kid_refs = refs[4:4 + nseg]
    k_refs = refs[4 + nseg:4 + 2 * nseg]
    v_refs = refs[4 + 2 * nseg:4 + 3 * nseg]
    o_ref = refs[n_in]
    bias_refs = refs[n_in + 1:]
    n_rep = H_B // KV_B

    @pl.when(pl.program_id(2) == 0)
    def _():
        wi = wi_ref[...]
        qpos = q_pos0 + pl.program_id(1) * tq
        scores, poss, adms = [], [], []
        for si in range(nseg):
            sl = seg_lens[si]
            kid = kid_refs[si][0:sl, :].astype(BF16)
            sc = jnp.zeros((tq, sl), F32)
            for p in range(H_IDX // 2):
                qp = qi_ref[:, p * LANES:(p + 1) * LANES]
                lane = lax.broadcasted_iota(I32, qp.shape, 1)
                qs = jnp.concatenate([jnp.where(lane < D_IDX, qp, 0.0),
                                      jnp.where(lane >= D_IDX, qp, 0.0)], axis=0).astype(BF16)
                lg = jnp.maximum(_dot_nt(qs, kid), 0.0)
                sc = sc + wi[:, 2 * p:2 * p + 1] * lg[:tq] + wi[:, 2 * p + 1:2 * p + 2] * lg[tq:]
            kpos = seg_pos0[si] + lax.broadcasted_iota(I32, (tq, sl), 1)
            if causal:
                qrow = qpos + lax.broadcasted_iota(I32, (tq, sl), 0)
                adm = _chunk_of(kpos) <= _chunk_of(qrow)
                sc = jnp.where(adm, sc, NEG_INF)
            else:
                adm = None
            scores.append(sc)
            poss.append(kpos)
            adms.append(adm)
        sels = _select_topk(scores, poss, n_sel)
        for si in range(nseg):
            bias = jnp.where(sels[si], 0.0, NEG_INF)
            if adms[si] is not None:
                bias = jnp.where(adms[si], bias, NEG_INF)
            bias_refs[si][:, 0:seg_lens[si]] = bias

    q = q_ref[...]
    qg = jnp.concatenate([q[:, r * LANES:(r + 1) * LANES] for r in range(n_rep)],
                         axis=0).astype(BF16)
    ss = []
    for si in range(nseg):
        b = bias_refs[si][:, 0:seg_lens[si]]
        s = _dot_nt(qg, k_refs[si][0:seg_lens[si], :].astype(BF16))
        ss.append(s + jnp.concatenate([b] * n_rep, axis=0))
    m = functools.reduce(jnp.maximum, [jnp.max(s, axis=-1, keepdims=True) for s in ss])
    l = 0.0
    acc = 0.0
    for si in range(nseg):
        p = jnp.exp(ss[si] - m)
        l = l + jnp.sum(p, axis=-1, keepdims=True)
        acc = acc + jnp.dot(p.astype(BF16), v_refs[si][0:seg_lens[si], :].astype(BF16),
                            preferred_element_type=F32)
    o = acc / l
    o = jnp.concatenate([o[r * tq:(r + 1) * tq] for r in range(n_rep)], axis=1)
    o_ref[...] = o * _silu(g_ref[...])


def _dsa_prompt(q_rot, k_rot, qi_rot, kid, wi, proj, tq=128):
    nq = SEQ // tq
    gw = KV_B * HD_B
    row = lambda w, c0: pl.BlockSpec((tq, w), lambda b, i, g, c0=c0: (b * nq + i, c0 + g))
    row0 = lambda w: pl.BlockSpec((tq, w), lambda b, i, g: (b * nq + i, 0))
    seq = lambda c0: pl.BlockSpec((SEQ, LANES), lambda b, i, g, c0=c0: (b, c0 + g))
    n_sel = min(TOPK_MAX, SEQ // 4)
    return pl.pallas_call(
        functools.partial(_dsa_body, tq=tq, seg_lens=(SEQ,), seg_pos0=(0,), q_pos0=0,
                          causal=True, n_sel=n_sel, n_aliased=0, bucket=DSA_BUCKET),
        grid=(BATCH, nq, KV_B),
        in_specs=[row0(H_IDX * D_IDX), row0(LANES), row(gw, 0), row(gw, B_G0 // gw),
                  pl.BlockSpec((SEQ, LANES), lambda b, i, g: (b, 0)),
                  seq(0), seq(B_V0 // LANES)],
        out_specs=row(gw, 0),
        out_shape=jax.ShapeDtypeStruct((M_ROWS, D_MODEL), F32),
        scratch_shapes=[pltpu.VMEM((tq, SEQ), F32)],
        compiler_params=_cparams(("parallel", "parallel", "arbitrary")),
        name="dsa_prompt",
    )(qi_rot, wi, q_rot, proj, kid, k_rot, proj)


def _dsa_sample(q_rot, k_rot, qi_rot, kid, wi, proj, k_past, v_past, kid_past, layer, og_prompt):
    tq = DEC_SEQ
    r0 = N_PROMPT // tq
    gw = KV_B * HD_B
    row = lambda w, c0: pl.BlockSpec((tq, w), lambda b, i, g, c0=c0: (r0 + b, c0 + g))
    row0 = lambda w: pl.BlockSpec((tq, w), lambda b, i, g: (r0 + b, 0))
    past = pl.BlockSpec((PAST_LEN, LANES), lambda b, i, g: (layer * DEC_BATCH + b, g))
    s_all = PAST_LEN + DEC_SEQ
    n_sel = min(TOPK_MAX, s_all // 4)
    return pl.pallas_call(
        functools.partial(_dsa_body, tq=tq, seg_lens=(PAST_LEN, DEC_SEQ),
                          seg_pos0=(0, PAST_LEN), q_pos0=PAST_LEN, causal=False, n_sel=n_sel,
                          n_aliased=1, bucket=None),
        grid=(DEC_BATCH, 1, KV_B),
        in_specs=[row0(H_IDX * D_IDX), row0(LANES), row(gw, 0), row(gw, B_G0 // gw),
                  pl.BlockSpec((PAST_LEN, LANES), lambda b, i, g: (layer * DEC_BATCH + b, 0)),
                  row0(LANES),
                  past, row(LANES, 0),
                  past, row(LANES, B_V0 // LANES),
                  pl.BlockSpec(memory_space=pl.ANY)],
        out_specs=row(gw, 0),
        out_shape=jax.ShapeDtypeStruct((M_ROWS, D_MODEL), F32),
        input_output_aliases={10: 0},
        scratch_shapes=[pltpu.VMEM((tq, PAST_LEN), F32), pltpu.VMEM((tq, DEC_SEQ), F32)],
        compiler_params=_cparams(("parallel", "arbitrary", "arbitrary")),
        name="dsa_sample",
    )(qi_rot, wi, q_rot, proj, kid_past, kid, k_past, k_rot, v_past, proj, og_prompt)


C_TM = 64
C_TILES_PER_SEQ = SEQ // C_TM
C_PROMPT_TILES = N_PROMPT // C_TM


def _c_seq_id(i):
    return jnp.where(i < C_PROMPT_TILES, i // C_TILES_PER_SEQ, i - C_PROMPT_TILES + BATCH)


def _c_prep_body(x_ref, g_ref, mu_ref, sh_ref, l_ref, hl_ref, carry):
    i = pl.program_id(0)

    @pl.when(i == 0)
    def _():
        carry[...] = jnp.zeros_like(carry)

    h = _rms(x_ref[...], g_ref[...])
    start = jnp.logical_or(i >= C_PROMPT_TILES, i % C_TILES_PER_SEQ == 0)
    first = jnp.where(start, sh_ref[0], carry[...])
    row = lax.broadcasted_iota(I32, h.shape, 0)
    prev = jnp.where(row == 0, first, pltpu.roll(h, 1, 0))
    last = h[C_TM - 1:C_TM, :]
    carry[...] = last
    hl_ref[0] = last
    d = prev - h
    for n in range(6):
        l_ref[n] = (h + d * mu_ref[n:n + 1, :]).astype(BF16)


def _c_prep(x, g, mu, shift0):
    m, d = x.shape
    nseq = shift0.shape[0]
    return pl.pallas_call(
        _c_prep_body,
        grid=(m // C_TM,),
        in_specs=[pl.BlockSpec((C_TM, d), lambda i: (i, 0)),
                  pl.BlockSpec((1, d), lambda i: (0, 0)),
                  pl.BlockSpec((6, d), lambda i: (0, 0)),
                  pl.BlockSpec((1, 1, d), lambda i: (_c_seq_id(i), 0, 0))],
        out_specs=[pl.BlockSpec((6, C_TM, d), lambda i: (0, i, 0)),
                   pl.BlockSpec((1, 1, d), lambda i: (_c_seq_id(i), 0, 0))],
        out_shape=[jax.ShapeDtypeStruct((6, m, d), BF16),
                   jax.ShapeDtypeStruct((nseq, 1, d), F32)],
        scratch_shapes=[pltpu.VMEM((1, d), F32)],
        compiler_params=_cparams(("arbitrary",)),
        name="c_prep",
    )(x, g.reshape(1, d), mu, shift0.reshape(nseq, 1, d))


def _bmm_body(l_ref, w_ref, o_ref):
    o_ref[0] = jnp.dot(l_ref[0], w_ref[0], preferred_element_type=F32)


def _c_bmm(lerp, w, tm=512, tn=1024):
    _, m, d = lerp.shape
    nb, _, n = w.shape
    return pl.pallas_call(
        _bmm_body,
        grid=(nb, m // tm, n // tn),
        in_specs=[pl.BlockSpec((1, tm, d), lambda b, i, j: (b, i, 0)),
                  pl.BlockSpec((1, d, tn), lambda b, i, j: (b, 0, j))],
        out_specs=pl.BlockSpec((1, tm, tn), lambda b, i, j: (b, i, j)),
        out_shape=jax.ShapeDtypeStruct((nb, m, n), F32),
        compiler_params=_cparams(("parallel", "parallel", "arbitrary")),
        name="c_bmm",
    )(lerp, w)


def _c_lora_body(l4_ref, l5_ref, wla_ref, wlb_ref, ala_ref, alb_ref, w0_ref, a0_ref,
                 wl_ref, al_ref):
    tw = jnp.tanh(jnp.dot(l4_ref[0], wla_ref[...], preferred_element_type=F32))
    wl_ref[...] = w0_ref[...] + jnp.dot(tw.astype(BF16), wlb_ref[...],
                                        preferred_element_type=F32)
    ta = jnp.dot(l5_ref[0], ala_ref[...], preferred_element_type=F32)
    al_ref[...] = a0_ref[...] + jnp.dot(ta.astype(BF16), alb_ref[...],
                                        preferred_element_type=F32)


def _c_lora(lerp, w_la, w_lb, a_la, a_lb, w0, a0, tm=512):
    _, m, d = lerp.shape
    pad_in = lambda w: jnp.pad(w, ((0, 0), (0, LANES - w.shape[1]))).astype(BF16)
    pad_out = lambda w: jnp.pad(w, ((0, LANES - w.shape[0]), (0, 0))).astype(BF16)
    lin = lambda n: pl.BlockSpec((1, tm, d), lambda i, n=n: (n, i, 0))
    win = pl.BlockSpec((d, LANES), lambda i: (0, 0))
    wout = pl.BlockSpec((LANES, d), lambda i: (0, 0))
    vec = pl.BlockSpec((1, d), lambda i: (0, 0))
    out = pl.BlockSpec((tm, d), lambda i: (i, 0))
    return pl.pallas_call(
        _c_lora_body,
        grid=(m // tm,),
        in_specs=[lin(4), lin(5), win, wout, win, wout, vec, vec],
        out_specs=[out, out],
        out_shape=[jax.ShapeDtypeStruct((m, d), F32)] * 2,
        compiler_params=_cparams(("parallel",)),
        name="c_lora",
    )(lerp, lerp, pad_in(w_la), pad_out(w_lb), pad_in(a_la), pad_out(a_lb),
      w0.reshape(1, d), a0.reshape(1, d))


C_HB = 8
DIMS_NN = (((1,), (0,)), ((), ()))
DIMS_NT = (((1,), (1,)), ((), ()))
DIMS_TN = (((0,), (0,)), ((), ()))


def _split(a):
    hi = a.astype(BF16)
    return hi, (a - hi.astype(F32)).astype(BF16)


def _dot3(a, b, dims):
    f = lambda x, y: lax.dot_general(x, y, dims, preferred_element_type=F32)
    return f(a[0], b[0]) + (f(a[0], b[1]) + f(a[1], b[0]))


def _cumsum_rows(x):
    row = lax.broadcasted_iota(I32, x.shape, 0)
    sh = 1
    while sh < x.shape[0]:
        x = x + jnp.where(row >= sh, pltpu.roll(x, sh, 0), 0.0)
        sh *= 2
    return x


def _rwkv_body(*refs, has_init):
    if has_init:
        (r_ref, k_ref, v_ref, g_ref, wl_ref, al_ref, kk_ref, ka_ref, rk_ref, lnw_ref, lnb_ref,
         s0_ref, _og_prompt, og_ref, so_ref, s_scr) = refs
    else:
        (r_ref, k_ref, v_ref, g_ref, wl_ref, al_ref, kk_ref, ka_ref, rk_ref, lnw_ref, lnb_ref,
         og_ref, so_ref, s_scr) = refs
    c = pl.program_id(2)
    n = CHUNK
    hs = HS_C

    @pl.when(c == 0)
    def _():
        if has_init:
            s_scr[...] = s0_ref[0]
        else:
            s_scr[...] = jnp.zeros_like(s_scr)

    r, k, v, g = r_ref[0], k_ref[0], v_ref[0], g_ref[0]
    z = -wl_ref[...]
    softplus = jnp.maximum(z, 0.0) + jnp.log(1.0 + jnp.exp(-jnp.abs(z)))
    logw = -jnp.exp(-softplus - 0.5)
    a = _sigmoid(al_ref[...])
    kkr = k * kk_ref[...]
    k2 = k * (1.0 + (a - 1.0) * ka_ref[...])
    bonus_in = r * k2 * rk_ref[...]

    cs_all = _cumsum_rows(logw)
    ec_all, eci_all, ecp_all = jnp.exp(cs_all), jnp.exp(-cs_all), jnp.exp(cs_all - logw)

    ti = lax.broadcasted_iota(I32, (n, n), 0)
    tj = lax.broadcasted_iota(I32, (n, n), 1)
    eye = jnp.where(tj == ti, 1.0, 0.0)
    gi = lax.broadcasted_iota(I32, (2 * n, 2 * n), 0)
    gj = lax.broadcasted_iota(I32, (2 * n, 2 * n), 1) & (n - 1)
    gmask = ((gi < n) & (gj < gi)) | ((gi >= n) & (gj <= gi - n))
    lvl_masks = []
    bs = 1
    while bs < n:
        sh = bs.bit_length()
        lvl_masks.append((jnp.right_shift(ti, sh) == jnp.right_shift(tj, sh))
                         & ((ti & bs) != 0) & ((tj & bs) == 0))
        bs *= 2

    heads = range(C_HB)
    sls = [slice(j * hs, (j + 1) * hs) for j in heads]
    vs = [v[:, sl] for sl in sls]
    zero = jnp.zeros_like(vs[0])
    p_, q_, kt, rt = [], [], [], []
    for sl in sls:
        kkj = kkr[:, sl]
        kkj = kkj / jnp.maximum(jnp.sqrt(jnp.sum(kkj * kkj, axis=-1, keepdims=True)), 1e-12)
        p_.append(kkj * ecp_all[:, sl])
        q_.append(kkj * a[:, sl] * eci_all[:, sl])
        kt.append(k2[:, sl] * eci_all[:, sl])
        rt.append(r[:, sl] * ec_all[:, sl])
    gm = [jnp.where(gmask, _dot3(_split(jnp.concatenate([p_[j], rt[j]], axis=0)),
                                 _split(jnp.concatenate([q_[j], kt[j]], axis=0)), DIMS_NT), 0.0)
          for j in heads]
    av = [_dot3(_split(gm[j][:n]), _split(jnp.concatenate([zero, vs[j]], axis=0)), DIMS_NN)
          for j in heads]
    a_qp = [gm[j][:n, :n] for j in heads]
    x = [eye - jnp.where(lvl_masks[0], a_qp[j], 0.0) for j in heads]
    for msk in lvl_masks[1:]:
        xs = [_split(x[j]) for j in heads]
        ax = [_dot3(_split(jnp.where(msk, a_qp[j], 0.0)), xs[j], DIMS_NN) for j in heads]
        x = [x[j] - _dot3(xs[j], _split(ax[j]), DIMS_NN) for j in heads]
    xpw = [_dot3(_split(x[j]), _split(jnp.concatenate([p_[j], av[j]], axis=1)), DIMS_NN)
           for j in heads]
    low = [jnp.concatenate([zero, vs[j]], axis=1) for j in heads]
    tb = [_dot3(_split(jnp.concatenate([xpw[j], low[j]], axis=0)),
                _split(jnp.concatenate([-q_[j], kt[j]], axis=0)), DIMS_TN) for j in heads]
    ro = [_dot3(_split(gm[j][n:]), _split(jnp.concatenate([-xpw[j], low[j]], axis=0)), DIMS_NN)
          for j in heads]
    s0 = [_split(s_scr[j]) for j in heads]
    o = [_dot3(_split(rt[j] + ro[j][:, :hs]), s0[j], DIMS_NT) + ro[j][:, hs:] for j in heads]
    for j in heads:
        ecl = ec_all[n - 1:n, sls[j]]
        s_scr[j] = _dot3(s0[j], _split((eye + tb[j][:hs]) * ecl), DIMS_NN) + tb[j][hs:] * ecl
    outs = []
    for j in heads:
        mean = jnp.mean(o[j], axis=-1, keepdims=True)
        var = jnp.mean(jnp.square(o[j] - mean), axis=-1, keepdims=True)
        on = (o[j] - mean) * lax.rsqrt(var + GN_EPS) * lnw_ref[:, sls[j]] + lnb_ref[:, sls[j]]
        bonus = jnp.sum(bonus_in[:, sls[j]], axis=-1, keepdims=True) * vs[j]
        outs.append(on + bonus)
    og_ref[...] = jnp.concatenate(outs, axis=1) * _silu(g)

    @pl.when(c == pl.num_programs(2) - 1)
    def _():
        so_ref[0] = s_scr[...]


def _rwkv(rkvg, wl, al, k_k, k_a, r_k, ln_w, ln_b, s0, og_prompt, *, nseq, nchunk, row0):
    w = C_HB * HS_C
    tok = lambda n: pl.BlockSpec((1, CHUNK, w), lambda s, h, c, n=n: (n, row0 + s * nchunk + c, h))
    tok2 = pl.BlockSpec((CHUNK, w), lambda s, h, c: (row0 + s * nchunk + c, h))
    vec = pl.BlockSpec((1, w), lambda s, h, c: (0, h))
    st = pl.BlockSpec((1, C_HB, HS_C, HS_C), lambda s, h, c: (s, h, 0, 0))
    has_init = s0 is not None
    ins = [rkvg, rkvg, rkvg, rkvg, wl, al] + [p.reshape(1, D_MODEL) for p in (k_k, k_a, r_k, ln_w, ln_b)]
    specs = [tok(0), tok(1), tok(2), tok(3), tok2, tok2] + [vec] * 5
    aliases = {}
    if has_init:
        ins += [s0, og_prompt]
        specs += [st, pl.BlockSpec(memory_space=pl.ANY)]
        aliases = {len(ins) - 1: 0}
    return pl.pallas_call(
        functools.partial(_rwkv_body, has_init=has_init),
        grid=(nseq, H_C // C_HB, nchunk),
        in_specs=specs,
        out_specs=[tok2, st],
        out_shape=[jax.ShapeDtypeStruct((M_ROWS, D_MODEL), F32),
                   jax.ShapeDtypeStruct((nseq, H_C, HS_C, HS_C), F32)],
        input_output_aliases=aliases,
        scratch_shapes=[pltpu.VMEM((C_HB, HS_C, HS_C), F32)],
        compiler_params=_cparams(("parallel", "parallel", "arbitrary")),
        name="rwkv_scan",
    )(*ins)


def kernel(x_prompt, x_sample, cache_a_k, cache_a_v, cache_b_k, cache_b_v, cache_b_kidx, state_c_wkv, state_c_shift, norm_g, final_g, w_out, a_w_in, a_lam, a_subln_g, b_w_in, c_mu, c_w_rkvg, c_w0, c_w_la, c_w_lb, c_a0, c_a_la, c_a_lb, c_k_k, c_k_a, c_r_k, c_ln_w, c_ln_b):
    x = jnp.concatenate([x_prompt.reshape(N_PROMPT, D_MODEL),
                         x_sample.reshape(N_SAMPLE, D_MODEL)], axis=0)
    cos1, sn1, cos2, sn2 = _rope_tables()
    split = lambda arr: (arr[:N_PROMPT], arr[N_PROMPT:])
    outs = {n: [] for n in ("akp", "avp", "aks", "avs", "bkp", "bvp", "bip", "bks", "bvs", "bis",
                            "cwp", "chp", "cws", "chs")}
    y = None
    for i in range(DEPTH):
        kind, j = i % N_MIXERS, i // N_MIXERS
        if kind == 0:
            lam_init = 0.8 - 0.6 * math.exp(-0.3 * i)
            qkvg = _norm_proj(x, norm_g[i], a_w_in[j].astype(BF16), 512, 1024)
            q_rot, k_rot = _rope_a(qkvg, cos1, sn1)
            og_p = _attn_a_prompt(q_rot, k_rot, qkvg, a_lam[j], a_subln_g[j], lam_init)
            og = _attn_a_sample(q_rot, k_rot, qkvg,
                                cache_a_k.reshape(-1, LANES), cache_a_v.reshape(-1, LANES), j,
                                a_lam[j], a_subln_g[j], lam_init, og_p)
            kp, ks = split(k_rot)
            vp, vs = split(qkvg[:, 2 * D_MODEL:3 * D_MODEL])
            outs["akp"].append(kp.reshape(BATCH, SEQ, H_A, 2 * DH_A))
            outs["avp"].append(vp.reshape(BATCH, SEQ, H_A, 2 * DH_A))
            outs["aks"].append(ks.reshape(DEC_BATCH, DEC_SEQ, H_A, 2 * DH_A))
            outs["avs"].append(vs.reshape(DEC_BATCH, DEC_SEQ, H_A, 2 * DH_A))
        elif kind == 1:
            w = b_w_in[j]
            w = jnp.concatenate([w[:, :4096], w[:, 4176:], w[:, 4096:4176],
                                 jnp.zeros((D_MODEL, B_COLS - w.shape[1]), w.dtype)], axis=1)
            proj = _norm_proj(x, norm_g[i], w.astype(BF16), 512, 896)
            q_rot, k_rot, qi_rot, kid, wi = _rope_b(proj, cos1, sn1, cos2, sn2)
            og_p = _dsa_prompt(q_rot, k_rot, qi_rot, kid, wi, proj)
            kidx_past = cache_b_kidx.reshape(-1, D_IDX)
            og = _dsa_sample(q_rot, k_rot, qi_rot, kid, wi, proj,
                             cache_b_k.reshape(-1, KV_B * HD_B), cache_b_v.reshape(-1, KV_B * HD_B),
                             jnp.concatenate([kidx_past, kidx_past], axis=1), j, og_p)
            kp, ks = split(k_rot)
            vp, vs = split(proj[:, B_V0:B_V0 + KV_B * HD_B])
            ip, is_ = split(kid[:, :D_IDX])
            outs["bkp"].append(kp.reshape(BATCH, SEQ, KV_B, HD_B))
            outs["bvp"].append(vp.reshape(BATCH, SEQ, KV_B, HD_B))
            outs["bip"].append(ip.reshape(BATCH, SEQ, D_IDX))
            outs["bks"].append(ks.reshape(DEC_BATCH, DEC_SEQ, KV_B, HD_B))
            outs["bvs"].append(vs.reshape(DEC_BATCH, DEC_SEQ, KV_B, HD_B))
            outs["bis"].append(is_.reshape(DEC_BATCH, DEC_SEQ, D_IDX))
        else:
            shift0 = jnp.concatenate([jnp.zeros((BATCH, D_MODEL), F32), state_c_shift[j]], axis=0)
            lerp, hlast = _c_prep(x, norm_g[i], c_mu[j], shift0)
            rkvg = _c_bmm(lerp, c_w_rkvg[j].astype(BF16))
            wl, al = _c_lora(lerp, c_w_la[j], c_w_lb[j], c_a_la[j], c_a_lb[j], c_w0[j], c_a0[j])
            par = (c_k_k[j], c_k_a[j], c_r_k[j], c_ln_w[j], c_ln_b[j])
            og_p, st_p = _rwkv(rkvg, wl, al, *par, None, None, nseq=BATCH, nchunk=SEQ // CHUNK,
                               row0=0)
            og, st_s = _rwkv(rkvg, wl, al, *par, state_c_wkv[j], og_p, nseq=DEC_BATCH, nchunk=1,
                             row0=N_PROMPT // CHUNK)
            outs["cwp"].append(st_p)
            outs["chp"].append(hlast[:BATCH, 0])
            outs["cws"].append(st_s)
            outs["chs"].append(hlast[BATCH:, 0])
        if i == DEPTH - 1:
            y = _proj_out(og, x, w_out[i].astype(BF16), final_g)
        else:
            x = _proj_out(og, x, w_out[i].astype(BF16))
    yp, ys = split(y)
    st = lambda n: jnp.stack(outs[n])
    return (yp.reshape(BATCH, SEQ, D_MODEL), ys.reshape(DEC_BATCH, DEC_SEQ, D_MODEL),
            st("akp"), st("avp"), st("aks"), st("avs"),
            st("bkp"), st("bvp"), st("bip"), st("bks"), st("bvs"), st("bis"),
            st("cwp"), st("chp"), st("cws"), st("chs"))
```

```python
import functools
import math

import jax
import jax.numpy as jnp
from jax import lax
from jax.experimental import pallas as pl
from jax.experimental.pallas import tpu as pltpu

F32 = jnp.float32
BF16 = jnp.bfloat16
I32 = jnp.int32

D_MODEL = 2048
BATCH = 4
SEQ = 2048
DEPTH = 4
DEC_BATCH = 8
DEC_SEQ = 64
PAST_LEN = 4096
CHUNK = 64
N_MIXERS = 3
ROPE_THETA = 10000.0
RMS_EPS = 1e-6
DH_A = 64
H_A = 16
SUBLN_EPS = 1e-5
HD_B = 128
H_B = 16
KV_B = 4
H_IDX = 16
D_IDX = 64
TOPK_MAX = 256
HS_C = 64
H_C = 32
R_DECAY = 96
R_ICLR = 96
GN_EPS = 64e-5

N_PROMPT = BATCH * SEQ
N_SAMPLE = DEC_BATCH * DEC_SEQ
M_ROWS = N_PROMPT + N_SAMPLE
LANES = 128
VMEM_LIMIT = 56 * 1024 * 1024
NEG_INF = float("-inf")
INT_MIN = -2 ** 31

B_Q0, B_K0, B_V0, B_QI0, B_G0, B_KI0 = 0, 2048, 2560, 3072, 4096, 6144
B_COLS = 6272

assert PAST_LEN % CHUNK == 0 and DEC_SEQ == CHUNK
LOG2_CHUNK = CHUNK.bit_length() - 1
assert 1 << LOG2_CHUNK == CHUNK


def _chunk_of(pos):
    return jnp.right_shift(pos, LOG2_CHUNK)


def _cparams(sem):
    return pltpu.CompilerParams(dimension_semantics=sem, vmem_limit_bytes=VMEM_LIMIT)


def _sigmoid(x):
    return 1.0 / (1.0 + jnp.exp(-x))


def _silu(x):
    return x * _sigmoid(x)


def _rms(x, g):
    return x * lax.rsqrt(jnp.mean(x * x, axis=-1, keepdims=True) + RMS_EPS) * g


def _dot_nt(a, b, precision=None):
    return lax.dot_general(a, b, (((1,), (1,)), ((), ())), precision=precision,
                           preferred_element_type=F32)


def _norm_proj_body(x_ref, g_ref, w_ref, o_ref, h_scr):
    @pl.when(pl.program_id(1) == 0)
    def _():
        h_scr[...] = _rms(x_ref[...], g_ref[...]).astype(BF16)
    o_ref[...] = jnp.dot(h_scr[...], w_ref[...], preferred_element_type=F32)


def _norm_proj(x, g, w, tm, tn):
    m, d = x.shape
    n = w.shape[1]
    return pl.pallas_call(
        _norm_proj_body,
        grid=(m // tm, n // tn),
        in_specs=[pl.BlockSpec((tm, d), lambda i, j: (i, 0)),
                  pl.BlockSpec((1, d), lambda i, j: (0, 0)),
                  pl.BlockSpec((d, tn), lambda i, j: (0, j))],
        out_specs=pl.BlockSpec((tm, tn), lambda i, j: (i, j)),
        out_shape=jax.ShapeDtypeStruct((m, n), F32),
        scratch_shapes=[pltpu.VMEM((tm, d), BF16)],
        compiler_params=_cparams(("parallel", "arbitrary")),
        name="norm_proj",
    )(x, g.reshape(1, d), w)


def _proj_out_body(ogp_ref, ogs_ref, x_ref, w_ref, *rest, n_prompt_tiles):
    o_ref = rest[-1]

    def emit(og_ref):
        xn = x_ref[...] + jnp.dot(og_ref[...].astype(BF16), w_ref[...], preferred_element_type=F32)
        o_ref[...] = _rms(xn, rest[0][...]) if len(rest) == 2 else xn

    is_prompt = pl.program_id(0) < n_prompt_tiles
    pl.when(is_prompt)(lambda: emit(ogp_ref))
    pl.when(jnp.logical_not(is_prompt))(lambda: emit(ogs_ref))


def _proj_out(og_prompt, og_sample, x, w, final_g=None, tm=256):
    m, d = x.shape
    npt = N_PROMPT // tm
    row = pl.BlockSpec((tm, d), lambda i: (i, 0))
    rowp = pl.BlockSpec((tm, d), lambda i: (jnp.minimum(i, npt - 1), 0))
    rows = pl.BlockSpec((tm, d), lambda i: (jnp.maximum(i - npt, 0), 0))
    full = pl.BlockSpec((d, d), lambda i: (0, 0))
    extra, extra_specs = (), []
    if final_g is not None:
        extra, extra_specs = (final_g.reshape(1, d),), [pl.BlockSpec((1, d), lambda i: (0, 0))]
    return pl.pallas_call(
        functools.partial(_proj_out_body, n_prompt_tiles=npt),
        grid=(m // tm,),
        in_specs=[rowp, rows, row, full] + extra_specs,
        out_specs=row,
        out_shape=jax.ShapeDtypeStruct((m, d), F32),
        compiler_params=_cparams(("parallel",)),
        name="proj_out",
    )(og_prompt, og_sample, x, w, *extra)


def _rope_tables():
    pos = jnp.concatenate([jnp.tile(jnp.arange(SEQ), BATCH),
                           jnp.tile(PAST_LEN + jnp.arange(DEC_SEQ), DEC_BATCH)]).astype(F32)

    def table(dh, reps):
        half = dh // 2
        inv = jnp.power(ROPE_THETA, -jnp.arange(half, dtype=F32) * (2.0 / dh))
        ang = pos[:, None] * inv[None, :]
        cos, sin = jnp.cos(ang), jnp.sin(ang)
        return (jnp.tile(jnp.concatenate([cos, cos], axis=1), (1, reps)),
                jnp.tile(jnp.concatenate([-sin, sin], axis=1), (1, reps)))

    cos1, sn1 = table(64, 2)
    cos2, sn2 = table(128, 1)
    return cos1, sn1, cos2, sn2


def _rope64(x, cos, sn):
    lane = lax.broadcasted_iota(I32, x.shape, 1)
    partner = jnp.where((lane & 63) < 32, pltpu.roll(x, 96, 1), pltpu.roll(x, 32, 1))
    return x * cos + partner * sn


def _rope128(x, cos, sn):
    return x * cos + pltpu.roll(x, 64, 1) * sn


def _rope_a_body(q_ref, k_ref, cos_ref, sn_ref, qo_ref, ko_ref):
    cos, sn = cos_ref[...], sn_ref[...]
    for h in range(H_A):
        sl = slice(h * LANES, (h + 1) * LANES)
        qo_ref[:, sl] = _rope64(q_ref[:, sl], cos, sn) * (DH_A ** -0.5)
        ko_ref[:, sl] = _rope64(k_ref[:, sl], cos, sn)


def _rope_a(qkvg, cos1, sn1, tm=256):
    m = qkvg.shape[0]
    blk = lambda c: pl.BlockSpec((tm, D_MODEL), lambda i, c=c: (i, c))
    tab = pl.BlockSpec((tm, LANES), lambda i: (i, 0))
    return pl.pallas_call(
        _rope_a_body,
        grid=(m // tm,),
        in_specs=[blk(0), blk(1), tab, tab],
        out_specs=[blk(0), blk(0)],
        out_shape=[jax.ShapeDtypeStruct((m, D_MODEL), F32)] * 2,
        compiler_params=_cparams(("parallel",)),
        name="rope_a",
    )(qkvg, qkvg, cos1, sn1)


def _stack_maps_t(q):
    lane = lax.broadcasted_iota(I32, q.shape, 1)
    qs = jnp.concatenate([jnp.where(lane < DH_A, q, 0.0), jnp.where(lane >= DH_A, q, 0.0)], axis=0)
    return qs.T.astype(BF16)


def _flash_t_step(s, vt, carry):
    m, l, acc = carry
    m_new = jnp.maximum(m, jnp.max(s, axis=0, keepdims=True))
    alpha = jnp.exp(m - m_new)
    p = jnp.exp(s - m_new)
    l = alpha * l + jnp.sum(p, axis=0, keepdims=True)
    pv = lax.dot_general(vt, p.astype(BF16), DIMS_TN, preferred_element_type=F32)
    return m_new, l, alpha * acc + pv


def _flash_t_init(n):
    return (jnp.full((1, n), -1e30, F32), jnp.zeros((1, n), F32), jnp.zeros((LANES, n), F32))


def _diff_epilogue(l, acc, tq, lam_ref, sg, g, lam_init):
    o = (acc / l).T
    lp = lam_ref[...]
    lam = (jnp.exp(jnp.sum(lp[0:1] * lp[1:2], axis=-1, keepdims=True))
           - jnp.exp(jnp.sum(lp[2:3] * lp[3:4], axis=-1, keepdims=True)) + lam_init)
    o = o[:tq] - lam * o[tq:]
    o = o * lax.rsqrt(jnp.mean(o * o, axis=-1, keepdims=True) + SUBLN_EPS) * sg
    o = o * (1.0 - lam_init)
    return o * _silu(g)


A_HP = 4


def _attn_a_prompt_body(q_ref, k_ref, v_ref, g_ref, lam_ref, sg_ref, o_ref, *, tq, lam_init):
    qi = pl.program_id(2)
    hsl = [slice(h * LANES, (h + 1) * LANES) for h in range(A_HP)]
    qst = [_stack_maps_t(q_ref[:, sl]) for sl in hsl]
    n = 2 * tq
    krow = lax.broadcasted_iota(I32, (tq, n), 0)
    qcol = lax.broadcasted_iota(I32, (tq, n), 1) & (tq - 1)
    diag_mask = _chunk_of(krow) <= _chunk_of(qcol)

    def tile(kt, carries, masked):
        off = pl.multiple_of(kt * tq, tq)
        ss = [jnp.dot(k_ref[pl.ds(off, tq), sl].astype(BF16), qst[h], preferred_element_type=F32)
              for h, sl in enumerate(hsl)]
        out = []
        for h, sl in enumerate(hsl):
            s = jnp.where(diag_mask, ss[h], NEG_INF) if masked else ss[h]
            out.append(_flash_t_step(s, v_ref[pl.ds(off, tq), sl].astype(BF16), carries[h]))
        return tuple(out)

    carries = lax.fori_loop(0, qi, lambda kt, c: tile(kt, c, False),
                            tuple(_flash_t_init(n) for _ in hsl))
    carries = tile(qi, carries, True)
    for h, sl in enumerate(hsl):
        _, l, acc = carries[h]
        o_ref[:, sl] = _diff_epilogue(l, acc, tq, lam_ref, sg_ref[...], g_ref[:, sl], lam_init)


def _attn_a_prompt(q_rot, k_rot, qkvg, lam_p, subln_g, lam_init, tq=256):
    nq = SEQ // tq
    w = A_HP * LANES
    nh = H_A // A_HP
    return pl.pallas_call(
        functools.partial(_attn_a_prompt_body, tq=tq, lam_init=lam_init),
        grid=(BATCH, nh, nq),
        in_specs=[pl.BlockSpec((tq, w), lambda b, h, i: (b * nq + i, h)),
                  pl.BlockSpec((SEQ, w), lambda b, h, i: (b, h)),
                  pl.BlockSpec((SEQ, w), lambda b, h, i: (b, 2 * nh + h)),
                  pl.BlockSpec((tq, w), lambda b, h, i: (b * nq + i, 3 * nh + h)),
                  pl.BlockSpec((4, DH_A), lambda b, h, i: (0, 0)),
                  pl.BlockSpec((1, LANES), lambda b, h, i: (0, 0))],
        out_specs=pl.BlockSpec((tq, w), lambda b, h, i: (b * nq + i, h)),
        out_shape=jax.ShapeDtypeStruct((N_PROMPT, D_MODEL), F32),
        compiler_params=_cparams(("parallel", "parallel", "arbitrary")),
        name="attn_a_prompt",
    )(q_rot, k_rot, qkvg, qkvg, lam_p, subln_g.reshape(1, LANES))


def _attn_a_sample_body(q_ref, kp_ref, vp_ref, k_ref, v_ref, g_ref, lam_ref, sg_ref,
                        o_ref, qst_scr, ml_scr, acc_scr, *, tk, lam_init):
    kt = pl.program_id(1)
    tq = DEC_SEQ
    hsl = [slice(h * LANES, (h + 1) * LANES) for h in range(H_A)]

    @pl.when(kt == 0)
    def _():
        for h, sl in enumerate(hsl):
            qst_scr[h] = _stack_maps_t(q_ref[:, sl])
        m0, l0, acc0 = _flash_t_init(2 * tq)
        for h in range(H_A):
            ml_scr[h, 0:1, :] = m0
            ml_scr[h, 1:2, :] = l0
            acc_scr[h] = acc0

    def load(h):
        return ml_scr[h, 0:1, :], ml_scr[h, 1:2, :], acc_scr[h]

    def store(h, carry):
        ml_scr[h, 0:1, :], ml_scr[h, 1:2, :], acc_scr[h] = carry

    def head_pair(i, _):
        hs = (2 * i, 2 * i + 1)
        ss = [jnp.dot(kp_ref[pl.ds(h, tk, stride=H_A), :].astype(BF16), qst_scr[h],
                      preferred_element_type=F32) for h in hs]
        for h, s in zip(hs, ss):
            vb = vp_ref[pl.ds(h, tk, stride=H_A), :].astype(BF16)
            store(h, _flash_t_step(s, vb, load(h)))
        return 0

    lax.fori_loop(0, H_A // 2, head_pair, 0)

    @pl.when(kt == pl.num_programs(1) - 1)
    def _():
        for h, sl in enumerate(hsl):
            s = jnp.dot(k_ref[:, sl].astype(BF16), qst_scr[h], preferred_element_type=F32)
            _, l, acc = _flash_t_step(s, v_ref[:, sl].astype(BF16), load(h))
            o_ref[:, sl] = _diff_epilogue(l, acc, tq, lam_ref, sg_ref[...], g_ref[:, sl], lam_init)


def _attn_a_sample(q_rot, k_rot, qkvg, k_past, v_past, layer, lam_p, subln_g, lam_init, tk=512):
    r0 = N_PROMPT // DEC_SEQ
    nkt = PAST_LEN // tk
    new = lambda c: pl.BlockSpec((DEC_SEQ, D_MODEL), lambda b, t, c=c: (r0 + b, c))
    past = pl.BlockSpec((tk * H_A, LANES), lambda b, t: ((layer * DEC_BATCH + b) * nkt + t, 0))
    return pl.pallas_call(
        functools.partial(_attn_a_sample_body, tk=tk, lam_init=lam_init),
        grid=(DEC_BATCH, nkt),
        in_specs=[new(0), past, past, new(0), new(2), new(3),
                  pl.BlockSpec((4, DH_A), lambda b, t: (0, 0)),
                  pl.BlockSpec((1, LANES), lambda b, t: (0, 0))],
        out_specs=pl.BlockSpec((DEC_SEQ, D_MODEL), lambda b, t: (b, 0)),
        out_shape=jax.ShapeDtypeStruct((N_SAMPLE, D_MODEL), F32),
        scratch_shapes=[pltpu.VMEM((H_A, LANES, 2 * DEC_SEQ), BF16),
                        pltpu.VMEM((H_A, 8, 2 * DEC_SEQ), F32),
                        pltpu.VMEM((H_A, LANES, 2 * DEC_SEQ), F32)],
        compiler_params=_cparams(("parallel", "arbitrary")),
        name="attn_a_sample",
    )(q_rot, k_past, v_past, k_rot, qkvg, qkvg, lam_p, subln_g.reshape(1, LANES))


def _rope_b_body(q_ref, k_ref, qi_ref, sm_ref, c1_ref, s1_ref, c2_ref, s2_ref,
                 qo_ref, ko_ref, qio_ref, kid_ref, wi_ref):
    c1, s1, c2, s2 = c1_ref[...], s1_ref[...], c2_ref[...], s2_ref[...]
    for h in range(H_B):
        sl = slice(h * LANES, (h + 1) * LANES)
        qo_ref[:, sl] = _rope128(q_ref[:, sl], c2, s2) * (HD_B ** -0.5)
    for h in range(KV_B):
        sl = slice(h * LANES, (h + 1) * LANES)
        ko_ref[:, sl] = _rope128(k_ref[:, sl], c2, s2)
    for h in range(H_IDX * D_IDX // LANES):
        sl = slice(h * LANES, (h + 1) * LANES)
        qio_ref[:, sl] = _rope64(qi_ref[:, sl], c1, s1) * (D_IDX ** -0.5)
    sm = sm_ref[...]
    lane = lax.broadcasted_iota(I32, sm.shape, 1)
    kr = _rope64(sm, c1, s1)
    kid_ref[...] = jnp.where(lane < D_IDX, kr, pltpu.roll(kr, D_IDX, 1))
    wi_ref[...] = pltpu.roll(sm, D_IDX, 1) * (H_IDX ** -0.5)


def _rope_b(proj, cos1, sn1, cos2, sn2, tm=256):
    m = proj.shape[0]
    tab = pl.BlockSpec((tm, LANES), lambda i: (i, 0))
    blk = lambda w, c: pl.BlockSpec((tm, w), lambda i, c=c: (i, c))
    widths = (H_B * HD_B, KV_B * HD_B, H_IDX * D_IDX, LANES, LANES)
    return pl.pallas_call(
        _rope_b_body,
        grid=(m // tm,),
        in_specs=[blk(2048, 0), blk(512, B_K0 // 512), blk(1024, B_QI0 // 1024),
                  blk(LANES, B_KI0 // LANES), tab, tab, tab, tab],
        out_specs=[blk(w, 0) for w in widths],
        out_shape=[jax.ShapeDtypeStruct((m, w), F32) for w in widths],
        compiler_params=_cparams(("parallel",)),
        name="rope_b",
    )(proj, proj, proj, proj, cos1, sn1, cos2, sn2)


POS_BITS = 13
DSA_BUCKET = 512


def _count(pred):
    return jnp.sum(jnp.where(pred, 1.0, 0.0), axis=-1, keepdims=True)


def _select_topk(scores, poss, n_sel):
    keys = []
    for s in scores:
        bits = lax.bitcast_convert_type(s + 0.0, I32)
        keys.append(jnp.where(bits < 0, bits ^ 0x7FFFFFFF, bits))
    tq = scores[0].shape[0]
    n_sel = float(n_sel)

    def thr_step(i, t):
        cand_bits = t | lax.shift_left(jnp.int32(1), 31 - i)
        cand = cand_bits ^ INT_MIN
        cnt = sum(_count(k >= cand) for k in keys)
        return jnp.where(cnt >= n_sel, cand_bits, t)

    t = lax.fori_loop(0, 32, thr_step, jnp.zeros((tq, 1), I32))
    thr = t ^ INT_MIN
    need = n_sel - sum(_count(k > thr) for k in keys)
    n_tied = sum(_count(k == thr) for k in keys)

    def tie_step(i, j):
        cand = j | lax.shift_left(jnp.int32(1), POS_BITS - 1 - i)
        cnt = sum(_count((k == thr) & (p < cand)) for k, p in zip(keys, poss))
        return jnp.where(cnt < need, cand, j)

    j = lax.cond(jnp.max(n_tied - need) > 0.0,
                 lambda: lax.fori_loop(0, POS_BITS, tie_step, jnp.zeros((tq, 1), I32)),
                 lambda: jnp.full((tq, 1), (1 << POS_BITS) - 1, I32))
    return [(k > thr) | ((k == thr) & (p <= j)) for k, p in zip(keys, poss)]


def _dsa_body(*refs, tq, seg_lens, seg_pos0, q_pos0, causal, n_sel, bucket):
    nseg = len(seg_lens)
    n_in = 4 + 3 * nseg
    assert max(seg_pos0[i] + seg_lens[i] for i in range(nseg)) <= 1 << POS_BITS
    if bucket is None:
        _dsa_run(refs, tq, seg_lens, seg_pos0, q_pos0, causal, n_sel, n_in)
        return
    assert causal and nseg == 1 and seg_pos0[0] == 0 and q_pos0 == 0 and bucket % tq == 0
    last = (pl.program_id(1) * tq) // bucket
    for v in range(seg_lens[0] // bucket):
        pl.when(last == v)(functools.partial(_dsa_run, refs, tq, ((v + 1) * bucket,), seg_pos0,
                                             q_pos0, causal, n_sel, n_in))


def _dsa_run(refs, tq, seg_lens, seg_pos0, q_pos0, causal, n_sel, n_in):
    nseg = len(seg_lens)
    qi_ref, wi_ref, q_ref, g_ref = refs[:4]
    kid_refs = refs[4:4 + nseg]
    k_refs = refs[4 + nseg:4 + 2 * nseg]
    v_refs = refs[4 + 2 * nseg:4 + 3 * nseg]
    o_ref = refs[n_in]
    bias_refs = refs[n_in + 1:]
    n_rep = H_B // KV_B

    @pl.when(pl.program_id(2) == 0)
    def _():
        wi = wi_ref[...]
        qpos = q_pos0 + pl.program_id(1) * tq
        scores, poss, adms = [], [], []
        for si in range(nseg):
            sl = seg_lens[si]
            kid = kid_refs[si][0:sl, :].astype(BF16)
            sc = jnp.zeros((tq, sl), F32)
            for p in range(H_IDX // 2):
                qp = qi_ref[:, p * LANES:(p + 1) * LANES]
                lane = lax.broadcasted_iota(I32, qp.shape, 1)
                qs = jnp.concatenate([jnp.where(lane < D_IDX, qp, 0.0),
                                      jnp.where(lane >= D_IDX, qp, 0.0)], axis=0).astype(BF16)
                lg = jnp.maximum(_dot_nt(qs, kid), 0.0)
                sc = sc + wi[:, 2 * p:2 * p + 1] * lg[:tq] + wi[:, 2 * p + 1:2 * p + 2] * lg[tq:]
            kpos = seg_pos0[si] + lax.broadcasted_iota(I32, (tq, sl), 1)
            if causal:
                qrow = qpos + lax.broadcasted_iota(I32, (tq, sl), 0)
                adm = _chunk_of(kpos) <= _chunk_of(qrow)
                sc = jnp.where(adm, sc, NEG_INF)
            else:
                adm = None
            scores.append(sc)
            poss.append(kpos)
            adms.append(adm)
        sels = _select_topk(scores, poss, n_sel)
        for si in range(nseg):
            bias = jnp.where(sels[si], 0.0, NEG_INF)
            if adms[si] is not None:
                bias = jnp.where(adms[si], bias, NEG_INF)
            bias_refs[si][:, 0:seg_lens[si]] = bias

    q = q_ref[...]
    qg = jnp.concatenate([q[:, r * LANES:(r + 1) * LANES] for r in range(n_rep)],
                         axis=0).astype(BF16)
    ss = []
    for si in range(nseg):
        b = bias_refs[si][:, 0:seg_lens[si]]
        s = _dot_nt(qg, k_refs[si][0:seg_lens[si], :].astype(BF16))
        ss.append(s + jnp.concatenate([b] * n_rep, axis=0))
    m = functools.reduce(jnp.maximum, [jnp.max(s, axis=-1, keepdims=True) for s in ss])
    l = 0.0
    acc = 0.0
    for si in range(nseg):
        p = jnp.exp(ss[si] - m)
        l = l + jnp.sum(p, axis=-1, keepdims=True)
        acc = acc + jnp.dot(p.astype(BF16), v_refs[si][0:seg_lens[si], :].astype(BF16),
                            preferred_element_type=F32)
    o = acc / l
    o = jnp.concatenate([o[r * tq:(r + 1) * tq] for r in range(n_rep)], axis=1)
    o_ref[...] = o * _silu(g_ref[...])


def _dsa_prompt(q_rot, k_rot, qi_rot, kid, wi, proj, tq=128):
    nq = SEQ // tq
    gw = KV_B * HD_B
    row = lambda w, c0: pl.BlockSpec((tq, w), lambda b, i, g, c0=c0: (b * nq + i, c0 + g))
    row0 = lambda w: pl.BlockSpec((tq, w), lambda b, i, g: (b * nq + i, 0))
    seq = lambda c0: pl.BlockSpec((SEQ, LANES), lambda b, i, g, c0=c0: (b, c0 + g))
    n_sel = min(TOPK_MAX, SEQ // 4)
    return pl.pallas_call(
        functools.partial(_dsa_body, tq=tq, seg_lens=(SEQ,), seg_pos0=(0,), q_pos0=0,
                          causal=True, n_sel=n_sel, bucket=DSA_BUCKET),
        grid=(BATCH, nq, KV_B),
        in_specs=[row0(H_IDX * D_IDX), row0(LANES), row(gw, 0), row(gw, B_G0 // gw),
                  pl.BlockSpec((SEQ, LANES), lambda b, i, g: (b, 0)),
                  seq(0), seq(B_V0 // LANES)],
        out_specs=row(gw, 0),
        out_shape=jax.ShapeDtypeStruct((N_PROMPT, D_MODEL), F32),
        scratch_shapes=[pltpu.VMEM((tq, SEQ), F32)],
        compiler_params=_cparams(("parallel", "parallel", "arbitrary")),
        name="dsa_prompt",
    )(qi_rot, wi, q_rot, proj, kid, k_rot, proj)


def _dsa_sample(q_rot, k_rot, qi_rot, kid, wi, proj, k_past, v_past, kid_past, layer):
    tq = DEC_SEQ
    r0 = N_PROMPT // tq
    gw = KV_B * HD_B
    row = lambda w, c0: pl.BlockSpec((tq, w), lambda b, i, g, c0=c0: (r0 + b, c0 + g))
    row0 = lambda w: pl.BlockSpec((tq, w), lambda b, i, g: (r0 + b, 0))
    past = pl.BlockSpec((PAST_LEN, LANES), lambda b, i, g: (layer * DEC_BATCH + b, g))
    s_all = PAST_LEN + DEC_SEQ
    n_sel = min(TOPK_MAX, s_all // 4)
    return pl.pallas_call(
        functools.partial(_dsa_body, tq=tq, seg_lens=(PAST_LEN, DEC_SEQ),
                          seg_pos0=(0, PAST_LEN), q_pos0=PAST_LEN, causal=False, n_sel=n_sel,
                          bucket=None),
        grid=(DEC_BATCH, 1, KV_B),
        in_specs=[row0(H_IDX * D_IDX), row0(LANES), row(gw, 0), row(gw, B_G0 // gw),
                  pl.BlockSpec((PAST_LEN, LANES), lambda b, i, g: (layer * DEC_BATCH + b, 0)),
                  row0(LANES),
                  past, row(LANES, 0),
                  past, row(LANES, B_V0 // LANES)],
        out_specs=pl.BlockSpec((tq, gw), lambda b, i, g: (b, g)),
        out_shape=jax.ShapeDtypeStruct((N_SAMPLE, D_MODEL), F32),
        scratch_shapes=[pltpu.VMEM((tq, PAST_LEN), F32), pltpu.VMEM((tq, DEC_SEQ), F32)],
        compiler_params=_cparams(("parallel", "arbitrary", "arbitrary")),
        name="dsa_sample",
    )(qi_rot, wi, q_rot, proj, kid_past, kid, k_past, k_rot, v_past, proj)


C_TM = 64
C_TILES_PER_SEQ = SEQ // C_TM
C_PROMPT_TILES = N_PROMPT // C_TM


def _c_seq_id(i):
    return jnp.where(i < C_PROMPT_TILES, i // C_TILES_PER_SEQ, i - C_PROMPT_TILES + BATCH)


def _c_prep_body(x_ref, g_ref, mu_ref, sh_ref, l_ref, hl_ref, carry):
    i = pl.program_id(0)

    @pl.when(i == 0)
    def _():
        carry[...] = jnp.zeros_like(carry)

    h = _rms(x_ref[...], g_ref[...])
    start = jnp.logical_or(i >= C_PROMPT_TILES, i % C_TILES_PER_SEQ == 0)
    first = jnp.where(start, sh_ref[0], carry[...])
    row = lax.broadcasted_iota(I32, h.shape, 0)
    prev = jnp.where(row == 0, first, pltpu.roll(h, 1, 0))
    last = h[C_TM - 1:C_TM, :]
    carry[...] = last
    hl_ref[0] = last
    d = prev - h
    for n in range(6):
        l_ref[n] = (h + d * mu_ref[n:n + 1, :]).astype(BF16)


def _c_prep(x, g, mu, shift0):
    m, d = x.shape
    nseq = shift0.shape[0]
    return pl.pallas_call(
        _c_prep_body,
        grid=(m // C_TM,),
        in_specs=[pl.BlockSpec((C_TM, d), lambda i: (i, 0)),
                  pl.BlockSpec((1, d), lambda i: (0, 0)),
                  pl.BlockSpec((6, d), lambda i: (0, 0)),
                  pl.BlockSpec((1, 1, d), lambda i: (_c_seq_id(i), 0, 0))],
        out_specs=[pl.BlockSpec((6, C_TM, d), lambda i: (0, i, 0)),
                   pl.BlockSpec((1, 1, d), lambda i: (_c_seq_id(i), 0, 0))],
        out_shape=[jax.ShapeDtypeStruct((6, m, d), BF16),
                   jax.ShapeDtypeStruct((nseq, 1, d), F32)],
        scratch_shapes=[pltpu.VMEM((1, d), F32)],
        compiler_params=_cparams(("arbitrary",)),
        name="c_prep",
    )(x, g.reshape(1, d), mu, shift0.reshape(nseq, 1, d))


def _bmm_body(l_ref, w_ref, o_ref):
    o_ref[0] = jnp.dot(l_ref[0], w_ref[0], preferred_element_type=F32)


def _c_bmm(lerp, w, tm=512, tn=1024):
    _, m, d = lerp.shape
    nb, _, n = w.shape
    return pl.pallas_call(
        _bmm_body,
        grid=(nb, m // tm, n // tn),
        in_specs=[pl.BlockSpec((1, tm, d), lambda b, i, j: (b, i, 0)),
                  pl.BlockSpec((1, d, tn), lambda b, i, j: (b, 0, j))],
        out_specs=pl.BlockSpec((1, tm, tn), lambda b, i, j: (b, i, j)),
        out_shape=jax.ShapeDtypeStruct((nb, m, n), F32),
        compiler_params=_cparams(("parallel", "parallel", "arbitrary")),
        name="c_bmm",
    )(lerp, w)


def _c_lora_body(l4_ref, l5_ref, wla_ref, wlb_ref, ala_ref, alb_ref, w0_ref, a0_ref,
                 wl_ref, al_ref):
    tw = jnp.tanh(jnp.dot(l4_ref[0], wla_ref[...], preferred_element_type=F32))
    wl_ref[...] = w0_ref[...] + jnp.dot(tw.astype(BF16), wlb_ref[...],
                                        preferred_element_type=F32)
    ta = jnp.dot(l5_ref[0], ala_ref[...], preferred_element_type=F32)
    al_ref[...] = a0_ref[...] + jnp.dot(ta.astype(BF16), alb_ref[...],
                                        preferred_element_type=F32)


def _c_lora(lerp, w_la, w_lb, a_la, a_lb, w0, a0, tm=512):
    _, m, d = lerp.shape
    pad_in = lambda w: jnp.pad(w, ((0, 0), (0, LANES - w.shape[1]))).astype(BF16)
    pad_out = lambda w: jnp.pad(w, ((0, LANES - w.shape[0]), (0, 0))).astype(BF16)
    lin = lambda n: pl.BlockSpec((1, tm, d), lambda i, n=n: (n, i, 0))
    win = pl.BlockSpec((d, LANES), lambda i: (0, 0))
    wout = pl.BlockSpec((LANES, d), lambda i: (0, 0))
    vec = pl.BlockSpec((1, d), lambda i: (0, 0))
    out = pl.BlockSpec((tm, d), lambda i: (i, 0))
    return pl.pallas_call(
        _c_lora_body,
        grid=(m // tm,),
        in_specs=[lin(4), lin(5), win, wout, win, wout, vec, vec],
        out_specs=[out, out],
        out_shape=[jax.ShapeDtypeStruct((m, d), F32)] * 2,
        compiler_params=_cparams(("parallel",)),
        name="c_lora",
    )(lerp, lerp, pad_in(w_la), pad_out(w_lb), pad_in(a_la), pad_out(a_lb),
      w0.reshape(1, d), a0.reshape(1, d))


C_HB = 8
DIMS_NN = (((1,), (0,)), ((), ()))
DIMS_NT = (((1,), (1,)), ((), ()))
DIMS_TN = (((0,), (0,)), ((), ()))


def _dot3(a, b, dims):
    (ca,), (cb,) = dims[0]
    ah = a.astype(BF16).astype(F32)
    bh = b.astype(BF16).astype(F32)
    sa = jnp.concatenate([ah, a - ah, ah], axis=ca).astype(BF16)
    sb = jnp.concatenate([bh, bh, b - bh], axis=cb).astype(BF16)
    return lax.dot_general(sa, sb, dims, preferred_element_type=F32)


def _cumsum_rows(x):
    row = lax.broadcasted_iota(I32, x.shape, 0)
    sh = 1
    while sh < x.shape[0]:
        x = x + jnp.where(row >= sh, pltpu.roll(x, sh, 0), 0.0)
        sh *= 2
    return x


def _rwkv_body(*refs, has_init):
    if has_init:
        (r_ref, k_ref, v_ref, g_ref, wl_ref, al_ref, kk_ref, ka_ref, rk_ref, lnw_ref, lnb_ref,
         s0_ref, og_ref, so_ref, s_scr) = refs
    else:
        (r_ref, k_ref, v_ref, g_ref, wl_ref, al_ref, kk_ref, ka_ref, rk_ref, lnw_ref, lnb_ref,
         og_ref, so_ref, s_scr) = refs
    c = pl.program_id(2)
    n = CHUNK
    hs = HS_C

    @pl.when(c == 0)
    def _():
        if has_init:
            s_scr[...] = s0_ref[0]
        else:
            s_scr[...] = jnp.zeros_like(s_scr)

    r, k, v, g = r_ref[0], k_ref[0], v_ref[0], g_ref[0]
    z = -wl_ref[...]
    softplus = jnp.maximum(z, 0.0) + jnp.log(1.0 + jnp.exp(-jnp.abs(z)))
    logw = -jnp.exp(-softplus - 0.5)
    a = _sigmoid(al_ref[...])
    kkr = k * kk_ref[...]
    k2 = k * (1.0 + (a - 1.0) * ka_ref[...])
    bonus_in = r * k2 * rk_ref[...]

    cs_all = _cumsum_rows(logw)
    ec_all, eci_all, ecp_all = jnp.exp(cs_all), jnp.exp(-cs_all), jnp.exp(cs_all - logw)

    ti = lax.broadcasted_iota(I32, (n, n), 0)
    tj = lax.broadcasted_iota(I32, (n, n), 1)
    eye = jnp.where(tj == ti, 1.0, 0.0)
    gi = lax.broadcasted_iota(I32, (2 * n, 2 * n), 0)
    gj = lax.broadcasted_iota(I32, (2 * n, 2 * n), 1) & (n - 1)
    gmask = ((gi < n) & (gj < gi)) | ((gi >= n) & (gj <= gi - n))
    lvl_masks = []
    bs = 1
    while bs < n:
        sh = bs.bit_length()
        lvl_masks.append((jnp.right_shift(ti, sh) == jnp.right_shift(tj, sh))
                         & ((ti & bs) != 0) & ((tj & bs) == 0))
        bs *= 2

    heads = range(C_HB)
    sls = [slice(j * hs, (j + 1) * hs) for j in heads]
    vs = [v[:, sl] for sl in sls]
    zero = jnp.zeros_like(vs[0])
    p_, q_, kt, rt = [], [], [], []
    for sl in sls:
        kkj = kkr[:, sl]
        kkj = kkj / jnp.maximum(jnp.sqrt(jnp.sum(kkj * kkj, axis=-1, keepdims=True)), 1e-12)
        p_.append(kkj * ecp_all[:, sl])
        q_.append(kkj * a[:, sl] * eci_all[:, sl])
        kt.append(k2[:, sl] * eci_all[:, sl])
        rt.append(r[:, sl] * ec_all[:, sl])
    gm = [jnp.where(gmask, _dot3(jnp.concatenate([p_[j], rt[j]], axis=0),
                                 jnp.concatenate([q_[j], kt[j]], axis=0), DIMS_NT), 0.0)
          for j in heads]
    av = [_dot3(gm[j][:n], jnp.concatenate([zero, vs[j]], axis=0), DIMS_NN)
          for j in heads]
    a_qp = [gm[j][:n, :n] for j in heads]
    x = [eye - jnp.where(lvl_masks[0], a_qp[j], 0.0) for j in heads]
    for msk in lvl_masks[1:]:
        ax = [_dot3(jnp.where(msk, a_qp[j], 0.0), x[j], DIMS_NN) for j in heads]
        x = [x[j] - _dot3(x[j], ax[j], DIMS_NN) for j in heads]
    xpw = [_dot3(x[j], jnp.concatenate([p_[j], av[j]], axis=1), DIMS_NN)
           for j in heads]
    low = [jnp.concatenate([zero, vs[j]], axis=1) for j in heads]
    tb = [_dot3(jnp.concatenate([xpw[j], low[j]], axis=0),
                jnp.concatenate([-q_[j], kt[j]], axis=0), DIMS_TN) for j in heads]
    ro = [_dot3(gm[j][n:], jnp.concatenate([-xpw[j], low[j]], axis=0), DIMS_NN) for j in heads]
    s0 = [s_scr[j] for j in heads]
    o = [_dot3(rt[j] + ro[j][:, :hs], s0[j], DIMS_NT) + ro[j][:, hs:] for j in heads]
    for j in heads:
        ecl = ec_all[n - 1:n, sls[j]]
        s_scr[j] = _dot3(s0[j], (eye + tb[j][:hs]) * ecl, DIMS_NN) + tb[j][hs:] * ecl
    outs = []
    for j in heads:
        mean = jnp.mean(o[j], axis=-1, keepdims=True)
        var = jnp.mean(jnp.square(o[j] - mean), axis=-1, keepdims=True)
        on = (o[j] - mean) * lax.rsqrt(var + GN_EPS) * lnw_ref[:, sls[j]] + lnb_ref[:, sls[j]]
        bonus = jnp.sum(bonus_in[:, sls[j]], axis=-1, keepdims=True) * vs[j]
        outs.append(on + bonus)
    og_ref[...] = jnp.concatenate(outs, axis=1) * _silu(g)

    @pl.when(c == pl.num_programs(2) - 1)
    def _():
        so_ref[0] = s_scr[...]


def _rwkv(rkvg, wl, al, k_k, k_a, r_k, ln_w, ln_b, s0, *, nseq, nchunk, row0):
    w = C_HB * HS_C
    tok = lambda n: pl.BlockSpec((1, CHUNK, w), lambda s, h, c, n=n: (n, row0 + s * nchunk + c, h))
    tok2 = pl.BlockSpec((CHUNK, w), lambda s, h, c: (row0 + s * nchunk + c, h))
    vec = pl.BlockSpec((1, w), lambda s, h, c: (0, h))
    st = pl.BlockSpec((1, C_HB, HS_C, HS_C), lambda s, h, c: (s, h, 0, 0))
    has_init = s0 is not None
    ins = [rkvg, rkvg, rkvg, rkvg, wl, al] + [p.reshape(1, D_MODEL) for p in (k_k, k_a, r_k, ln_w, ln_b)]
    specs = [tok(0), tok(1), tok(2), tok(3), tok2, tok2] + [vec] * 5
    if has_init:
        ins.append(s0)
        specs.append(st)
    return pl.pallas_call(
        functools.partial(_rwkv_body, has_init=has_init),
        grid=(nseq, H_C // C_HB, nchunk),
        in_specs=specs,
        out_specs=[pl.BlockSpec((CHUNK, w), lambda s, h, c: (s * nchunk + c, h)), st],
        out_shape=[jax.ShapeDtypeStruct((nseq * nchunk * CHUNK, D_MODEL), F32),
                   jax.ShapeDtypeStruct((nseq, H_C, HS_C, HS_C), F32)],
        scratch_shapes=[pltpu.VMEM((C_HB, HS_C, HS_C), F32)],
        compiler_params=_cparams(("parallel", "parallel", "arbitrary")),
        name="rwkv_scan",
    )(*ins)


def kernel(x_prompt, x_sample, cache_a_k, cache_a_v, cache_b_k, cache_b_v, cache_b_kidx, state_c_wkv, state_c_shift, norm_g, final_g, w_out, a_w_in, a_lam, a_subln_g, b_w_in, c_mu, c_w_rkvg, c_w0, c_w_la, c_w_lb, c_a0, c_a_la, c_a_lb, c_k_k, c_k_a, c_r_k, c_ln_w, c_ln_b):
    x = jnp.concatenate([x_prompt.reshape(N_PROMPT, D_MODEL),
                         x_sample.reshape(N_SAMPLE, D_MODEL)], axis=0)
    cos1, sn1, cos2, sn2 = _rope_tables()
    split = lambda arr: (arr[:N_PROMPT], arr[N_PROMPT:])
    outs = {n: [] for n in ("akp", "avp", "aks", "avs", "bkp", "bvp", "bip", "bks", "bvs", "bis",
                            "cwp", "chp", "cws", "chs")}
    y = None
    for i in range(DEPTH):
        kind, j = i % N_MIXERS, i // N_MIXERS
        if kind == 0:
            lam_init = 0.8 - 0.6 * math.exp(-0.3 * i)
            qkvg = _norm_proj(x, norm_g[i], a_w_in[j].astype(BF16), 512, 1024)
            q_rot, k_rot = _rope_a(qkvg, cos1, sn1)
            og_p = _attn_a_prompt(q_rot, k_rot, qkvg, a_lam[j], a_subln_g[j], lam_init)
            og_s = _attn_a_sample(q_rot, k_rot, qkvg,
                                  cache_a_k.reshape(-1, LANES), cache_a_v.reshape(-1, LANES), j,
                                  a_lam[j], a_subln_g[j], lam_init)
            kp, ks = split(k_rot)
            vp, vs = split(qkvg[:, 2 * D_MODEL:3 * D_MODEL])
            outs["akp"].append(kp.reshape(BATCH, SEQ, H_A, 2 * DH_A))
            outs["avp"].append(vp.reshape(BATCH, SEQ, H_A, 2 * DH_A))
            outs["aks"].append(ks.reshape(DEC_BATCH, DEC_SEQ, H_A, 2 * DH_A))
            outs["avs"].append(vs.reshape(DEC_BATCH, DEC_SEQ, H_A, 2 * DH_A))
        elif kind == 1:
            w = b_w_in[j]
            w = jnp.concatenate([w[:, :4096], w[:, 4176:], w[:, 4096:4176],
                                 jnp.zeros((D_MODEL, B_COLS - w.shape[1]), w.dtype)], axis=1)
            proj = _norm_proj(x, norm_g[i], w.astype(BF16), 512, 896)
            q_rot, k_rot, qi_rot, kid, wi = _rope_b(proj, cos1, sn1, cos2, sn2)
            og_p = _dsa_prompt(q_rot, k_rot, qi_rot, kid, wi, proj)
            kidx_past = cache_b_kidx.reshape(-1, D_IDX)
            og_s = _dsa_sample(q_rot, k_rot, qi_rot, kid, wi, proj,
                               cache_b_k.reshape(-1, KV_B * HD_B), cache_b_v.reshape(-1, KV_B * HD_B),
                               jnp.concatenate([kidx_past, kidx_past], axis=1), j)
            kp, ks = split(k_rot)
            vp, vs = split(proj[:, B_V0:B_V0 + KV_B * HD_B])
            ip, is_ = split(kid[:, :D_IDX])
            outs["bkp"].append(kp.reshape(BATCH, SEQ, KV_B, HD_B))
            outs["bvp"].append(vp.reshape(BATCH, SEQ, KV_B, HD_B))
            outs["bip"].append(ip.reshape(BATCH, SEQ, D_IDX))
            outs["bks"].append(ks.reshape(DEC_BATCH, DEC_SEQ, KV_B, HD_B))
            outs["bvs"].append(vs.reshape(DEC_BATCH, DEC_SEQ, KV_B, HD_B))
            outs["bis"].append(is_.reshape(DEC_BATCH, DEC_SEQ, D_IDX))
        else:
            shift0 = jnp.concatenate([jnp.zeros((BATCH, D_MODEL), F32), state_c_shift[j]], axis=0)
            lerp, hlast = _c_prep(x, norm_g[i], c_mu[j], shift0)
            rkvg = _c_bmm(lerp, c_w_rkvg[j].astype(BF16))
            wl, al = _c_lora(lerp, c_w_la[j], c_w_lb[j], c_a_la[j], c_a_lb[j], c_w0[j], c_a0[j])
            par = (c_k_k[j], c_k_a[j], c_r_k[j], c_ln_w[j], c_ln_b[j])
            og_p, st_p = _rwkv(rkvg, wl, al, *par, None, nseq=BATCH, nchunk=SEQ // CHUNK, row0=0)
            og_s, st_s = _rwkv(rkvg, wl, al, *par, state_c_wkv[j], nseq=DEC_BATCH, nchunk=1,
                               row0=N_PROMPT // CHUNK)
            outs["cwp"].append(st_p)
            outs["chp"].append(hlast[:BATCH, 0])
            outs["cws"].append(st_s)
            outs["chs"].append(hlast[BATCH:, 0])
        if i == DEPTH - 1:
            y = _proj_out(og_p, og_s, x, w_out[i].astype(BF16), final_g)
        else:
            x = _proj_out(og_p, og_s, x, w_out[i].astype(BF16))
    yp, ys = split(y)
    st = lambda n: jnp.stack(outs[n])
    return (yp.reshape(BATCH, SEQ, D_MODEL), ys.reshape(DEC_BATCH, DEC_SEQ, D_MODEL),
            st("akp"), st("avp"), st("aks"), st("avs"),
            st("bkp"), st("bvp"), st("bip"), st("bks"), st("bvs"), st("bis"),
            st("cwp"), st("chp"), st("cws"), st("chs"))
```

```python
import functools
import math

import jax
import jax.numpy as jnp
from jax import lax
from jax.experimental import pallas as pl
from jax.experimental.pallas import tpu as pltpu

F32 = jnp.float32
BF16 = jnp.bfloat16
I32 = jnp.int32

D_MODEL = 2048
BATCH = 4
SEQ = 2048
DEPTH = 4
DEC_BATCH = 8
DEC_SEQ = 64
PAST_LEN = 4096
CHUNK = 64
N_MIXERS = 3
ROPE_THETA = 10000.0
RMS_EPS = 1e-6
DH_A = 64
H_A = 16
SUBLN_EPS = 1e-5
HD_B = 128
H_B = 16
KV_B = 4
H_IDX = 16
D_IDX = 64
TOPK_MAX = 256
HS_C = 64
H_C = 32
R_DECAY = 96
R_ICLR = 96
GN_EPS = 64e-5

N_PROMPT = BATCH * SEQ
N_SAMPLE = DEC_BATCH * DEC_SEQ
M_ROWS = N_PROMPT + N_SAMPLE
LANES = 128
VMEM_LIMIT = 56 * 1024 * 1024
PROJ_TM = M_ROWS // 8
NEG_INF = float("-inf")
INT_MIN = -2 ** 31

B_Q0, B_K0, B_V0, B_QI0, B_G0, B_KI0 = 0, 2048, 2560, 3072, 4096, 6144
B_COLS = 6272

assert PAST_LEN % CHUNK == 0 and DEC_SEQ == CHUNK
LOG2_CHUNK = CHUNK.bit_length() - 1
assert 1 << LOG2_CHUNK == CHUNK


def _chunk_of(pos):
    return jnp.right_shift(pos, LOG2_CHUNK)


def _cparams(sem):
    return pltpu.CompilerParams(dimension_semantics=sem, vmem_limit_bytes=VMEM_LIMIT)


def _sigmoid(x):
    return 1.0 / (1.0 + jnp.exp(-x))


def _silu(x):
    return x * _sigmoid(x)


def _rms(x, g):
    return x * lax.rsqrt(jnp.mean(x * x, axis=-1, keepdims=True) + RMS_EPS) * g


def _dot_nt(a, b, precision=None):
    return lax.dot_general(a, b, (((1,), (1,)), ((), ())), precision=precision,
                           preferred_element_type=F32)


def _norm_proj_body(x_ref, g_ref, w_ref, o_ref, h_scr):
    @pl.when(pl.program_id(1) == 0)
    def _():
        h_scr[...] = _rms(x_ref[...], g_ref[...]).astype(BF16)
    o_ref[...] = jnp.dot(h_scr[...], w_ref[...], preferred_element_type=F32)


def _norm_proj(x, g, w, tm, tn):
    m, d = x.shape
    n = w.shape[1]
    return pl.pallas_call(
        _norm_proj_body,
        grid=(m // tm, n // tn),
        in_specs=[pl.BlockSpec((tm, d), lambda i, j: (i, 0)),
                  pl.BlockSpec((1, d), lambda i, j: (0, 0)),
                  pl.BlockSpec((d, tn), lambda i, j: (0, j))],
        out_specs=pl.BlockSpec((tm, tn), lambda i, j: (i, j)),
        out_shape=jax.ShapeDtypeStruct((m, n), F32),
        scratch_shapes=[pltpu.VMEM((tm, d), BF16)],
        compiler_params=_cparams(("parallel", "arbitrary")),
        name="norm_proj",
    )(x, g.reshape(1, d), w)


def _proj_out_body(ogp_ref, ogs_ref, x_ref, w_ref, o_ref, *, n_prompt_tiles):
    def emit(og_ref):
        o_ref[...] = x_ref[...] + jnp.dot(og_ref[...].astype(BF16), w_ref[...],
                                          preferred_element_type=F32)

    _when_group(n_prompt_tiles, emit, (ogp_ref,), (ogs_ref,))


def _proj_out_final_body(ogp_ref, ogs_ref, x_ref, w_ref, g_ref, yp_ref, ys_ref, *, n_prompt_tiles):
    def emit(og_ref, y_ref):
        xn = x_ref[...] + jnp.dot(og_ref[...].astype(BF16), w_ref[...], preferred_element_type=F32)
        y_ref[...] = _rms(xn, g_ref[...])

    _when_group(n_prompt_tiles, emit, (ogp_ref, yp_ref), (ogs_ref, ys_ref))


def _proj_out(og_prompt, og_sample, x, w, final_g=None, tm=256):
    m, d = x.shape
    row = pl.BlockSpec((tm, d), lambda i: (i, 0))
    full = pl.BlockSpec((d, d), lambda i: (0, 0))
    in_specs = _group_specs(tm, d) + [row, full]
    if final_g is None:
        body, extra, out_specs = _proj_out_body, (), row
        out_shape = jax.ShapeDtypeStruct((m, d), F32)
    else:
        body, extra = _proj_out_final_body, (final_g.reshape(1, d),)
        in_specs.append(pl.BlockSpec((1, d), lambda i: (0, 0)))
        out_specs, out_shape = _group_specs(tm, d), _group_shapes(d)
    return pl.pallas_call(
        functools.partial(body, n_prompt_tiles=N_PROMPT // tm),
        grid=(m // tm,),
        in_specs=in_specs,
        out_specs=out_specs,
        out_shape=out_shape,
        compiler_params=_cparams(("arbitrary",)),
        name="proj_out",
    )(og_prompt, og_sample, x, w, *extra)


def _rope_tables():
    pos = jnp.concatenate([jnp.tile(jnp.arange(SEQ), BATCH),
                           jnp.tile(PAST_LEN + jnp.arange(DEC_SEQ), DEC_BATCH)]).astype(F32)

    def table(dh, reps):
        half = dh // 2
        inv = jnp.power(ROPE_THETA, -jnp.arange(half, dtype=F32) * (2.0 / dh))
        ang = pos[:, None] * inv[None, :]
        cos, sin = jnp.cos(ang), jnp.sin(ang)
        return (jnp.tile(jnp.concatenate([cos, cos], axis=1), (1, reps)),
                jnp.tile(jnp.concatenate([-sin, sin], axis=1), (1, reps)))

    cos1, sn1 = table(64, 2)
    cos2, sn2 = table(128, 1)
    return cos1, sn1, cos2, sn2


def _rope64(x, cos, sn):
    lane = lax.broadcasted_iota(I32, x.shape, 1)
    partner = jnp.where((lane & 63) < 32, pltpu.roll(x, 96, 1), pltpu.roll(x, 32, 1))
    return x * cos + partner * sn


def _rope128(x, cos, sn):
    return x * cos + pltpu.roll(x, 64, 1) * sn


def _group_specs(tm, width):
    npt = N_PROMPT // tm
    return [pl.BlockSpec((tm, width), lambda i: (jnp.minimum(i, npt - 1), 0)),
            pl.BlockSpec((tm, width), lambda i: (jnp.maximum(i - npt, 0), 0))]


def _group_shapes(width):
    return [jax.ShapeDtypeStruct((N_PROMPT, width), F32), jax.ShapeDtypeStruct((N_SAMPLE, width), F32)]


def _when_group(n_prompt_tiles, emit, prompt_refs, sample_refs):
    is_prompt = pl.program_id(0) < n_prompt_tiles
    pl.when(is_prompt)(lambda: emit(*prompt_refs))
    pl.when(jnp.logical_not(is_prompt))(lambda: emit(*sample_refs))


def _rope_a_body(q_ref, k_ref, v_ref, cos_ref, sn_ref, qo_ref, kp_ref, ks_ref, vp_ref, vs_ref,
                 *, n_prompt_tiles):
    cos, sn = cos_ref[...], sn_ref[...]
    hsl = [slice(h * LANES, (h + 1) * LANES) for h in range(H_A)]
    for sl in hsl:
        qo_ref[:, sl] = _rope64(q_ref[:, sl], cos, sn) * (DH_A ** -0.5)

    def emit(ko_ref, vo_ref):
        for sl in hsl:
            ko_ref[:, sl] = _rope64(k_ref[:, sl], cos, sn)
        vo_ref[...] = v_ref[...]

    _when_group(n_prompt_tiles, emit, (kp_ref, vp_ref), (ks_ref, vs_ref))


def _rope_a(qkvg, cos1, sn1, tm=256):
    m = qkvg.shape[0]
    blk = lambda c: pl.BlockSpec((tm, D_MODEL), lambda i, c=c: (i, c))
    tab = pl.BlockSpec((tm, LANES), lambda i: (i, 0))
    return pl.pallas_call(
        functools.partial(_rope_a_body, n_prompt_tiles=N_PROMPT // tm),
        grid=(m // tm,),
        in_specs=[blk(0), blk(1), blk(2), tab, tab],
        out_specs=[blk(0)] + _group_specs(tm, D_MODEL) * 2,
        out_shape=[jax.ShapeDtypeStruct((m, D_MODEL), F32)] + _group_shapes(D_MODEL) * 2,
        compiler_params=_cparams(("arbitrary",)),
        name="rope_a",
    )(qkvg, qkvg, qkvg, cos1, sn1)


def _stack_maps_t(q):
    lane = lax.broadcasted_iota(I32, q.shape, 1)
    qs = jnp.concatenate([jnp.where(lane < DH_A, q, 0.0), jnp.where(lane >= DH_A, q, 0.0)], axis=0)
    return qs.T.astype(BF16)


def _flash_t_step(s, vt, carry):
    m, l, acc = carry
    m_new = jnp.maximum(m, jnp.max(s, axis=0, keepdims=True))
    alpha = jnp.exp(m - m_new)
    p = jnp.exp(s - m_new)
    l = alpha * l + jnp.sum(p, axis=0, keepdims=True)
    pv = lax.dot_general(vt, p.astype(BF16), DIMS_TN, preferred_element_type=F32)
    return m_new, l, alpha * acc + pv


def _flash_t_init(n):
    return (jnp.full((1, n), -1e30, F32), jnp.zeros((1, n), F32), jnp.zeros((LANES, n), F32))


def _diff_epilogue(l, acc, tq, lam_ref, sg, g, lam_init):
    o = (acc / l).T
    lp = lam_ref[...]
    lam = (jnp.exp(jnp.sum(lp[0:1] * lp[1:2], axis=-1, keepdims=True))
           - jnp.exp(jnp.sum(lp[2:3] * lp[3:4], axis=-1, keepdims=True)) + lam_init)
    o = o[:tq] - lam * o[tq:]
    o = o * lax.rsqrt(jnp.mean(o * o, axis=-1, keepdims=True) + SUBLN_EPS) * sg
    o = o * (1.0 - lam_init)
    return o * _silu(g)


A_HP = 4


def _attn_a_prompt_body(q_ref, k_ref, v_ref, g_ref, lam_ref, sg_ref, o_ref, *, tq, lam_init):
    qi = pl.program_id(2)
    hsl = [slice(h * LANES, (h + 1) * LANES) for h in range(A_HP)]
    qst = [_stack_maps_t(q_ref[:, sl]) for sl in hsl]
    n = 2 * tq
    krow = lax.broadcasted_iota(I32, (tq, n), 0)
    qcol = lax.broadcasted_iota(I32, (tq, n), 1) & (tq - 1)
    diag_mask = _chunk_of(krow) <= _chunk_of(qcol)

    def tile(kt, carries, masked):
        off = pl.multiple_of(kt * tq, tq)
        ss = [jnp.dot(k_ref[pl.ds(off, tq), sl].astype(BF16), qst[h], preferred_element_type=F32)
              for h, sl in enumerate(hsl)]
        out = []
        for h, sl in enumerate(hsl):
            s = jnp.where(diag_mask, ss[h], NEG_INF) if masked else ss[h]
            out.append(_flash_t_step(s, v_ref[pl.ds(off, tq), sl].astype(BF16), carries[h]))
        return tuple(out)

    carries = lax.fori_loop(0, qi, lambda kt, c: tile(kt, c, False),
                            tuple(_flash_t_init(n) for _ in hsl))
    carries = tile(qi, carries, True)
    for h, sl in enumerate(hsl):
        _, l, acc = carries[h]
        o_ref[:, sl] = _diff_epilogue(l, acc, tq, lam_ref, sg_ref[...], g_ref[:, sl], lam_init)


def _attn_a_prompt(q_rot, k_p, v_p, qkvg, lam_p, subln_g, lam_init, tq=256):
    nq = SEQ // tq
    w = A_HP * LANES
    nh = H_A // A_HP
    return pl.pallas_call(
        functools.partial(_attn_a_prompt_body, tq=tq, lam_init=lam_init),
        grid=(BATCH, nh, nq),
        in_specs=[pl.BlockSpec((tq, w), lambda b, h, i: (b * nq + i, h)),
                  pl.BlockSpec((SEQ, w), lambda b, h, i: (b, h)),
                  pl.BlockSpec((SEQ, w), lambda b, h, i: (b, h)),
                  pl.BlockSpec((tq, w), lambda b, h, i: (b * nq + i, 3 * nh + h)),
                  pl.BlockSpec((4, DH_A), lambda b, h, i: (0, 0)),
                  pl.BlockSpec((1, LANES), lambda b, h, i: (0, 0))],
        out_specs=pl.BlockSpec((tq, w), lambda b, h, i: (b * nq + i, h)),
        out_shape=jax.ShapeDtypeStruct((N_PROMPT, D_MODEL), F32),
        compiler_params=_cparams(("parallel", "parallel", "arbitrary")),
        name="attn_a_prompt",
    )(q_rot, k_p, v_p, qkvg, lam_p, subln_g.reshape(1, LANES))


def _attn_a_sample_body(q_ref, kp_ref, vp_ref, k_ref, v_ref, g_ref, lam_ref, sg_ref,
                        o_ref, qst_scr, ml_scr, acc_scr, *, tk, lam_init):
    kt = pl.program_id(1)
    tq = DEC_SEQ
    hsl = [slice(h * LANES, (h + 1) * LANES) for h in range(H_A)]

    @pl.when(kt == 0)
    def _():
        for h, sl in enumerate(hsl):
            qst_scr[h] = _stack_maps_t(q_ref[:, sl])
        m0, l0, acc0 = _flash_t_init(2 * tq)
        for h in range(H_A):
            ml_scr[h, 0:1, :] = m0
            ml_scr[h, 1:2, :] = l0
            acc_scr[h] = acc0

    def load(h):
        return ml_scr[h, 0:1, :], ml_scr[h, 1:2, :], acc_scr[h]

    def store(h, carry):
        ml_scr[h, 0:1, :], ml_scr[h, 1:2, :], acc_scr[h] = carry

    def head_group(i, _):
        hs = [A_HP * i + d for d in range(A_HP)]
        ss = [jnp.dot(kp_ref[pl.ds(h, tk, stride=H_A), :].astype(BF16), qst_scr[h],
                      preferred_element_type=F32) for h in hs]
        for h, s in zip(hs, ss):
            vb = vp_ref[pl.ds(h, tk, stride=H_A), :].astype(BF16)
            store(h, _flash_t_step(s, vb, load(h)))
        return 0

    lax.fori_loop(0, H_A // A_HP, head_group, 0)

    @pl.when(kt == pl.num_programs(1) - 1)
    def _():
        for h, sl in enumerate(hsl):
            s = jnp.dot(k_ref[:, sl].astype(BF16), qst_scr[h], preferred_element_type=F32)
            _, l, acc = _flash_t_step(s, v_ref[:, sl].astype(BF16), load(h))
            o_ref[:, sl] = _diff_epilogue(l, acc, tq, lam_ref, sg_ref[...], g_ref[:, sl], lam_init)


def _attn_a_sample(q_rot, k_s, v_s, qkvg, k_past, v_past, layer, lam_p, subln_g, lam_init, tk=512):
    r0 = N_PROMPT // DEC_SEQ
    nkt = PAST_LEN // tk
    new = lambda c: pl.BlockSpec((DEC_SEQ, D_MODEL), lambda b, t, c=c: (r0 + b, c))
    own = pl.BlockSpec((DEC_SEQ, D_MODEL), lambda b, t: (b, 0))
    past = pl.BlockSpec((tk * H_A, LANES), lambda b, t: ((layer * DEC_BATCH + b) * nkt + t, 0))
    return pl.pallas_call(
        functools.partial(_attn_a_sample_body, tk=tk, lam_init=lam_init),
        grid=(DEC_BATCH, nkt),
        in_specs=[new(0), past, past, own, own, new(3),
                  pl.BlockSpec((4, DH_A), lambda b, t: (0, 0)),
                  pl.BlockSpec((1, LANES), lambda b, t: (0, 0))],
        out_specs=own,
        out_shape=jax.ShapeDtypeStruct((N_SAMPLE, D_MODEL), F32),
        scratch_shapes=[pltpu.VMEM((H_A, LANES, 2 * DEC_SEQ), BF16),
                        pltpu.VMEM((H_A, 8, 2 * DEC_SEQ), F32),
                        pltpu.VMEM((H_A, LANES, 2 * DEC_SEQ), F32)],
        compiler_params=_cparams(("parallel", "arbitrary")),
        name="attn_a_sample",
    )(q_rot, k_past, v_past, k_s, v_s, qkvg, lam_p, subln_g.reshape(1, LANES))


def _rope_b_body(q_ref, k_ref, v_ref, qi_ref, sm_ref, c1_ref, s1_ref, c2_ref, s2_ref,
                 qo_ref, qio_ref, kid_ref, wi_ref, kp_ref, ks_ref, vp_ref, vs_ref, ip_ref, is_ref,
                 *, n_prompt_tiles):
    c1, s1, c2, s2 = c1_ref[...], s1_ref[...], c2_ref[...], s2_ref[...]
    for h in range(H_B):
        sl = slice(h * LANES, (h + 1) * LANES)
        qo_ref[:, sl] = _rope128(q_ref[:, sl], c2, s2) * (HD_B ** -0.5)
    for h in range(H_IDX * D_IDX // LANES):
        sl = slice(h * LANES, (h + 1) * LANES)
        qio_ref[:, sl] = _rope64(qi_ref[:, sl], c1, s1) * (D_IDX ** -0.5)
    sm = sm_ref[...]
    lane = lax.broadcasted_iota(I32, sm.shape, 1)
    kr = _rope64(sm, c1, s1)
    kid_ref[...] = jnp.where(lane < D_IDX, kr, pltpu.roll(kr, D_IDX, 1))
    wi_ref[...] = pltpu.roll(sm, D_IDX, 1) * (H_IDX ** -0.5)

    def emit(ko_ref, vo_ref, io_ref):
        for h in range(KV_B):
            sl = slice(h * LANES, (h + 1) * LANES)
            ko_ref[:, sl] = _rope128(k_ref[:, sl], c2, s2)
        vo_ref[...] = v_ref[...]
        io_ref[...] = kr[:, :D_IDX]

    _when_group(n_prompt_tiles, emit, (kp_ref, vp_ref, ip_ref), (ks_ref, vs_ref, is_ref))


def _rope_b(proj, cos1, sn1, cos2, sn2, tm=256):
    m = proj.shape[0]
    tab = pl.BlockSpec((tm, LANES), lambda i: (i, 0))
    blk = lambda w, c: pl.BlockSpec((tm, w), lambda i, c=c: (i, c))
    widths = (H_B * HD_B, H_IDX * D_IDX, LANES, LANES)
    kvw = KV_B * HD_B
    return pl.pallas_call(
        functools.partial(_rope_b_body, n_prompt_tiles=N_PROMPT // tm),
        grid=(m // tm,),
        in_specs=[blk(2048, 0), blk(kvw, B_K0 // kvw), blk(kvw, B_V0 // kvw),
                  blk(1024, B_QI0 // 1024), blk(LANES, B_KI0 // LANES), tab, tab, tab, tab],
        out_specs=([blk(w, 0) for w in widths] + _group_specs(tm, kvw) * 2
                   + _group_specs(tm, D_IDX)),
        out_shape=([jax.ShapeDtypeStruct((m, w), F32) for w in widths] + _group_shapes(kvw) * 2
                   + _group_shapes(D_IDX)),
        compiler_params=_cparams(("arbitrary",)),
        name="rope_b",
    )(proj, proj, proj, proj, proj, cos1, sn1, cos2, sn2)


POS_BITS = 13
DSA_BUCKET = 512


def _count(pred):
    return jnp.sum(jnp.where(pred, 1.0, 0.0), axis=-1, keepdims=True)


def _select_topk(scores, poss, n_sel):
    keys = []
    for s in scores:
        bits = lax.bitcast_convert_type(s + 0.0, I32)
        keys.append(jnp.where(bits < 0, bits ^ 0x7FFFFFFF, bits))
    tq = scores[0].shape[0]
    n_sel = float(n_sel)

    def thr_step(i, t):
        cand_bits = t | lax.shift_left(jnp.int32(1), 31 - i)
        cand = cand_bits ^ INT_MIN
        cnt = sum(_count(k >= cand) for k in keys)
        return jnp.where(cnt >= n_sel, cand_bits, t)

    t = lax.fori_loop(0, 32, thr_step, jnp.zeros((tq, 1), I32))
    thr = t ^ INT_MIN
    need = n_sel - sum(_count(k > thr) for k in keys)
    n_tied = sum(_count(k == thr) for k in keys)

    def tie_step(i, j):
        cand = j | lax.shift_left(jnp.int32(1), POS_BITS - 1 - i)
        cnt = sum(_count((k == thr) & (p < cand)) for k, p in zip(keys, poss))
        return jnp.where(cnt < need, cand, j)

    j = lax.cond(jnp.max(n_tied - need) > 0.0,
                 lambda: lax.fori_loop(0, POS_BITS, tie_step, jnp.zeros((tq, 1), I32)),
                 lambda: jnp.full((tq, 1), (1 << POS_BITS) - 1, I32))
    return [(k > thr) | ((k == thr) & (p <= j)) for k, p in zip(keys, poss)]


def _dsa_body(*refs, tq, seg_lens, seg_pos0, q_pos0, causal, n_sel, bucket):
    nseg = len(seg_lens)
    n_in = 4 + 3 * nseg
    assert max(seg_pos0[i] + seg_lens[i] for i in range(nseg)) <= 1 << POS_BITS
    if bucket is None:
        _dsa_run(refs, tq, seg_lens, seg_pos0, q_pos0, causal, n_sel, n_in)
        return
    assert causal and nseg == 1 and seg_pos0[0] == 0 and q_pos0 == 0 and bucket % tq == 0
    last = (pl.program_id(1) * tq) // bucket
    for v in range(seg_lens[0] // bucket):
        pl.when(last == v)(functools.partial(_dsa_run, refs, tq, ((v + 1) * bucket,), seg_pos0,
                                             q_pos0, causal, n_sel, n_in))


def _dsa_run(refs, tq, seg_lens, seg_pos0, q_pos0, causal, n_sel, n_in):
    nseg = len(seg_lens)
    qi_ref, wi_ref, q_ref, g_ref = refs[:4]
    kid_refs = refs[4:4 + nseg]
    k_refs = refs[4 + nseg:4 + 2 * nseg]
    v_refs = refs[4 + 2 * nseg:4 + 3 * nseg]
    o_ref = refs[n_in]
    bias_refs = refs[n_in + 1:]
    n_rep = H_B // KV_B

    @pl.when(pl.program_id(2) == 0)
    def _():
        wi = wi_ref[...]
        qpos = q_pos0 + pl.program_id(1) * tq
        scores, poss, adms = [], [], []
        for si in range(nseg):
            sl = seg_lens[si]
            kid = kid_refs[si][0:sl, :].astype(BF16)
            sc = jnp.zeros((tq, sl), F32)
            for p in range(H_IDX // 2):
                qp = qi_ref[:, p * LANES:(p + 1) * LANES]
                lane = lax.broadcasted_iota(I32, qp.shape, 1)
                qs = jnp.concatenate([jnp.where(lane < D_IDX, qp, 0.0),
                                      jnp.where(lane >= D_IDX, qp, 0.0)], axis=0).astype(BF16)
                lg = jnp.maximum(_dot_nt(qs, kid), 0.0)
                sc = sc + wi[:, 2 * p:2 * p + 1] * lg[:tq] + wi[:, 2 * p + 1:2 * p + 2] * lg[tq:]
            kpos = seg_pos0[si] + lax.broadcasted_iota(I32, (tq, sl), 1)
            if causal:
                qrow = qpos + lax.broadcasted_iota(I32, (tq, sl), 0)
                adm = _chunk_of(kpos) <= _chunk_of(qrow)
                sc = jnp.where(adm, sc, NEG_INF)
            else:
                adm = None
            scores.append(sc)
            poss.append(kpos)
            adms.append(adm)
        sels = _select_topk(scores, poss, n_sel)
        for si in range(nseg):
            bias = jnp.where(sels[si], 0.0, NEG_INF)
            if adms[si] is not None:
                bias = jnp.where(adms[si], bias, NEG_INF)
            bias_refs[si][:, 0:seg_lens[si]] = bias

    q = q_ref[...]
    qg = jnp.concatenate([q[:, r * LANES:(r + 1) * LANES] for r in range(n_rep)],
                         axis=0).astype(BF16)
    ss = []
    for si in range(nseg):
        b = bias_refs[si][:, 0:seg_lens[si]]
        s = _dot_nt(qg, k_refs[si][0:seg_lens[si], :].astype(BF16))
        ss.append(s + jnp.concatenate([b] * n_rep, axis=0))
    m = functools.reduce(jnp.maximum, [jnp.max(s, axis=-1, keepdims=True) for s in ss])
    l = 0.0
    acc = 0.0
    for si in range(nseg):
        p = jnp.exp(ss[si] - m)
        l = l + jnp.sum(p, axis=-1, keepdims=True)
        acc = acc + jnp.dot(p.astype(BF16), v_refs[si][0:seg_lens[si], :].astype(BF16),
                            preferred_element_type=F32)
    o = acc / l
    o = jnp.concatenate([o[r * tq:(r + 1) * tq] for r in range(n_rep)], axis=1)
    o_ref[...] = o * _silu(g_ref[...])


def _dsa_prompt(q_rot, k_p, v_p, qi_rot, kid, wi, proj, tq=128):
    nq = SEQ // tq
    gw = KV_B * HD_B
    row = lambda w, c0: pl.BlockSpec((tq, w), lambda b, i, g, c0=c0: (b * nq + i, c0 + g))
    row0 = lambda w: pl.BlockSpec((tq, w), lambda b, i, g: (b * nq + i, 0))
    seq = lambda c0: pl.BlockSpec((SEQ, LANES), lambda b, i, g, c0=c0: (b, c0 + g))
    n_sel = min(TOPK_MAX, SEQ // 4)
    return pl.pallas_call(
        functools.partial(_dsa_body, tq=tq, seg_lens=(SEQ,), seg_pos0=(0,), q_pos0=0,
                          causal=True, n_sel=n_sel, bucket=DSA_BUCKET),
        grid=(BATCH, nq, KV_B),
        in_specs=[row0(H_IDX * D_IDX), row0(LANES), row(gw, 0), row(gw, B_G0 // gw),
                  pl.BlockSpec((SEQ, LANES), lambda b, i, g: (b, 0)),
                  seq(0), seq(0)],
        out_specs=row(gw, 0),
        out_shape=jax.ShapeDtypeStruct((N_PROMPT, D_MODEL), F32),
        scratch_shapes=[pltpu.VMEM((tq, SEQ), F32)],
        compiler_params=_cparams(("parallel", "parallel", "arbitrary")),
        name="dsa_prompt",
    )(qi_rot, wi, q_rot, proj, kid, k_p, v_p)


def _dsa_sample(q_rot, k_s, v_s, qi_rot, kid, wi, proj, k_past, v_past, kid_past, layer):
    tq = DEC_SEQ
    r0 = N_PROMPT // tq
    gw = KV_B * HD_B
    row = lambda w, c0: pl.BlockSpec((tq, w), lambda b, i, g, c0=c0: (r0 + b, c0 + g))
    row0 = lambda w: pl.BlockSpec((tq, w), lambda b, i, g: (r0 + b, 0))
    past = pl.BlockSpec((PAST_LEN, LANES), lambda b, i, g: (layer * DEC_BATCH + b, g))
    own = pl.BlockSpec((tq, LANES), lambda b, i, g: (b, g))
    s_all = PAST_LEN + DEC_SEQ
    n_sel = min(TOPK_MAX, s_all // 4)
    return pl.pallas_call(
        functools.partial(_dsa_body, tq=tq, seg_lens=(PAST_LEN, DEC_SEQ),
                          seg_pos0=(0, PAST_LEN), q_pos0=PAST_LEN, causal=False, n_sel=n_sel,
                          bucket=None),
        grid=(DEC_BATCH, 1, KV_B),
        in_specs=[row0(H_IDX * D_IDX), row0(LANES), row(gw, 0), row(gw, B_G0 // gw),
                  pl.BlockSpec((PAST_LEN, LANES), lambda b, i, g: (layer * DEC_BATCH + b, 0)),
                  row0(LANES),
                  past, own,
                  past, own],
        out_specs=pl.BlockSpec((tq, gw), lambda b, i, g: (b, g)),
        out_shape=jax.ShapeDtypeStruct((N_SAMPLE, D_MODEL), F32),
        scratch_shapes=[pltpu.VMEM((tq, PAST_LEN), F32), pltpu.VMEM((tq, DEC_SEQ), F32)],
        compiler_params=_cparams(("parallel", "arbitrary", "arbitrary")),
        name="dsa_sample",
    )(qi_rot, wi, q_rot, proj, kid_past, kid, k_past, k_s, v_past, v_s)


C_TM = 64
C_TILES_PER_SEQ = SEQ // C_TM
C_PROMPT_TILES = N_PROMPT // C_TM


def _c_seq_id(i):
    return jnp.where(i < C_PROMPT_TILES, i // C_TILES_PER_SEQ, i - C_PROMPT_TILES + BATCH)


def _c_prep_body(x_ref, g_ref, mu_ref, sh_ref, l_ref, hl_ref, carry):
    i = pl.program_id(0)

    @pl.when(i == 0)
    def _():
        carry[...] = jnp.zeros_like(carry)

    h = _rms(x_ref[...], g_ref[...])
    start = jnp.logical_or(i >= C_PROMPT_TILES, i % C_TILES_PER_SEQ == 0)
    first = jnp.where(start, sh_ref[0], carry[...])
    row = lax.broadcasted_iota(I32, h.shape, 0)
    prev = jnp.where(row == 0, first, pltpu.roll(h, 1, 0))
    last = h[C_TM - 1:C_TM, :]
    carry[...] = last
    hl_ref[0] = last
    d = prev - h
    for n in range(6):
        l_ref[n] = (h + d * mu_ref[n:n + 1, :]).astype(BF16)


def _c_prep(x, g, mu, shift0):
    m, d = x.shape
    nseq = shift0.shape[0]
    return pl.pallas_call(
        _c_prep_body,
        grid=(m // C_TM,),
        in_specs=[pl.BlockSpec((C_TM, d), lambda i: (i, 0)),
                  pl.BlockSpec((1, d), lambda i: (0, 0)),
                  pl.BlockSpec((6, d), lambda i: (0, 0)),
                  pl.BlockSpec((1, 1, d), lambda i: (_c_seq_id(i), 0, 0))],
        out_specs=[pl.BlockSpec((6, C_TM, d), lambda i: (0, i, 0)),
                   pl.BlockSpec((1, 1, d), lambda i: (_c_seq_id(i), 0, 0))],
        out_shape=[jax.ShapeDtypeStruct((6, m, d), BF16),
                   jax.ShapeDtypeStruct((nseq, 1, d), F32)],
        scratch_shapes=[pltpu.VMEM((1, d), F32)],
        compiler_params=_cparams(("arbitrary",)),
        name="c_prep",
    )(x, g.reshape(1, d), mu, shift0.reshape(nseq, 1, d))


def _bmm_body(l_ref, w_ref, o_ref):
    o_ref[0] = jnp.dot(l_ref[0], w_ref[0], preferred_element_type=F32)


def _c_bmm(lerp, w, tm=PROJ_TM, tn=1024):
    _, m, d = lerp.shape
    nb, _, n = w.shape
    return pl.pallas_call(
        _bmm_body,
        grid=(nb, m // tm, n // tn),
        in_specs=[pl.BlockSpec((1, tm, d), lambda b, i, j: (b, i, 0)),
                  pl.BlockSpec((1, d, tn), lambda b, i, j: (b, 0, j))],
        out_specs=pl.BlockSpec((1, tm, tn), lambda b, i, j: (b, i, j)),
        out_shape=jax.ShapeDtypeStruct((nb, m, n), F32),
        compiler_params=_cparams(("parallel", "parallel", "arbitrary")),
        name="c_bmm",
    )(lerp, w)


def _c_lora_body(l4_ref, l5_ref, wla_ref, wlb_ref, ala_ref, alb_ref, w0_ref, a0_ref,
                 wl_ref, al_ref):
    tw = jnp.tanh(jnp.dot(l4_ref[0], wla_ref[...], preferred_element_type=F32))
    wl_ref[...] = w0_ref[...] + jnp.dot(tw.astype(BF16), wlb_ref[...],
                                        preferred_element_type=F32)
    ta = jnp.dot(l5_ref[0], ala_ref[...], preferred_element_type=F32)
    al_ref[...] = a0_ref[...] + jnp.dot(ta.astype(BF16), alb_ref[...],
                                        preferred_element_type=F32)


def _c_lora(lerp, w_la, w_lb, a_la, a_lb, w0, a0, tm=512):
    _, m, d = lerp.shape
    pad_in = lambda w: jnp.pad(w, ((0, 0), (0, LANES - w.shape[1]))).astype(BF16)
    pad_out = lambda w: jnp.pad(w, ((0, LANES - w.shape[0]), (0, 0))).astype(BF16)
    lin = lambda n: pl.BlockSpec((1, tm, d), lambda i, n=n: (n, i, 0))
    win = pl.BlockSpec((d, LANES), lambda i: (0, 0))
    wout = pl.BlockSpec((LANES, d), lambda i: (0, 0))
    vec = pl.BlockSpec((1, d), lambda i: (0, 0))
    out = pl.BlockSpec((tm, d), lambda i: (i, 0))
    return pl.pallas_call(
        _c_lora_body,
        grid=(m // tm,),
        in_specs=[lin(4), lin(5), win, wout, win, wout, vec, vec],
        out_specs=[out, out],
        out_shape=[jax.ShapeDtypeStruct((m, d), F32)] * 2,
        compiler_params=_cparams(("parallel",)),
        name="c_lora",
    )(lerp, lerp, pad_in(w_la), pad_out(w_lb), pad_in(a_la), pad_out(a_lb),
      w0.reshape(1, d), a0.reshape(1, d))


C_HB = 16
DIMS_NN = (((1,), (0,)), ((), ()))
DIMS_NT = (((1,), (1,)), ((), ()))
DIMS_TN = (((0,), (0,)), ((), ()))


def _dot3(a, b, dims):
    (ca,), (cb,) = dims[0]
    ah = a.astype(BF16).astype(F32)
    bh = b.astype(BF16).astype(F32)
    sa = jnp.concatenate([ah, a - ah, ah], axis=ca).astype(BF16)
    sb = jnp.concatenate([bh, bh, b - bh], axis=cb).astype(BF16)
    return lax.dot_general(sa, sb, dims, preferred_element_type=F32)


def _cumsum_rows(x):
    row = lax.broadcasted_iota(I32, x.shape, 0)
    sh = 1
    while sh < x.shape[0]:
        x = x + jnp.where(row >= sh, pltpu.roll(x, sh, 0), 0.0)
        sh *= 2
    return x


def _rwkv_body(*refs, has_init):
    if has_init:
        (r_ref, k_ref, v_ref, g_ref, wl_ref, al_ref, kk_ref, ka_ref, rk_ref, lnw_ref, lnb_ref,
         s0_ref, og_ref, so_ref, s_scr) = refs
    else:
        (r_ref, k_ref, v_ref, g_ref, wl_ref, al_ref, kk_ref, ka_ref, rk_ref, lnw_ref, lnb_ref,
         og_ref, so_ref, s_scr) = refs
    c = pl.program_id(2)
    n = CHUNK
    hs = HS_C

    @pl.when(c == 0)
    def _():
        if has_init:
            s_scr[...] = s0_ref[0]
        else:
            s_scr[...] = jnp.zeros_like(s_scr)

    r, k, v, g = r_ref[0], k_ref[0], v_ref[0], g_ref[0]
    z = -wl_ref[...]
    softplus = jnp.maximum(z, 0.0) + jnp.log(1.0 + jnp.exp(-jnp.abs(z)))
    logw = -jnp.exp(-softplus - 0.5)
    a = _sigmoid(al_ref[...])
    kkr = k * kk_ref[...]
    k2 = k * (1.0 + (a - 1.0) * ka_ref[...])
    bonus_in = r * k2 * rk_ref[...]

    cs_all = _cumsum_rows(logw)
    ec_all, eci_all, ecp_all = jnp.exp(cs_all), jnp.exp(-cs_all), jnp.exp(cs_all - logw)

    ti = lax.broadcasted_iota(I32, (n, n), 0)
    tj = lax.broadcasted_iota(I32, (n, n), 1)
    eye = jnp.where(tj == ti, 1.0, 0.0)
    gi = lax.broadcasted_iota(I32, (2 * n, 2 * n), 0)
    gj = lax.broadcasted_iota(I32, (2 * n, 2 * n), 1) & (n - 1)
    gmask = ((gi < n) & (gj < gi)) | ((gi >= n) & (gj <= gi - n))
    lvl_masks = []
    bs = 1
    while bs < n:
        sh = bs.bit_length()
        lvl_masks.append((jnp.right_shift(ti, sh) == jnp.right_shift(tj, sh))
                         & ((ti & bs) != 0) & ((tj & bs) == 0))
        bs *= 2

    heads = range(C_HB)
    sls = [slice(j * hs, (j + 1) * hs) for j in heads]
    vs = [v[:, sl] for sl in sls]
    zero = jnp.zeros_like(vs[0])
    p_, q_, kt, rt = [], [], [], []
    for sl in sls:
        kkj = kkr[:, sl]
        kkj = kkj / jnp.maximum(jnp.sqrt(jnp.sum(kkj * kkj, axis=-1, keepdims=True)), 1e-12)
        p_.append(kkj * ecp_all[:, sl])
        q_.append(kkj * a[:, sl] * eci_all[:, sl])
        kt.append(k2[:, sl] * eci_all[:, sl])
        rt.append(r[:, sl] * ec_all[:, sl])
    gm = [jnp.where(gmask, _dot3(jnp.concatenate([p_[j], rt[j]], axis=0),
                                 jnp.concatenate([q_[j], kt[j]], axis=0), DIMS_NT), 0.0)
          for j in heads]
    av = [_dot3(gm[j][:n], jnp.concatenate([zero, vs[j]], axis=0), DIMS_NN)
          for j in heads]
    a_qp = [gm[j][:n, :n] for j in heads]
    x = [eye - jnp.where(lvl_masks[0], a_qp[j], 0.0) for j in heads]
    for msk in lvl_masks[1:]:
        ax = [_dot3(jnp.where(msk, a_qp[j], 0.0), x[j], DIMS_NN) for j in heads]
        x = [x[j] - _dot3(x[j], ax[j], DIMS_NN) for j in heads]
    xpw = [_dot3(x[j], jnp.concatenate([p_[j], av[j]], axis=1), DIMS_NN)
           for j in heads]
    low = [jnp.concatenate([zero, vs[j]], axis=1) for j in heads]
    tb = [_dot3(jnp.concatenate([xpw[j], low[j]], axis=0),
                jnp.concatenate([-q_[j], kt[j]], axis=0), DIMS_TN) for j in heads]
    ro = [_dot3(gm[j][n:], jnp.concatenate([-xpw[j], low[j]], axis=0), DIMS_NN) for j in heads]
    s0 = [s_scr[j] for j in heads]
    o = [_dot3(rt[j] + ro[j][:, :hs], s0[j], DIMS_NT) + ro[j][:, hs:] for j in heads]
    for j in heads:
        ecl = ec_all[n - 1:n, sls[j]]
        s_scr[j] = _dot3(s0[j], (eye + tb[j][:hs]) * ecl, DIMS_NN) + tb[j][hs:] * ecl
    outs = []
    for j in heads:
        mean = jnp.mean(o[j], axis=-1, keepdims=True)
        var = jnp.mean(jnp.square(o[j] - mean), axis=-1, keepdims=True)
        on = (o[j] - mean) * lax.rsqrt(var + GN_EPS) * lnw_ref[:, sls[j]] + lnb_ref[:, sls[j]]
        bonus = jnp.sum(bonus_in[:, sls[j]], axis=-1, keepdims=True) * vs[j]
        outs.append(on + bonus)
    og_ref[...] = jnp.concatenate(outs, axis=1) * _silu(g)

    @pl.when(c == pl.num_programs(2) - 1)
    def _():
        so_ref[0] = s_scr[...]


def _rwkv(rkvg, wl, al, k_k, k_a, r_k, ln_w, ln_b, s0, *, nseq, nchunk, row0):
    w = C_HB * HS_C
    tok = lambda n: pl.BlockSpec((1, CHUNK, w), lambda s, h, c, n=n: (n, row0 + s * nchunk + c, h))
    tok2 = pl.BlockSpec((CHUNK, w), lambda s, h, c: (row0 + s * nchunk + c, h))
    vec = pl.BlockSpec((1, w), lambda s, h, c: (0, h))
    st = pl.BlockSpec((1, C_HB, HS_C, HS_C), lambda s, h, c: (s, h, 0, 0))
    has_init = s0 is not None
    ins = [rkvg, rkvg, rkvg, rkvg, wl, al] + [p.reshape(1, D_MODEL) for p in (k_k, k_a, r_k, ln_w, ln_b)]
    specs = [tok(0), tok(1), tok(2), tok(3), tok2, tok2] + [vec] * 5
    if has_init:
        ins.append(s0)
        specs.append(st)
    return pl.pallas_call(
        functools.partial(_rwkv_body, has_init=has_init),
        grid=(nseq, H_C // C_HB, nchunk),
        in_specs=specs,
        out_specs=[pl.BlockSpec((CHUNK, w), lambda s, h, c: (s * nchunk + c, h)), st],
        out_shape=[jax.ShapeDtypeStruct((nseq * nchunk * CHUNK, D_MODEL), F32),
                   jax.ShapeDtypeStruct((nseq, H_C, HS_C, HS_C), F32)],
        scratch_shapes=[pltpu.VMEM((C_HB, HS_C, HS_C), F32)],
        compiler_params=_cparams(("parallel", "parallel", "arbitrary")),
        name="rwkv_scan",
    )(*ins)


def kernel(x_prompt, x_sample, cache_a_k, cache_a_v, cache_b_k, cache_b_v, cache_b_kidx, state_c_wkv, state_c_shift, norm_g, final_g, w_out, a_w_in, a_lam, a_subln_g, b_w_in, c_mu, c_w_rkvg, c_w0, c_w_la, c_w_lb, c_a0, c_a_la, c_a_lb, c_k_k, c_k_a, c_r_k, c_ln_w, c_ln_b):
    x = jnp.concatenate([x_prompt.reshape(N_PROMPT, D_MODEL),
                         x_sample.reshape(N_SAMPLE, D_MODEL)], axis=0)
    cos1, sn1, cos2, sn2 = _rope_tables()
    outs = {n: [] for n in ("akp", "avp", "aks", "avs", "bkp", "bvp", "bip", "bks", "bvs", "bis",
                            "cwp", "chp", "cws", "chs")}
    for i in range(DEPTH):
        kind, j = i % N_MIXERS, i // N_MIXERS
        if kind == 0:
            lam_init = 0.8 - 0.6 * math.exp(-0.3 * i)
            qkvg = _norm_proj(x, norm_g[i], a_w_in[j].astype(BF16), PROJ_TM, 1024)
            q_rot, kp, ks, vp, vs = _rope_a(qkvg, cos1, sn1)
            og_p = _attn_a_prompt(q_rot, kp, vp, qkvg, a_lam[j], a_subln_g[j], lam_init)
            og_s = _attn_a_sample(q_rot, ks, vs, qkvg,
                                  cache_a_k.reshape(-1, LANES), cache_a_v.reshape(-1, LANES), j,
                                  a_lam[j], a_subln_g[j], lam_init)
            outs["akp"].append(kp.reshape(BATCH, SEQ, H_A, 2 * DH_A))
            outs["avp"].append(vp.reshape(BATCH, SEQ, H_A, 2 * DH_A))
            outs["aks"].append(ks.reshape(DEC_BATCH, DEC_SEQ, H_A, 2 * DH_A))
            outs["avs"].append(vs.reshape(DEC_BATCH, DEC_SEQ, H_A, 2 * DH_A))
        elif kind == 1:
            w = b_w_in[j]
            w = jnp.concatenate([w[:, :4096], w[:, 4176:], w[:, 4096:4176],
                                 jnp.zeros((D_MODEL, B_COLS - w.shape[1]), w.dtype)], axis=1)
            proj = _norm_proj(x, norm_g[i], w.astype(BF16), PROJ_TM, 896)
            q_rot, qi_rot, kid, wi, kp, ks, vp, vs, ip, is_ = _rope_b(proj, cos1, sn1, cos2, sn2)
            og_p = _dsa_prompt(q_rot, kp, vp, qi_rot, kid, wi, proj)
            kidx_past = cache_b_kidx.reshape(-1, D_IDX)
            og_s = _dsa_sample(q_rot, ks, vs, qi_rot, kid, wi, proj,
                               cache_b_k.reshape(-1, KV_B * HD_B), cache_b_v.reshape(-1, KV_B * HD_B),
                               jnp.concatenate([kidx_past, kidx_past], axis=1), j)
            outs["bkp"].append(kp.reshape(BATCH, SEQ, KV_B, HD_B))
            outs["bvp"].append(vp.reshape(BATCH, SEQ, KV_B, HD_B))
            outs["bip"].append(ip.reshape(BATCH, SEQ, D_IDX))
            outs["bks"].append(ks.reshape(DEC_BATCH, DEC_SEQ, KV_B, HD_B))
            outs["bvs"].append(vs.reshape(DEC_BATCH, DEC_SEQ, KV_B, HD_B))
            outs["bis"].append(is_.reshape(DEC_BATCH, DEC_SEQ, D_IDX))
        else:
            shift0 = jnp.concatenate([jnp.zeros((BATCH, D_MODEL), F32), state_c_shift[j]], axis=0)
            lerp, hlast = _c_prep(x, norm_g[i], c_mu[j], shift0)
            rkvg = _c_bmm(lerp, c_w_rkvg[j].astype(BF16))
            wl, al = _c_lora(lerp, c_w_la[j], c_w_lb[j], c_a_la[j], c_a_lb[j], c_w0[j], c_a0[j])
            par = (c_k_k[j], c_k_a[j], c_r_k[j], c_ln_w[j], c_ln_b[j])
            og_p, st_p = _rwkv(rkvg, wl, al, *par, None, nseq=BATCH, nchunk=SEQ // CHUNK, row0=0)
            og_s, st_s = _rwkv(rkvg, wl, al, *par, state_c_wkv[j], nseq=DEC_BATCH, nchunk=1,
                               row0=N_PROMPT // CHUNK)
            outs["cwp"].append(st_p)
            outs["chp"].append(hlast[:BATCH, 0])
            outs["cws"].append(st_s)
            outs["chs"].append(hlast[BATCH:, 0])
        if i == DEPTH - 1:
            yp, ys = _proj_out(og_p, og_s, x, w_out[i].astype(BF16), final_g)
        else:
            x = _proj_out(og_p, og_s, x, w_out[i].astype(BF16))
    st = lambda n: jnp.stack(outs[n])
    return (yp.reshape(BATCH, SEQ, D_MODEL), ys.reshape(DEC_BATCH, DEC_SEQ, D_MODEL),
            st("akp"), st("avp"), st("aks"), st("avs"),
            st("bkp"), st("bvp"), st("bip"), st("bks"), st("bvs"), st("bis"),
            st("cwp"), st("chp"), st("cws"), st("chs"))
```

```python
import functools
import math

import jax
import jax.numpy as jnp
from jax import lax
from jax.experimental import pallas as pl
from jax.experimental.pallas import tpu as pltpu

F32 = jnp.float32
BF16 = jnp.bfloat16
I32 = jnp.int32

D_MODEL = 2048
BATCH = 4
SEQ = 2048
DEPTH = 4
DEC_BATCH = 8
DEC_SEQ = 64
PAST_LEN = 4096
CHUNK = 64
N_MIXERS = 3
ROPE_THETA = 10000.0
RMS_EPS = 1e-6
DH_A = 64
H_A = 16
SUBLN_EPS = 1e-5
HD_B = 128
H_B = 16
KV_B = 4
H_IDX = 16
D_IDX = 64
TOPK_MAX = 256
HS_C = 64
H_C = 32
R_DECAY = 96
R_ICLR = 96
GN_EPS = 64e-5

N_PROMPT = BATCH * SEQ
N_SAMPLE = DEC_BATCH * DEC_SEQ
M_ROWS = N_PROMPT + N_SAMPLE
LANES = 128
VMEM_LIMIT = 56 * 1024 * 1024
PROJ_TM = M_ROWS // 8
NEG_INF = float("-inf")
INT_MIN = -2 ** 31
F32_MIN_NORMAL = float.fromhex("0x1p-126")
F32_LOWEST = -float.fromhex("0x1.fffffep127")

B_Q0, B_K0, B_V0, B_QI0, B_G0, B_KI0 = 0, 2048, 2560, 3072, 4096, 6144
B_COLS = 6272

assert PAST_LEN % CHUNK == 0 and DEC_SEQ == CHUNK
LOG2_CHUNK = CHUNK.bit_length() - 1
assert 1 << LOG2_CHUNK == CHUNK


def _chunk_of(pos):
    return jnp.right_shift(pos, LOG2_CHUNK)


def _cparams(sem):
    return pltpu.CompilerParams(dimension_semantics=sem, vmem_limit_bytes=VMEM_LIMIT)


def _sigmoid(x):
    return 1.0 / (1.0 + jnp.exp(-x))


def _silu(x):
    return x * _sigmoid(x)


def _rms(x, g):
    return x * lax.rsqrt(jnp.mean(x * x, axis=-1, keepdims=True) + RMS_EPS) * g


def _dot_nt(a, b, precision=None):
    return lax.dot_general(a, b, (((1,), (1,)), ((), ())), precision=precision,
                           preferred_element_type=F32)


def _norm_proj_body(x_ref, g_ref, w_ref, o_ref, h_scr):
    @pl.when(pl.program_id(1) == 0)
    def _():
        h_scr[...] = _rms(x_ref[...], g_ref[...]).astype(BF16)
    o_ref[...] = jnp.dot(h_scr[...], w_ref[...], preferred_element_type=F32)


def _norm_proj(x, g, w, tm, tn):
    m, d = x.shape
    n = w.shape[1]
    return pl.pallas_call(
        _norm_proj_body,
        grid=(m // tm, n // tn),
        in_specs=[pl.BlockSpec((tm, d), lambda i, j: (i, 0)),
                  pl.BlockSpec((1, d), lambda i, j: (0, 0)),
                  pl.BlockSpec((d, tn), lambda i, j: (0, j))],
        out_specs=pl.BlockSpec((tm, tn), lambda i, j: (i, j)),
        out_shape=jax.ShapeDtypeStruct((m, n), F32),
        scratch_shapes=[pltpu.VMEM((tm, d), BF16)],
        compiler_params=_cparams(("parallel", "arbitrary")),
        name="norm_proj",
    )(x, g.reshape(1, d), w)


def _proj_out_body(ogp_ref, ogs_ref, x_ref, w_ref, o_ref, *, n_prompt_tiles):
    def emit(og_ref):
        o_ref[...] = x_ref[...] + jnp.dot(og_ref[...].astype(BF16), w_ref[...],
                                          preferred_element_type=F32)

    _when_group(n_prompt_tiles, emit, (ogp_ref,), (ogs_ref,))


def _proj_out_final_body(ogp_ref, ogs_ref, x_ref, w_ref, g_ref, yp_ref, ys_ref, *, n_prompt_tiles):
    def emit(og_ref, y_ref):
        xn = x_ref[...] + jnp.dot(og_ref[...].astype(BF16), w_ref[...], preferred_element_type=F32)
        y_ref[...] = _rms(xn, g_ref[...])

    _when_group(n_prompt_tiles, emit, (ogp_ref, yp_ref), (ogs_ref, ys_ref))


def _proj_out(og_prompt, og_sample, x, w, final_g=None, tm=256):
    m, d = x.shape
    row = pl.BlockSpec((tm, d), lambda i: (i, 0))
    full = pl.BlockSpec((d, d), lambda i: (0, 0))
    in_specs = _group_specs(tm, d) + [row, full]
    if final_g is None:
        body, extra, out_specs = _proj_out_body, (), row
        out_shape = jax.ShapeDtypeStruct((m, d), F32)
    else:
        body, extra = _proj_out_final_body, (final_g.reshape(1, d),)
        in_specs.append(pl.BlockSpec((1, d), lambda i: (0, 0)))
        out_specs, out_shape = _group_specs(tm, d), _group_shapes(d)
    return pl.pallas_call(
        functools.partial(body, n_prompt_tiles=N_PROMPT // tm),
        grid=(m // tm,),
        in_specs=in_specs,
        out_specs=out_specs,
        out_shape=out_shape,
        compiler_params=_cparams(("arbitrary",)),
        name="proj_out",
    )(og_prompt, og_sample, x, w, *extra)


def _rope_tables():
    pos = jnp.concatenate([jnp.tile(jnp.arange(SEQ), BATCH),
                           jnp.tile(PAST_LEN + jnp.arange(DEC_SEQ), DEC_BATCH)]).astype(F32)

    def table(dh, reps):
        half = dh // 2
        inv = jnp.power(ROPE_THETA, -jnp.arange(half, dtype=F32) * (2.0 / dh))
        ang = pos[:, None] * inv[None, :]
        cos, sin = jnp.cos(ang), jnp.sin(ang)
        return (jnp.tile(jnp.concatenate([cos, cos], axis=1), (1, reps)),
                jnp.tile(jnp.concatenate([-sin, sin], axis=1), (1, reps)))

    cos1, sn1 = table(64, 2)
    cos2, sn2 = table(128, 1)
    return cos1, sn1, cos2, sn2


def _rope64(x, cos, sn):
    lane = lax.broadcasted_iota(I32, x.shape, 1)
    partner = jnp.where((lane & 63) < 32, pltpu.roll(x, 96, 1), pltpu.roll(x, 32, 1))
    return x * cos + partner * sn


def _rope128(x, cos, sn):
    return x * cos + pltpu.roll(x, 64, 1) * sn


def _group_specs(tm, width):
    npt = N_PROMPT // tm
    return [pl.BlockSpec((tm, width), lambda i: (jnp.minimum(i, npt - 1), 0)),
            pl.BlockSpec((tm, width), lambda i: (jnp.maximum(i - npt, 0), 0))]


def _group_shapes(width):
    return [jax.ShapeDtypeStruct((N_PROMPT, width), F32), jax.ShapeDtypeStruct((N_SAMPLE, width), F32)]


def _when_group(n_prompt_tiles, emit, prompt_refs, sample_refs):
    is_prompt = pl.program_id(0) < n_prompt_tiles
    pl.when(is_prompt)(lambda: emit(*prompt_refs))
    pl.when(jnp.logical_not(is_prompt))(lambda: emit(*sample_refs))


def _rope_a_body(q_ref, k_ref, v_ref, cos_ref, sn_ref, qo_ref, kp_ref, ks_ref, vp_ref, vs_ref,
                 *, n_prompt_tiles):
    cos, sn = cos_ref[...], sn_ref[...]
    hsl = [slice(h * LANES, (h + 1) * LANES) for h in range(H_A)]
    for sl in hsl:
        qo_ref[:, sl] = _rope64(q_ref[:, sl], cos, sn) * (DH_A ** -0.5)

    def emit(ko_ref, vo_ref):
        for sl in hsl:
            ko_ref[:, sl] = _rope64(k_ref[:, sl], cos, sn)
        vo_ref[...] = v_ref[...]

    _when_group(n_prompt_tiles, emit, (kp_ref, vp_ref), (ks_ref, vs_ref))


def _rope_a(qkvg, cos1, sn1, tm=256):
    m = qkvg.shape[0]
    blk = lambda c: pl.BlockSpec((tm, D_MODEL), lambda i, c=c: (i, c))
    tab = pl.BlockSpec((tm, LANES), lambda i: (i, 0))
    return pl.pallas_call(
        functools.partial(_rope_a_body, n_prompt_tiles=N_PROMPT // tm),
        grid=(m // tm,),
        in_specs=[blk(0), blk(1), blk(2), tab, tab],
        out_specs=[blk(0)] + _group_specs(tm, D_MODEL) * 2,
        out_shape=[jax.ShapeDtypeStruct((m, D_MODEL), F32)] + _group_shapes(D_MODEL) * 2,
        compiler_params=_cparams(("arbitrary",)),
        name="rope_a",
    )(qkvg, qkvg, qkvg, cos1, sn1)


def _stack_maps_t(q):
    lane = lax.broadcasted_iota(I32, q.shape, 1)
    qs = jnp.concatenate([jnp.where(lane < DH_A, q, 0.0), jnp.where(lane >= DH_A, q, 0.0)], axis=0)
    return qs.T.astype(BF16)


def _flash_t_step(s, vt, carry):
    m, l, acc = carry
    m_new = jnp.maximum(m, jnp.max(s, axis=0, keepdims=True))
    alpha = jnp.exp(m - m_new)
    p = jnp.exp(s - m_new)
    l = alpha * l + jnp.sum(p, axis=0, keepdims=True)
    pv = lax.dot_general(vt, p.astype(BF16), DIMS_TN, preferred_element_type=F32)
    return m_new, l, alpha * acc + pv


def _flash_t_init(n):
    return (jnp.full((1, n), -1e30, F32), jnp.zeros((1, n), F32), jnp.zeros((LANES, n), F32))


def _diff_epilogue(l, acc, tq, lam_ref, sg, g, lam_init):
    o = (acc / l).T
    lp = lam_ref[...]
    lam = (jnp.exp(jnp.sum(lp[0:1] * lp[1:2], axis=-1, keepdims=True))
           - jnp.exp(jnp.sum(lp[2:3] * lp[3:4], axis=-1, keepdims=True)) + lam_init)
    o = o[:tq] - lam * o[tq:]
    o = o * lax.rsqrt(jnp.mean(o * o, axis=-1, keepdims=True) + SUBLN_EPS) * sg
    o = o * (1.0 - lam_init)
    return o * _silu(g)


A_HP = 4


def _attn_a_prompt_body(q_ref, k_ref, v_ref, g_ref, lam_ref, sg_ref, o_ref, *, tq, lam_init):
    qi = pl.program_id(2)
    hsl = [slice(h * LANES, (h + 1) * LANES) for h in range(A_HP)]
    qst = [_stack_maps_t(q_ref[:, sl]) for sl in hsl]
    n = 2 * tq
    krow = lax.broadcasted_iota(I32, (tq, n), 0)
    qcol = lax.broadcasted_iota(I32, (tq, n), 1) & (tq - 1)
    diag_mask = _chunk_of(krow) <= _chunk_of(qcol)

    def tile(kt, carries, masked):
        off = pl.multiple_of(kt * tq, tq)
        ss = [jnp.dot(k_ref[pl.ds(off, tq), sl].astype(BF16), qst[h], preferred_element_type=F32)
              for h, sl in enumerate(hsl)]
        out = []
        for h, sl in enumerate(hsl):
            s = jnp.where(diag_mask, ss[h], NEG_INF) if masked else ss[h]
            out.append(_flash_t_step(s, v_ref[pl.ds(off, tq), sl].astype(BF16), carries[h]))
        return tuple(out)

    carries = lax.fori_loop(0, qi, lambda kt, c: tile(kt, c, False),
                            tuple(_flash_t_init(n) for _ in hsl))
    carries = tile(qi, carries, True)
    for h, sl in enumerate(hsl):
        _, l, acc = carries[h]
        o_ref[:, sl] = _diff_epilogue(l, acc, tq, lam_ref, sg_ref[...], g_ref[:, sl], lam_init)


def _attn_a_prompt(q_rot, k_p, v_p, qkvg, lam_p, subln_g, lam_init, tq=256):
    nq = SEQ // tq
    w = A_HP * LANES
    nh = H_A // A_HP
    return pl.pallas_call(
        functools.partial(_attn_a_prompt_body, tq=tq, lam_init=lam_init),
        grid=(BATCH, nh, nq),
        in_specs=[pl.BlockSpec((tq, w), lambda b, h, i: (b * nq + i, h)),
                  pl.BlockSpec((SEQ, w), lambda b, h, i: (b, h)),
                  pl.BlockSpec((SEQ, w), lambda b, h, i: (b, h)),
                  pl.BlockSpec((tq, w), lambda b, h, i: (b * nq + i, 3 * nh + h)),
                  pl.BlockSpec((4, DH_A), lambda b, h, i: (0, 0)),
                  pl.BlockSpec((1, LANES), lambda b, h, i: (0, 0))],
        out_specs=pl.BlockSpec((tq, w), lambda b, h, i: (b * nq + i, h)),
        out_shape=jax.ShapeDtypeStruct((N_PROMPT, D_MODEL), F32),
        compiler_params=_cparams(("parallel", "parallel", "arbitrary")),
        name="attn_a_prompt",
    )(q_rot, k_p, v_p, qkvg, lam_p, subln_g.reshape(1, LANES))


def _attn_a_sample_body(q_ref, kp_ref, vp_ref, k_ref, v_ref, g_ref, lam_ref, sg_ref,
                        o_ref, qst_scr, ml_scr, acc_scr, *, tk, lam_init):
    kt = pl.program_id(1)
    tq = DEC_SEQ
    hsl = [slice(h * LANES, (h + 1) * LANES) for h in range(H_A)]

    @pl.when(kt == 0)
    def _():
        for h, sl in enumerate(hsl):
            qst_scr[h] = _stack_maps_t(q_ref[:, sl])
        m0, l0, acc0 = _flash_t_init(2 * tq)
        for h in range(H_A):
            ml_scr[h, 0:1, :] = m0
            ml_scr[h, 1:2, :] = l0
            acc_scr[h] = acc0

    def load(h):
        return ml_scr[h, 0:1, :], ml_scr[h, 1:2, :], acc_scr[h]

    def store(h, carry):
        ml_scr[h, 0:1, :], ml_scr[h, 1:2, :], acc_scr[h] = carry

    def head_group(i, _):
        hs = [A_HP * i + d for d in range(A_HP)]
        ss = [jnp.dot(kp_ref[pl.ds(h, tk, stride=H_A), :].astype(BF16), qst_scr[h],
                      preferred_element_type=F32) for h in hs]
        for h, s in zip(hs, ss):
            vb = vp_ref[pl.ds(h, tk, stride=H_A), :].astype(BF16)
            store(h, _flash_t_step(s, vb, load(h)))
        return 0

    lax.fori_loop(0, H_A // A_HP, head_group, 0)

    @pl.when(kt == pl.num_programs(1) - 1)
    def _():
        for h, sl in enumerate(hsl):
            s = jnp.dot(k_ref[:, sl].astype(BF16), qst_scr[h], preferred_element_type=F32)
            _, l, acc = _flash_t_step(s, v_ref[:, sl].astype(BF16), load(h))
            o_ref[:, sl] = _diff_epilogue(l, acc, tq, lam_ref, sg_ref[...], g_ref[:, sl], lam_init)


def _attn_a_sample(q_rot, k_s, v_s, qkvg, k_past, v_past, layer, lam_p, subln_g, lam_init, tk=512):
    r0 = N_PROMPT // DEC_SEQ
    nkt = PAST_LEN // tk
    new = lambda c: pl.BlockSpec((DEC_SEQ, D_MODEL), lambda b, t, c=c: (r0 + b, c))
    own = pl.BlockSpec((DEC_SEQ, D_MODEL), lambda b, t: (b, 0))
    past = pl.BlockSpec((tk * H_A, LANES), lambda b, t: ((layer * DEC_BATCH + b) * nkt + t, 0))
    return pl.pallas_call(
        functools.partial(_attn_a_sample_body, tk=tk, lam_init=lam_init),
        grid=(DEC_BATCH, nkt),
        in_specs=[new(0), past, past, own, own, new(3),
                  pl.BlockSpec((4, DH_A), lambda b, t: (0, 0)),
                  pl.BlockSpec((1, LANES), lambda b, t: (0, 0))],
        out_specs=own,
        out_shape=jax.ShapeDtypeStruct((N_SAMPLE, D_MODEL), F32),
        scratch_shapes=[pltpu.VMEM((H_A, LANES, 2 * DEC_SEQ), BF16),
                        pltpu.VMEM((H_A, 8, 2 * DEC_SEQ), F32),
                        pltpu.VMEM((H_A, LANES, 2 * DEC_SEQ), F32)],
        compiler_params=_cparams(("parallel", "arbitrary")),
        name="attn_a_sample",
    )(q_rot, k_past, v_past, k_s, v_s, qkvg, lam_p, subln_g.reshape(1, LANES))


def _rope_b_body(q_ref, k_ref, v_ref, qi_ref, sm_ref, c1_ref, s1_ref, c2_ref, s2_ref,
                 qo_ref, qio_ref, kid_ref, wi_ref, kp_ref, ks_ref, vp_ref, vs_ref, ip_ref, is_ref,
                 *, n_prompt_tiles):
    c1, s1, c2, s2 = c1_ref[...], s1_ref[...], c2_ref[...], s2_ref[...]
    for h in range(H_B):
        sl = slice(h * LANES, (h + 1) * LANES)
        qo_ref[:, sl] = _rope128(q_ref[:, sl], c2, s2) * (HD_B ** -0.5)
    for h in range(H_IDX * D_IDX // LANES):
        sl = slice(h * LANES, (h + 1) * LANES)
        qio_ref[:, sl] = _rope64(qi_ref[:, sl], c1, s1) * (D_IDX ** -0.5)
    sm = sm_ref[...]
    lane = lax.broadcasted_iota(I32, sm.shape, 1)
    kr = _rope64(sm, c1, s1)
    kid_ref[...] = jnp.where(lane < D_IDX, kr, pltpu.roll(kr, D_IDX, 1))
    wi_ref[...] = pltpu.roll(sm, D_IDX, 1) * (H_IDX ** -0.5)

    def emit(ko_ref, vo_ref, io_ref):
        for h in range(KV_B):
            sl = slice(h * LANES, (h + 1) * LANES)
            ko_ref[:, sl] = _rope128(k_ref[:, sl], c2, s2)
        vo_ref[...] = v_ref[...]
        io_ref[...] = kr[:, :D_IDX]

    _when_group(n_prompt_tiles, emit, (kp_ref, vp_ref, ip_ref), (ks_ref, vs_ref, is_ref))


def _rope_b(proj, cos1, sn1, cos2, sn2, tm=256):
    m = proj.shape[0]
    tab = pl.BlockSpec((tm, LANES), lambda i: (i, 0))
    blk = lambda w, c: pl.BlockSpec((tm, w), lambda i, c=c: (i, c))
    widths = (H_B * HD_B, H_IDX * D_IDX, LANES, LANES)
    kvw = KV_B * HD_B
    return pl.pallas_call(
        functools.partial(_rope_b_body, n_prompt_tiles=N_PROMPT // tm),
        grid=(m // tm,),
        in_specs=[blk(2048, 0), blk(kvw, B_K0 // kvw), blk(kvw, B_V0 // kvw),
                  blk(1024, B_QI0 // 1024), blk(LANES, B_KI0 // LANES), tab, tab, tab, tab],
        out_specs=([blk(w, 0) for w in widths] + _group_specs(tm, kvw) * 2
                   + _group_specs(tm, D_IDX)),
        out_shape=([jax.ShapeDtypeStruct((m, w), F32) for w in widths] + _group_shapes(kvw) * 2
                   + _group_shapes(D_IDX)),
        compiler_params=_cparams(("arbitrary",)),
        name="rope_b",
    )(proj, proj, proj, proj, proj, cos1, sn1, cos2, sn2)


POS_BITS = 13
DSA_BUCKET = 512


def _count(pred):
    return jnp.sum(jnp.where(pred, 1.0, 0.0), axis=-1, keepdims=True)


def _select_topk(scores, poss, n_sel):
    tq = scores[0].shape[0]
    n_sel = float(n_sel)

    def key_to_float(key):
        return lax.bitcast_convert_type(jnp.where(key < 0, key ^ 0x7FFFFFFF, key), F32)

    def thr_step(i, t):
        cand_bits = t | lax.shift_left(jnp.int32(1), 31 - i)
        cand = key_to_float(cand_bits ^ INT_MIN)
        cnt = sum(_count(s >= cand) for s in scores)
        return jnp.where(cnt >= n_sel, cand_bits, t)

    t = lax.fori_loop(0, 32, thr_step, jnp.zeros((tq, 1), I32))
    thr_key = t ^ INT_MIN
    thr, above = key_to_float(thr_key), key_to_float(thr_key + 1)
    above = jnp.where(thr == 0.0, F32_MIN_NORMAL, above)
    few = sum(_count(s > NEG_INF) for s in scores) <= n_sel
    thr = jnp.where(few, F32_LOWEST, thr)
    above = jnp.where(few, F32_LOWEST, above)
    gts = [s >= above for s in scores]
    eqs = [(s >= thr) & jnp.logical_not(g) for s, g in zip(scores, gts)]
    need = n_sel - sum(_count(g) for g in gts)
    tied = [jnp.where(e, 1.0, 0.0) for e in eqs]
    n_tied = sum(jnp.sum(e, axis=-1, keepdims=True) for e in tied)

    def tie_step(i, j):
        cand = j | lax.shift_left(jnp.int32(1), POS_BITS - 1 - i)
        cnt = sum(jnp.sum(jnp.where(p < cand, e, 0.0), axis=-1, keepdims=True)
                  for e, p in zip(tied, poss))
        return jnp.where(cnt < need, cand, j)

    j = lax.cond(jnp.max(n_tied - need) > 0.0,
                 lambda: lax.fori_loop(0, POS_BITS, tie_step, jnp.zeros((tq, 1), I32)),
                 lambda: jnp.full((tq, 1), (1 << POS_BITS) - 1, I32))
    return [g | (e & (p <= j)) for g, e, p in zip(gts, eqs, poss)]


def _dsa_body(*refs, tq, seg_lens, seg_pos0, q_pos0, causal, n_sel, bucket):
    nseg = len(seg_lens)
    n_in = 4 + 3 * nseg
    assert max(seg_pos0[i] + seg_lens[i] for i in range(nseg)) <= 1 << POS_BITS
    if bucket is None:
        _dsa_run(refs, tq, seg_lens, seg_pos0, q_pos0, causal, n_sel, n_in)
        return
    assert causal and nseg == 1 and seg_pos0[0] == 0 and q_pos0 == 0 and bucket % tq == 0
    last = (pl.program_id(1) * tq) // bucket
    for v in range(seg_lens[0] // bucket):
        pl.when(last == v)(functools.partial(_dsa_run, refs, tq, ((v + 1) * bucket,), seg_pos0,
                                             q_pos0, causal, n_sel, n_in))


def _dsa_run(refs, tq, seg_lens, seg_pos0, q_pos0, causal, n_sel, n_in):
    nseg = len(seg_lens)
    qi_ref, wi_ref, q_ref, g_ref = refs[:4]
    kid_refs = refs[4:4 + nseg]
    k_refs = refs[4 + nseg:4 + 2 * nseg]
    v_refs = refs[4 + 2 * nseg:4 + 3 * nseg]
    o_ref = refs[n_in]
    bias_refs = refs[n_in + 1:]
    n_rep = H_B // KV_B

    @pl.when(pl.program_id(2) == 0)
    def _():
        wi = wi_ref[...]
        qpos = q_pos0 + pl.program_id(1) * tq
        scores, poss, adms = [], [], []
        for si in range(nseg):
            sl = seg_lens[si]
            kid = kid_refs[si][0:sl, :].astype(BF16)
            sc = jnp.zeros((tq, sl), F32)
            for p in range(H_IDX // 2):
                qp = qi_ref[:, p * LANES:(p + 1) * LANES]
                lane = lax.broadcasted_iota(I32, qp.shape, 1)
                qs = jnp.concatenate([jnp.where(lane < D_IDX, qp, 0.0),
                                      jnp.where(lane >= D_IDX, qp, 0.0)], axis=0).astype(BF16)
                lg = jnp.maximum(_dot_nt(qs, kid), 0.0)
                sc = sc + wi[:, 2 * p:2 * p + 1] * lg[:tq] + wi[:, 2 * p + 1:2 * p + 2] * lg[tq:]
            kpos = seg_pos0[si] + lax.broadcasted_iota(I32, (tq, sl), 1)
            if causal:
                qrow = qpos + lax.broadcasted_iota(I32, (tq, sl), 0)
                adm = _chunk_of(kpos) <= _chunk_of(qrow)
                sc = jnp.where(adm, sc, NEG_INF)
            else:
                adm = None
            scores.append(sc)
            poss.append(kpos)
            adms.append(adm)
        sels = _select_topk(scores, poss, n_sel)
        for si in range(nseg):
            bias = jnp.where(sels[si], 0.0, NEG_INF)
            if adms[si] is not None:
                bias = jnp.where(adms[si], bias, NEG_INF)
            bias_refs[si][:, 0:seg_lens[si]] = bias

    q = q_ref[...]
    qg = jnp.concatenate([q[:, r * LANES:(r + 1) * LANES] for r in range(n_rep)],
                         axis=0).astype(BF16)
    ss = []
    for si in range(nseg):
        b = bias_refs[si][:, 0:seg_lens[si]]
        s = _dot_nt(qg, k_refs[si][0:seg_lens[si], :].astype(BF16))
        ss.append(s + jnp.concatenate([b] * n_rep, axis=0))
    m = functools.reduce(jnp.maximum, [jnp.max(s, axis=-1, keepdims=True) for s in ss])
    l = 0.0
    acc = 0.0
    for si in range(nseg):
        p = jnp.exp(ss[si] - m)
        l = l + jnp.sum(p, axis=-1, keepdims=True)
        acc = acc + jnp.dot(p.astype(BF16), v_refs[si][0:seg_lens[si], :].astype(BF16),
                            preferred_element_type=F32)
    o = acc / l
    o = jnp.concatenate([o[r * tq:(r + 1) * tq] for r in range(n_rep)], axis=1)
    o_ref[...] = o * _silu(g_ref[...])


def _dsa_prompt(q_rot, k_p, v_p, qi_rot, kid, wi, proj, tq=128):
    nq = SEQ // tq
    gw = KV_B * HD_B
    row = lambda w, c0: pl.BlockSpec((tq, w), lambda b, i, g, c0=c0: (b * nq + i, c0 + g))
    row0 = lambda w: pl.BlockSpec((tq, w), lambda b, i, g: (b * nq + i, 0))
    seq = lambda c0: pl.BlockSpec((SEQ, LANES), lambda b, i, g, c0=c0: (b, c0 + g))
    n_sel = min(TOPK_MAX, SEQ // 4)
    return pl.pallas_call(
        functools.partial(_dsa_body, tq=tq, seg_lens=(SEQ,), seg_pos0=(0,), q_pos0=0,
                          causal=True, n_sel=n_sel, bucket=DSA_BUCKET),
        grid=(BATCH, nq, KV_B),
        in_specs=[row0(H_IDX * D_IDX), row0(LANES), row(gw, 0), row(gw, B_G0 // gw),
                  pl.BlockSpec((SEQ, LANES), lambda b, i, g: (b, 0)),
                  seq(0), seq(0)],
        out_specs=row(gw, 0),
        out_shape=jax.ShapeDtypeStruct((N_PROMPT, D_MODEL), F32),
        scratch_shapes=[pltpu.VMEM((tq, SEQ), F32)],
        compiler_params=_cparams(("parallel", "parallel", "arbitrary")),
        name="dsa_prompt",
    )(qi_rot, wi, q_rot, proj, kid, k_p, v_p)


def _dsa_sample(q_rot, k_s, v_s, qi_rot, kid, wi, proj, k_past, v_past, kid_past, layer):
    tq = DEC_SEQ
    r0 = N_PROMPT // tq
    gw = KV_B * HD_B
    row = lambda w, c0: pl.BlockSpec((tq, w), lambda b, i, g, c0=c0: (r0 + b, c0 + g))
    row0 = lambda w: pl.BlockSpec((tq, w), lambda b, i, g: (r0 + b, 0))
    past = pl.BlockSpec((PAST_LEN, LANES), lambda b, i, g: (layer * DEC_BATCH + b, g))
    own = pl.BlockSpec((tq, LANES), lambda b, i, g: (b, g))
    s_all = PAST_LEN + DEC_SEQ
    n_sel = min(TOPK_MAX, s_all // 4)
    return pl.pallas_call(
        functools.partial(_dsa_body, tq=tq, seg_lens=(PAST_LEN, DEC_SEQ),
                          seg_pos0=(0, PAST_LEN), q_pos0=PAST_LEN, causal=False, n_sel=n_sel,
                          bucket=None),
        grid=(DEC_BATCH, 1, KV_B),
        in_specs=[row0(H_IDX * D_IDX), row0(LANES), row(gw, 0), row(gw, B_G0 // gw),
                  pl.BlockSpec((PAST_LEN, LANES), lambda b, i, g: (layer * DEC_BATCH + b, 0)),
                  row0(LANES),
                  past, own,
                  past, own],
        out_specs=pl.BlockSpec((tq, gw), lambda b, i, g: (b, g)),
        out_shape=jax.ShapeDtypeStruct((N_SAMPLE, D_MODEL), F32),
        scratch_shapes=[pltpu.VMEM((tq, PAST_LEN), F32), pltpu.VMEM((tq, DEC_SEQ), F32)],
        compiler_params=_cparams(("parallel", "arbitrary", "arbitrary")),
        name="dsa_sample",
    )(qi_rot, wi, q_rot, proj, kid_past, kid, k_past, k_s, v_past, v_s)


C_TM = 64
C_TILES_PER_SEQ = SEQ // C_TM
C_PROMPT_TILES = N_PROMPT // C_TM


def _c_seq_id(i):
    return jnp.where(i < C_PROMPT_TILES, i // C_TILES_PER_SEQ, i - C_PROMPT_TILES + BATCH)


def _c_prep_body(x_ref, g_ref, mu_ref, sh_ref, l_ref, hl_ref, carry):
    i = pl.program_id(0)

    @pl.when(i == 0)
    def _():
        carry[...] = jnp.zeros_like(carry)

    h = _rms(x_ref[...], g_ref[...])
    start = jnp.logical_or(i >= C_PROMPT_TILES, i % C_TILES_PER_SEQ == 0)
    first = jnp.where(start, sh_ref[0], carry[...])
    row = lax.broadcasted_iota(I32, h.shape, 0)
    prev = jnp.where(row == 0, first, pltpu.roll(h, 1, 0))
    last = h[C_TM - 1:C_TM, :]
    carry[...] = last
    hl_ref[0] = last
    d = prev - h
    for n in range(6):
        l_ref[n] = (h + d * mu_ref[n:n + 1, :]).astype(BF16)


def _c_prep(x, g, mu, shift0):
    m, d = x.shape
    nseq = shift0.shape[0]
    return pl.pallas_call(
        _c_prep_body,
        grid=(m // C_TM,),
        in_specs=[pl.BlockSpec((C_TM, d), lambda i: (i, 0)),
                  pl.BlockSpec((1, d), lambda i: (0, 0)),
                  pl.BlockSpec((6, d), lambda i: (0, 0)),
                  pl.BlockSpec((1, 1, d), lambda i: (_c_seq_id(i), 0, 0))],
        out_specs=[pl.BlockSpec((6, C_TM, d), lambda i: (0, i, 0)),
                   pl.BlockSpec((1, 1, d), lambda i: (_c_seq_id(i), 0, 0))],
        out_shape=[jax.ShapeDtypeStruct((6, m, d), BF16),
                   jax.ShapeDtypeStruct((nseq, 1, d), F32)],
        scratch_shapes=[pltpu.VMEM((1, d), F32)],
        compiler_params=_cparams(("arbitrary",)),
        name="c_prep",
    )(x, g.reshape(1, d), mu, shift0.reshape(nseq, 1, d))


def _bmm_body(l_ref, w_ref, o_ref):
    o_ref[0] = jnp.dot(l_ref[0], w_ref[0], preferred_element_type=F32)


def _c_bmm(lerp, w, tm=PROJ_TM, tn=1024):
    _, m, d = lerp.shape
    nb, _, n = w.shape
    return pl.pallas_call(
        _bmm_body,
        grid=(nb, m // tm, n // tn),
        in_specs=[pl.BlockSpec((1, tm, d), lambda b, i, j: (b, i, 0)),
                  pl.BlockSpec((1, d, tn), lambda b, i, j: (b, 0, j))],
        out_specs=pl.BlockSpec((1, tm, tn), lambda b, i, j: (b, i, j)),
        out_shape=jax.ShapeDtypeStruct((nb, m, n), F32),
        compiler_params=_cparams(("parallel", "parallel", "arbitrary")),
        name="c_bmm",
    )(lerp, w)


def _c_lora_body(l4_ref, l5_ref, wla_ref, wlb_ref, ala_ref, alb_ref, w0_ref, a0_ref,
                 wl_ref, al_ref):
    tw = jnp.tanh(jnp.dot(l4_ref[0], wla_ref[...], preferred_element_type=F32))
    wl_ref[...] = w0_ref[...] + jnp.dot(tw.astype(BF16), wlb_ref[...],
                                        preferred_element_type=F32)
    ta = jnp.dot(l5_ref[0], ala_ref[...], preferred_element_type=F32)
    al_ref[...] = a0_ref[...] + jnp.dot(ta.astype(BF16), alb_ref[...],
                                        preferred_element_type=F32)


def _c_lora(lerp, w_la, w_lb, a_la, a_lb, w0, a0, tm=512):
    _, m, d = lerp.shape
    pad_in = lambda w: jnp.pad(w, ((0, 0), (0, LANES - w.shape[1]))).astype(BF16)
    pad_out = lambda w: jnp.pad(w, ((0, LANES - w.shape[0]), (0, 0))).astype(BF16)
    lin = lambda n: pl.BlockSpec((1, tm, d), lambda i, n=n: (n, i, 0))
    win = pl.BlockSpec((d, LANES), lambda i: (0, 0))
    wout = pl.BlockSpec((LANES, d), lambda i: (0, 0))
    vec = pl.BlockSpec((1, d), lambda i: (0, 0))
    out = pl.BlockSpec((tm, d), lambda i: (i, 0))
    return pl.pallas_call(
        _c_lora_body,
        grid=(m // tm,),
        in_specs=[lin(4), lin(5), win, wout, win, wout, vec, vec],
        out_specs=[out, out],
        out_shape=[jax.ShapeDtypeStruct((m, d), F32)] * 2,
        compiler_params=_cparams(("parallel",)),
        name="c_lora",
    )(lerp, lerp, pad_in(w_la), pad_out(w_lb), pad_in(a_la), pad_out(a_lb),
      w0.reshape(1, d), a0.reshape(1, d))


C_HB = 16
DIMS_NN = (((1,), (0,)), ((), ()))
DIMS_NT = (((1,), (1,)), ((), ()))
DIMS_TN = (((0,), (0,)), ((), ()))


def _dot1(a, b, dims):
    return lax.dot_general(a.astype(BF16), b.astype(BF16), dims, preferred_element_type=F32)


def _dot3(a, b, dims):
    (ca,), (cb,) = dims[0]
    ah = a.astype(BF16).astype(F32)
    bh = b.astype(BF16).astype(F32)
    sa = jnp.concatenate([ah, a - ah, ah], axis=ca).astype(BF16)
    sb = jnp.concatenate([bh, bh, b - bh], axis=cb).astype(BF16)
    return lax.dot_general(sa, sb, dims, preferred_element_type=F32)


def _cumsum_rows(x):
    row = lax.broadcasted_iota(I32, x.shape, 0)
    sh = 1
    while sh < x.shape[0]:
        x = x + jnp.where(row >= sh, pltpu.roll(x, sh, 0), 0.0)
        sh *= 2
    return x


def _rwkv_body(*refs, has_init):
    if has_init:
        (r_ref, k_ref, v_ref, g_ref, wl_ref, al_ref, kk_ref, ka_ref, rk_ref, lnw_ref, lnb_ref,
         s0_ref, og_ref, so_ref, s_scr) = refs
    else:
        (r_ref, k_ref, v_ref, g_ref, wl_ref, al_ref, kk_ref, ka_ref, rk_ref, lnw_ref, lnb_ref,
         og_ref, so_ref, s_scr) = refs
    c = pl.program_id(2)
    n = CHUNK
    hs = HS_C

    @pl.when(c == 0)
    def _():
        if has_init:
            s_scr[...] = s0_ref[0]
        else:
            s_scr[...] = jnp.zeros_like(s_scr)

    r, k, v, g = r_ref[0], k_ref[0], v_ref[0], g_ref[0]
    logw = -math.exp(-0.5) * _sigmoid(wl_ref[...])
    a = _sigmoid(al_ref[...])
    kkr = k * kk_ref[...]
    k2 = k * (1.0 + (a - 1.0) * ka_ref[...])
    bonus_in = r * k2 * rk_ref[...]

    cs_all = _cumsum_rows(logw)
    ec_all, eci_all, ecp_all = jnp.exp(cs_all), jnp.exp(-cs_all), jnp.exp(cs_all - logw)

    ti = lax.broadcasted_iota(I32, (n, n), 0)
    tj = lax.broadcasted_iota(I32, (n, n), 1)
    eye = jnp.where(tj == ti, 1.0, 0.0)
    gi = lax.broadcasted_iota(I32, (2 * n, 2 * n), 0)
    gj = lax.broadcasted_iota(I32, (2 * n, 2 * n), 1) & (n - 1)
    gmask = ((gi < n) & (gj < gi)) | ((gi >= n) & (gj <= gi - n))
    lvl_masks = []
    bs = 1
    while bs < n:
        sh = bs.bit_length()
        lvl_masks.append((jnp.right_shift(ti, sh) == jnp.right_shift(tj, sh))
                         & ((ti & bs) != 0) & ((tj & bs) == 0))
        bs *= 2

    heads = range(C_HB)
    sls = [slice(j * hs, (j + 1) * hs) for j in heads]
    vs = [v[:, sl] for sl in sls]
    zero = jnp.zeros_like(vs[0])
    gw = 4 * hs
    ei = lax.broadcasted_iota(I32, (gw, gw), 0)
    ej = lax.broadcasted_iota(I32, (gw, gw), 1)
    lg = hs.bit_length() - 1
    seg = jnp.where(jnp.right_shift(ei, lg) == jnp.right_shift(ej, lg), 1.0, 0.0).astype(BF16)
    sq = kkr * kkr
    sq_hi = sq.astype(BF16)
    sq_lo = (sq - sq_hi.astype(F32)).astype(BF16)
    ssq = jnp.concatenate(
        [jnp.dot(sq_hi[:, c:c + gw], seg, preferred_element_type=F32)
         + jnp.dot(sq_lo[:, c:c + gw], seg, preferred_element_type=F32)
         for c in range(0, C_HB * hs, gw)], axis=1)
    kkn = kkr / jnp.maximum(jnp.sqrt(ssq), 1e-12)
    p_all, q_all = kkn * ecp_all, kkn * a * eci_all
    kt_all, rt_all = k2 * eci_all, r * ec_all
    p_, q_ = [p_all[:, sl] for sl in sls], [q_all[:, sl] for sl in sls]
    kt, rt = [kt_all[:, sl] for sl in sls], [rt_all[:, sl] for sl in sls]
    gm = [jnp.where(gmask, _dot1(jnp.concatenate([p_[j], rt[j]], axis=0),
                                 jnp.concatenate([q_[j], kt[j]], axis=0), DIMS_NT), 0.0)
          for j in heads]
    av = [_dot1(gm[j][:n], jnp.concatenate([zero, vs[j]], axis=0), DIMS_NN)
          for j in heads]
    a_qp = [gm[j][:n, :n] for j in heads]
    x = [eye - jnp.where(lvl_masks[0], a_qp[j], 0.0) for j in heads]
    for msk in lvl_masks[1:]:
        ax = [_dot1(jnp.where(msk, a_qp[j], 0.0), x[j], DIMS_NN) for j in heads]
        x = [x[j] - _dot1(x[j], ax[j], DIMS_NN) for j in heads]
    xpw = [_dot1(x[j], jnp.concatenate([p_[j], av[j]], axis=1), DIMS_NN)
           for j in heads]
    low = [jnp.concatenate([zero, vs[j]], axis=1) for j in heads]
    tb = [_dot3(jnp.concatenate([xpw[j], low[j]], axis=0),
                jnp.concatenate([-q_[j], kt[j]], axis=0), DIMS_TN) for j in heads]
    ro = [_dot1(gm[j][n:], jnp.concatenate([-xpw[j], low[j]], axis=0), DIMS_NN) for j in heads]
    s0 = [s_scr[j] for j in heads]
    o = [_dot1(rt[j] + ro[j][:, :hs], s0[j], DIMS_NT) + ro[j][:, hs:] for j in heads]
    for j in heads:
        ecl = ec_all[n - 1:n, sls[j]]
        s_scr[j] = _dot3(s0[j], (eye + tb[j][:hs]) * ecl, DIMS_NN) + tb[j][hs:] * ecl
    outs = []
    for j in heads:
        mean = jnp.mean(o[j], axis=-1, keepdims=True)
        var = jnp.mean(jnp.square(o[j] - mean), axis=-1, keepdims=True)
        on = (o[j] - mean) * lax.rsqrt(var + GN_EPS) * lnw_ref[:, sls[j]] + lnb_ref[:, sls[j]]
        bonus = jnp.sum(bonus_in[:, sls[j]], axis=-1, keepdims=True) * vs[j]
        outs.append(on + bonus)
    og_ref[...] = jnp.concatenate(outs, axis=1) * _silu(g)

    @pl.when(c == pl.num_programs(2) - 1)
    def _():
        so_ref[0] = s_scr[...]


def _rwkv(rkvg, wl, al, k_k, k_a, r_k, ln_w, ln_b, s0, *, nseq, nchunk, row0):
    w = C_HB * HS_C
    tok = lambda n: pl.BlockSpec((1, CHUNK, w), lambda s, h, c, n=n: (n, row0 + s * nchunk + c, h))
    tok2 = pl.BlockSpec((CHUNK, w), lambda s, h, c: (row0 + s * nchunk + c, h))
    vec = pl.BlockSpec((1, w), lambda s, h, c: (0, h))
    st = pl.BlockSpec((1, C_HB, HS_C, HS_C), lambda s, h, c: (s, h, 0, 0))
    has_init = s0 is not None
    ins = [rkvg, rkvg, rkvg, rkvg, wl, al] + [p.reshape(1, D_MODEL) for p in (k_k, k_a, r_k, ln_w, ln_b)]
    specs = [tok(0), tok(1), tok(2), tok(3), tok2, tok2] + [vec] * 5
    if has_init:
        ins.append(s0)
        specs.append(st)
    return pl.pallas_call(
        functools.partial(_rwkv_body, has_init=has_init),
        grid=(nseq, H_C // C_HB, nchunk),
        in_specs=specs,
        out_specs=[pl.BlockSpec((CHUNK, w), lambda s, h, c: (s * nchunk + c, h)), st],
        out_shape=[jax.ShapeDtypeStruct((nseq * nchunk * CHUNK, D_MODEL), F32),
                   jax.ShapeDtypeStruct((nseq, H_C, HS_C, HS_C), F32)],
        scratch_shapes=[pltpu.VMEM((C_HB, HS_C, HS_C), F32)],
        compiler_params=_cparams(("parallel", "parallel", "arbitrary")),
        name="rwkv_scan",
    )(*ins)


def kernel(x_prompt, x_sample, cache_a_k, cache_a_v, cache_b_k, cache_b_v, cache_b_kidx, state_c_wkv, state_c_shift, norm_g, final_g, w_out, a_w_in, a_lam, a_subln_g, b_w_in, c_mu, c_w_rkvg, c_w0, c_w_la, c_w_lb, c_a0, c_a_la, c_a_lb, c_k_k, c_k_a, c_r_k, c_ln_w, c_ln_b):
    x = jnp.concatenate([x_prompt.reshape(N_PROMPT, D_MODEL),
                         x_sample.reshape(N_SAMPLE, D_MODEL)], axis=0)
    cos1, sn1, cos2, sn2 = _rope_tables()
    outs = {n: [] for n in ("akp", "avp", "aks", "avs", "bkp", "bvp", "bip", "bks", "bvs", "bis",
                            "cwp", "chp", "cws", "chs")}
    for i in range(DEPTH):
        kind, j = i % N_MIXERS, i // N_MIXERS
        if kind == 0:
            lam_init = 0.8 - 0.6 * math.exp(-0.3 * i)
            qkvg = _norm_proj(x, norm_g[i], a_w_in[j].astype(BF16), PROJ_TM, 1024)
            q_rot, kp, ks, vp, vs = _rope_a(qkvg, cos1, sn1)
            og_p = _attn_a_prompt(q_rot, kp, vp, qkvg, a_lam[j], a_subln_g[j], lam_init)
            og_s = _attn_a_sample(q_rot, ks, vs, qkvg,
                                  cache_a_k.reshape(-1, LANES), cache_a_v.reshape(-1, LANES), j,
                                  a_lam[j], a_subln_g[j], lam_init)
            outs["akp"].append(kp.reshape(BATCH, SEQ, H_A, 2 * DH_A))
            outs["avp"].append(vp.reshape(BATCH, SEQ, H_A, 2 * DH_A))
            outs["aks"].append(ks.reshape(DEC_BATCH, DEC_SEQ, H_A, 2 * DH_A))
            outs["avs"].append(vs.reshape(DEC_BATCH, DEC_SEQ, H_A, 2 * DH_A))
        elif kind == 1:
            w = b_w_in[j]
            w = jnp.concatenate([w[:, :4096], w[:, 4176:], w[:, 4096:4176],
                                 jnp.zeros((D_MODEL, B_COLS - w.shape[1]), w.dtype)], axis=1)
            proj = _norm_proj(x, norm_g[i], w.astype(BF16), PROJ_TM, 896)
            q_rot, qi_rot, kid, wi, kp, ks, vp, vs, ip, is_ = _rope_b(proj, cos1, sn1, cos2, sn2)
            og_p = _dsa_prompt(q_rot, kp, vp, qi_rot, kid, wi, proj)
            kidx_past = cache_b_kidx.reshape(-1, D_IDX)
            og_s = _dsa_sample(q_rot, ks, vs, qi_rot, kid, wi, proj,
                               cache_b_k.reshape(-1, KV_B * HD_B), cache_b_v.reshape(-1, KV_B * HD_B),
                               jnp.concatenate([kidx_past, kidx_past], axis=1), j)
            outs["bkp"].append(kp.reshape(BATCH, SEQ, KV_B, HD_B))
            outs["bvp"].append(vp.reshape(BATCH, SEQ, KV_B, HD_B))
            outs["bip"].append(ip.reshape(BATCH, SEQ, D_IDX))
            outs["bks"].append(ks.reshape(DEC_BATCH, DEC_SEQ, KV_B, HD_B))
            outs["bvs"].append(vs.reshape(DEC_BATCH, DEC_SEQ, KV_B, HD_B))
            outs["bis"].append(is_.reshape(DEC_BATCH, DEC_SEQ, D_IDX))
        else:
            shift0 = jnp.concatenate([jnp.zeros((BATCH, D_MODEL), F32), state_c_shift[j]], axis=0)
            lerp, hlast = _c_prep(x, norm_g[i], c_mu[j], shift0)
            rkvg = _c_bmm(lerp, c_w_rkvg[j].astype(BF16))
            wl, al = _c_lora(lerp, c_w_la[j], c_w_lb[j], c_a_la[j], c_a_lb[j], c_w0[j], c_a0[j])
            par = (c_k_k[j], c_k_a[j], c_r_k[j], c_ln_w[j], c_ln_b[j])
            og_p, st_p = _rwkv(rkvg, wl, al, *par, None, nseq=BATCH, nchunk=SEQ // CHUNK, row0=0)
            og_s, st_s = _rwkv(rkvg, wl, al, *par, state_c_wkv[j], nseq=DEC_BATCH, nchunk=1,
                               row0=N_PROMPT // CHUNK)
            outs["cwp"].append(st_p)
            outs["chp"].append(hlast[:BATCH, 0])
            outs["cws"].append(st_s)
            outs["chs"].append(hlast[BATCH:, 0])
        if i == DEPTH - 1:
            yp, ys = _proj_out(og_p, og_s, x, w_out[i].astype(BF16), final_g)
        else:
            x = _proj_out(og_p, og_s, x, w_out[i].astype(BF16))
    st = lambda n: jnp.stack(outs[n])
    return (yp.reshape(BATCH, SEQ, D_MODEL), ys.reshape(DEC_BATCH, DEC_SEQ, D_MODEL),
            st("akp"), st("avp"), st("aks"), st("avs"),
            st("bkp"), st("bvp"), st("bip"), st("bks"), st("bvs"), st("bis"),
            st("cwp"), st("chp"), st("cws"), st("chs"))
```

```python
import functools
import math

import jax
import jax.numpy as jnp
from jax import lax
from jax.experimental import pallas as pl
from jax.experimental.pallas import tpu as pltpu

F32 = jnp.float32
BF16 = jnp.bfloat16
I32 = jnp.int32

D_MODEL = 2048
BATCH = 4
SEQ = 2048
DEPTH = 4
DEC_BATCH = 8
DEC_SEQ = 64
PAST_LEN = 4096
CHUNK = 64
N_MIXERS = 3
ROPE_THETA = 10000.0
RMS_EPS = 1e-6
DH_A = 64
H_A = 16
SUBLN_EPS = 1e-5
HD_B = 128
H_B = 16
KV_B = 4
H_IDX = 16
D_IDX = 64
TOPK_MAX = 256
HS_C = 64
H_C = 32
R_DECAY = 96
R_ICLR = 96
GN_EPS = 64e-5

N_PROMPT = BATCH * SEQ
N_SAMPLE = DEC_BATCH * DEC_SEQ
M_ROWS = N_PROMPT + N_SAMPLE
LANES = 128
VMEM_LIMIT = 56 * 1024 * 1024
PROJ_TM = M_ROWS // 8
OG_DTYPE = jnp.bfloat16
NEG_INF = float("-inf")
INT_MIN = -2 ** 31
F32_MIN_NORMAL = float.fromhex("0x1p-126")
F32_LOWEST = -float.fromhex("0x1.fffffep127")

B_Q0, B_K0, B_V0, B_QI0, B_G0, B_KI0 = 0, 2048, 2560, 3072, 4096, 6144
B_COLS = 6272

assert PAST_LEN % CHUNK == 0 and DEC_SEQ == CHUNK
LOG2_CHUNK = CHUNK.bit_length() - 1
assert 1 << LOG2_CHUNK == CHUNK


def _chunk_of(pos):
    return jnp.right_shift(pos, LOG2_CHUNK)


def _cparams(sem):
    return pltpu.CompilerParams(dimension_semantics=sem, vmem_limit_bytes=VMEM_LIMIT)


def _sigmoid(x):
    return 1.0 / (1.0 + jnp.exp(-x))


def _silu(x):
    return x * _sigmoid(x)


def _rms(x, g):
    return x * lax.rsqrt(jnp.mean(x * x, axis=-1, keepdims=True) + RMS_EPS) * g


def _dot_nt(a, b, precision=None):
    return lax.dot_general(a, b, (((1,), (1,)), ((), ())), precision=precision,
                           preferred_element_type=F32)


def _norm_proj_body(x_ref, g_ref, w_ref, o_ref, h_scr):
    @pl.when(pl.program_id(1) == 0)
    def _():
        h_scr[...] = _rms(x_ref[...], g_ref[...]).astype(BF16)
    o_ref[...] = jnp.dot(h_scr[...], w_ref[...], preferred_element_type=F32)


def _norm_proj(x, g, w, tm, tn):
    m, d = x.shape
    n = w.shape[1]
    return pl.pallas_call(
        _norm_proj_body,
        grid=(m // tm, n // tn),
        in_specs=[pl.BlockSpec((tm, d), lambda i, j: (i, 0)),
                  pl.BlockSpec((1, d), lambda i, j: (0, 0)),
                  pl.BlockSpec((d, tn), lambda i, j: (0, j))],
        out_specs=pl.BlockSpec((tm, tn), lambda i, j: (i, j)),
        out_shape=jax.ShapeDtypeStruct((m, n), F32),
        scratch_shapes=[pltpu.VMEM((tm, d), BF16)],
        compiler_params=_cparams(("parallel", "arbitrary")),
        name="norm_proj",
    )(x, g.reshape(1, d), w)


def _proj_out_body(ogp_ref, ogs_ref, x_ref, w_ref, o_ref, *, n_prompt_tiles):
    def emit(og_ref):
        o_ref[...] = x_ref[...] + jnp.dot(og_ref[...].astype(BF16), w_ref[...],
                                          preferred_element_type=F32)

    _when_group(n_prompt_tiles, emit, (ogp_ref,), (ogs_ref,))


def _proj_out_final_body(ogp_ref, ogs_ref, x_ref, w_ref, g_ref, yp_ref, ys_ref, *, n_prompt_tiles):
    def emit(og_ref, y_ref):
        xn = x_ref[...] + jnp.dot(og_ref[...].astype(BF16), w_ref[...], preferred_element_type=F32)
        y_ref[...] = _rms(xn, g_ref[...])

    _when_group(n_prompt_tiles, emit, (ogp_ref, yp_ref), (ogs_ref, ys_ref))


def _proj_out(og_prompt, og_sample, x, w, final_g=None, tm=512):
    m, d = x.shape
    row = pl.BlockSpec((tm, d), lambda i: (i, 0))
    full = pl.BlockSpec((d, d), lambda i: (0, 0))
    in_specs = _group_specs(tm, d) + [row, full]
    if final_g is None:
        body, extra, out_specs = _proj_out_body, (), row
        out_shape = jax.ShapeDtypeStruct((m, d), F32)
    else:
        body, extra = _proj_out_final_body, (final_g.reshape(1, d),)
        in_specs.append(pl.BlockSpec((1, d), lambda i: (0, 0)))
        out_specs, out_shape = _group_specs(tm, d), _group_shapes(d)
    return pl.pallas_call(
        functools.partial(body, n_prompt_tiles=N_PROMPT // tm),
        grid=(m // tm,),
        in_specs=in_specs,
        out_specs=out_specs,
        out_shape=out_shape,
        compiler_params=_cparams(("arbitrary",)),
        name="proj_out",
    )(og_prompt, og_sample, x, w, *extra)


def _rope_tables():
    pos = jnp.concatenate([jnp.tile(jnp.arange(SEQ), BATCH),
                           jnp.tile(PAST_LEN + jnp.arange(DEC_SEQ), DEC_BATCH)]).astype(F32)

    def table(dh, reps):
        half = dh // 2
        inv = jnp.power(ROPE_THETA, -jnp.arange(half, dtype=F32) * (2.0 / dh))
        ang = pos[:, None] * inv[None, :]
        cos, sin = jnp.cos(ang), jnp.sin(ang)
        return (jnp.tile(jnp.concatenate([cos, cos], axis=1), (1, reps)),
                jnp.tile(jnp.concatenate([-sin, sin], axis=1), (1, reps)))

    cos1, sn1 = table(64, 2)
    cos2, sn2 = table(128, 1)
    return cos1, sn1, cos2, sn2


def _rope64(x, cos, sn):
    lane = lax.broadcasted_iota(I32, x.shape, 1)
    partner = jnp.where((lane & 63) < 32, pltpu.roll(x, 96, 1), pltpu.roll(x, 32, 1))
    return x * cos + partner * sn


def _rope128(x, cos, sn):
    return x * cos + pltpu.roll(x, 64, 1) * sn


def _group_specs(tm, width):
    npt = N_PROMPT // tm
    return [pl.BlockSpec((tm, width), lambda i: (jnp.minimum(i, npt - 1), 0)),
            pl.BlockSpec((tm, width), lambda i: (jnp.maximum(i - npt, 0), 0))]


def _group_shapes(width):
    return [jax.ShapeDtypeStruct((N_PROMPT, width), F32), jax.ShapeDtypeStruct((N_SAMPLE, width), F32)]


def _when_group(n_prompt_tiles, emit, prompt_refs, sample_refs):
    is_prompt = pl.program_id(0) < n_prompt_tiles
    pl.when(is_prompt)(lambda: emit(*prompt_refs))
    pl.when(jnp.logical_not(is_prompt))(lambda: emit(*sample_refs))


def _rope_a_body(q_ref, k_ref, v_ref, cos_ref, sn_ref, qo_ref, kp_ref, ks_ref, vp_ref, vs_ref,
                 *, n_prompt_tiles):
    cos, sn = cos_ref[...], sn_ref[...]
    hsl = [slice(h * LANES, (h + 1) * LANES) for h in range(H_A)]
    for sl in hsl:
        qo_ref[:, sl] = _rope64(q_ref[:, sl], cos, sn) * (DH_A ** -0.5)

    def emit(ko_ref, vo_ref):
        for sl in hsl:
            ko_ref[:, sl] = _rope64(k_ref[:, sl], cos, sn)
        vo_ref[...] = v_ref[...]

    _when_group(n_prompt_tiles, emit, (kp_ref, vp_ref), (ks_ref, vs_ref))


def _rope_a(qkvg, cos1, sn1, tm=256):
    m = qkvg.shape[0]
    blk = lambda c: pl.BlockSpec((tm, D_MODEL), lambda i, c=c: (i, c))
    tab = pl.BlockSpec((tm, LANES), lambda i: (i, 0))
    return pl.pallas_call(
        functools.partial(_rope_a_body, n_prompt_tiles=N_PROMPT // tm),
        grid=(m // tm,),
        in_specs=[blk(0), blk(1), blk(2), tab, tab],
        out_specs=[blk(0)] + _group_specs(tm, D_MODEL) * 2,
        out_shape=[jax.ShapeDtypeStruct((m, D_MODEL), F32)] + _group_shapes(D_MODEL) * 2,
        compiler_params=_cparams(("arbitrary",)),
        name="rope_a",
    )(qkvg, qkvg, qkvg, cos1, sn1)


def _stack_maps_t(q):
    lane = lax.broadcasted_iota(I32, q.shape, 1)
    qs = jnp.concatenate([jnp.where(lane < DH_A, q, 0.0), jnp.where(lane >= DH_A, q, 0.0)], axis=0)
    return qs.T.astype(BF16)


def _flash_t_step(s, vt, carry):
    m, l, acc = carry
    m_new = jnp.maximum(m, jnp.max(s, axis=0, keepdims=True))
    alpha = jnp.exp(m - m_new)
    p = jnp.exp(s - m_new)
    l = alpha * l + jnp.sum(p, axis=0, keepdims=True)
    pv = lax.dot_general(vt, p.astype(BF16), DIMS_TN, preferred_element_type=F32)
    return m_new, l, alpha * acc + pv


def _flash_t_init(n):
    return (jnp.full((1, n), -1e30, F32), jnp.zeros((1, n), F32), jnp.zeros((LANES, n), F32))


def _diff_epilogue(l, acc, tq, lam_ref, sg, g, lam_init):
    o = (acc / l).T
    lp = lam_ref[...]
    lam = (jnp.exp(jnp.sum(lp[0:1] * lp[1:2], axis=-1, keepdims=True))
           - jnp.exp(jnp.sum(lp[2:3] * lp[3:4], axis=-1, keepdims=True)) + lam_init)
    o = o[:tq] - lam * o[tq:]
    o = o * lax.rsqrt(jnp.mean(o * o, axis=-1, keepdims=True) + SUBLN_EPS) * sg
    o = o * (1.0 - lam_init)
    return o * _silu(g)


A_HP = 4


def _attn_a_prompt_body(q_ref, k_ref, v_ref, g_ref, lam_ref, sg_ref, o_ref, kb_scr, vb_scr,
                        *, tq, lam_init):
    qi = pl.program_id(2)

    @pl.when(qi == 0)
    def _():
        kb_scr[...] = k_ref[...].astype(BF16)
        vb_scr[...] = v_ref[...].astype(BF16)

    hsl = [slice(h * LANES, (h + 1) * LANES) for h in range(A_HP)]
    qst = [_stack_maps_t(q_ref[:, sl]) for sl in hsl]
    n = 2 * tq
    krow = lax.broadcasted_iota(I32, (tq, n), 0)
    qcol = lax.broadcasted_iota(I32, (tq, n), 1) & (tq - 1)
    diag_mask = _chunk_of(krow) <= _chunk_of(qcol)

    def tile(kt, carries, masked):
        off = pl.multiple_of(kt * tq, tq)
        ss = [jnp.dot(kb_scr[pl.ds(off, tq), sl], qst[h], preferred_element_type=F32)
              for h, sl in enumerate(hsl)]
        out = []
        for h, sl in enumerate(hsl):
            s = jnp.where(diag_mask, ss[h], NEG_INF) if masked else ss[h]
            out.append(_flash_t_step(s, vb_scr[pl.ds(off, tq), sl], carries[h]))
        return tuple(out)

    carries = lax.fori_loop(0, qi, lambda kt, c: tile(kt, c, False),
                            tuple(_flash_t_init(n) for _ in hsl))
    carries = tile(qi, carries, True)
    for h, sl in enumerate(hsl):
        _, l, acc = carries[h]
        o_ref[:, sl] = _diff_epilogue(l, acc, tq, lam_ref, sg_ref[...], g_ref[:, sl],
                                      lam_init).astype(o_ref.dtype)


def _attn_a_prompt(q_rot, k_p, v_p, qkvg, lam_p, subln_g, lam_init, tq=256):
    nq = SEQ // tq
    w = A_HP * LANES
    nh = H_A // A_HP
    return pl.pallas_call(
        functools.partial(_attn_a_prompt_body, tq=tq, lam_init=lam_init),
        grid=(BATCH, nh, nq),
        in_specs=[pl.BlockSpec((tq, w), lambda b, h, i: (b * nq + i, h)),
                  pl.BlockSpec((SEQ, w), lambda b, h, i: (b, h)),
                  pl.BlockSpec((SEQ, w), lambda b, h, i: (b, h)),
                  pl.BlockSpec((tq, w), lambda b, h, i: (b * nq + i, 3 * nh + h)),
                  pl.BlockSpec((4, DH_A), lambda b, h, i: (0, 0)),
                  pl.BlockSpec((1, LANES), lambda b, h, i: (0, 0))],
        out_specs=pl.BlockSpec((tq, w), lambda b, h, i: (b * nq + i, h)),
        out_shape=jax.ShapeDtypeStruct((N_PROMPT, D_MODEL), OG_DTYPE),
        scratch_shapes=[pltpu.VMEM((SEQ, w), BF16), pltpu.VMEM((SEQ, w), BF16)],
        compiler_params=_cparams(("parallel", "parallel", "arbitrary")),
        name="attn_a_prompt",
    )(q_rot, k_p, v_p, qkvg, lam_p, subln_g.reshape(1, LANES))


def _attn_a_sample_body(q_ref, kp_ref, vp_ref, k_ref, v_ref, g_ref, lam_ref, sg_ref,
                        o_ref, qst_scr, ml_scr, acc_scr, *, tk, lam_init):
    kt = pl.program_id(1)
    tq = DEC_SEQ
    hsl = [slice(h * LANES, (h + 1) * LANES) for h in range(H_A)]

    @pl.when(kt == 0)
    def _():
        for h, sl in enumerate(hsl):
            qst_scr[h] = _stack_maps_t(q_ref[:, sl])
        m0, l0, acc0 = _flash_t_init(2 * tq)
        for h in range(H_A):
            ml_scr[h, 0:1, :] = m0
            ml_scr[h, 1:2, :] = l0
            acc_scr[h] = acc0

    def load(h):
        return ml_scr[h, 0:1, :], ml_scr[h, 1:2, :], acc_scr[h]

    def store(h, carry):
        ml_scr[h, 0:1, :], ml_scr[h, 1:2, :], acc_scr[h] = carry

    def head_group(i, _):
        hs = [A_HP * i + d for d in range(A_HP)]
        ss = [jnp.dot(kp_ref[pl.ds(h, tk, stride=H_A), :].astype(BF16), qst_scr[h],
                      preferred_element_type=F32) for h in hs]
        for h, s in zip(hs, ss):
            vb = vp_ref[pl.ds(h, tk, stride=H_A), :].astype(BF16)
            store(h, _flash_t_step(s, vb, load(h)))
        return 0

    lax.fori_loop(0, H_A // A_HP, head_group, 0)

    @pl.when(kt == pl.num_programs(1) - 1)
    def _():
        for h, sl in enumerate(hsl):
            s = jnp.dot(k_ref[:, sl].astype(BF16), qst_scr[h], preferred_element_type=F32)
            _, l, acc = _flash_t_step(s, v_ref[:, sl].astype(BF16), load(h))
            o_ref[:, sl] = _diff_epilogue(l, acc, tq, lam_ref, sg_ref[...], g_ref[:, sl],
                                          lam_init).astype(o_ref.dtype)


def _attn_a_sample(q_rot, k_s, v_s, qkvg, k_past, v_past, layer, lam_p, subln_g, lam_init, tk=512):
    r0 = N_PROMPT // DEC_SEQ
    nkt = PAST_LEN // tk
    new = lambda c: pl.BlockSpec((DEC_SEQ, D_MODEL), lambda b, t, c=c: (r0 + b, c))
    own = pl.BlockSpec((DEC_SEQ, D_MODEL), lambda b, t: (b, 0))
    past = pl.BlockSpec((tk * H_A, LANES), lambda b, t: ((layer * DEC_BATCH + b) * nkt + t, 0))
    return pl.pallas_call(
        functools.partial(_attn_a_sample_body, tk=tk, lam_init=lam_init),
        grid=(DEC_BATCH, nkt),
        in_specs=[new(0), past, past, own, own, new(3),
                  pl.BlockSpec((4, DH_A), lambda b, t: (0, 0)),
                  pl.BlockSpec((1, LANES), lambda b, t: (0, 0))],
        out_specs=own,
        out_shape=jax.ShapeDtypeStruct((N_SAMPLE, D_MODEL), OG_DTYPE),
        scratch_shapes=[pltpu.VMEM((H_A, LANES, 2 * DEC_SEQ), BF16),
                        pltpu.VMEM((H_A, 8, 2 * DEC_SEQ), F32),
                        pltpu.VMEM((H_A, LANES, 2 * DEC_SEQ), F32)],
        compiler_params=_cparams(("parallel", "arbitrary")),
        name="attn_a_sample",
    )(q_rot, k_past, v_past, k_s, v_s, qkvg, lam_p, subln_g.reshape(1, LANES))


def _rope_b_body(q_ref, k_ref, v_ref, qi_ref, sm_ref, c1_ref, s1_ref, c2_ref, s2_ref,
                 qo_ref, qio_ref, kid_ref, wi_ref, kp_ref, ks_ref, vp_ref, vs_ref, ip_ref, is_ref,
                 *, n_prompt_tiles):
    c1, s1, c2, s2 = c1_ref[...], s1_ref[...], c2_ref[...], s2_ref[...]
    for h in range(H_B):
        sl = slice(h * LANES, (h + 1) * LANES)
        qo_ref[:, sl] = _rope128(q_ref[:, sl], c2, s2) * (HD_B ** -0.5)
    for h in range(H_IDX * D_IDX // LANES):
        sl = slice(h * LANES, (h + 1) * LANES)
        qio_ref[:, sl] = _rope64(qi_ref[:, sl], c1, s1) * (D_IDX ** -0.5)
    sm = sm_ref[...]
    lane = lax.broadcasted_iota(I32, sm.shape, 1)
    kr = _rope64(sm, c1, s1)
    kid_ref[...] = jnp.where(lane < D_IDX, kr, pltpu.roll(kr, D_IDX, 1))
    wi_ref[...] = pltpu.roll(sm, D_IDX, 1) * (H_IDX ** -0.5)

    def emit(ko_ref, vo_ref, io_ref):
        for h in range(KV_B):
            sl = slice(h * LANES, (h + 1) * LANES)
            ko_ref[:, sl] = _rope128(k_ref[:, sl], c2, s2)
        vo_ref[...] = v_ref[...]
        io_ref[...] = kr[:, :D_IDX]

    _when_group(n_prompt_tiles, emit, (kp_ref, vp_ref, ip_ref), (ks_ref, vs_ref, is_ref))


def _rope_b(proj, cos1, sn1, cos2, sn2, tm=256):
    m = proj.shape[0]
    tab = pl.BlockSpec((tm, LANES), lambda i: (i, 0))
    blk = lambda w, c: pl.BlockSpec((tm, w), lambda i, c=c: (i, c))
    widths = (H_B * HD_B, H_IDX * D_IDX, LANES, LANES)
    kvw = KV_B * HD_B
    return pl.pallas_call(
        functools.partial(_rope_b_body, n_prompt_tiles=N_PROMPT // tm),
        grid=(m // tm,),
        in_specs=[blk(2048, 0), blk(kvw, B_K0 // kvw), blk(kvw, B_V0 // kvw),
                  blk(1024, B_QI0 // 1024), blk(LANES, B_KI0 // LANES), tab, tab, tab, tab],
        out_specs=([blk(w, 0) for w in widths] + _group_specs(tm, kvw) * 2
                   + _group_specs(tm, D_IDX)),
        out_shape=([jax.ShapeDtypeStruct((m, w), F32) for w in widths] + _group_shapes(kvw) * 2
                   + _group_shapes(D_IDX)),
        compiler_params=_cparams(("arbitrary",)),
        name="rope_b",
    )(proj, proj, proj, proj, proj, cos1, sn1, cos2, sn2)


POS_BITS = 13
DSA_BUCKET = 512


def _count(pred):
    return jnp.sum(jnp.where(pred, 1.0, 0.0), axis=-1, keepdims=True)


def _select_topk(scores, poss, n_sel):
    tq = scores[0].shape[0]
    n_sel = float(n_sel)

    def key_to_float(key):
        return lax.bitcast_convert_type(jnp.where(key < 0, key ^ 0x7FFFFFFF, key), F32)

    def thr_step(i, t):
        cand_bits = t | lax.shift_left(jnp.int32(1), 31 - i)
        cand = key_to_float(cand_bits ^ INT_MIN)
        cnt = sum(_count(s >= cand) for s in scores)
        return jnp.where(cnt >= n_sel, cand_bits, t)

    t = lax.fori_loop(0, 32, thr_step, jnp.zeros((tq, 1), I32))
    thr_key = t ^ INT_MIN
    thr, above = key_to_float(thr_key), key_to_float(thr_key + 1)
    above = jnp.where(thr == 0.0, F32_MIN_NORMAL, above)
    few = sum(_count(s > NEG_INF) for s in scores) <= n_sel
    thr = jnp.where(few, F32_LOWEST, thr)
    above = jnp.where(few, F32_LOWEST, above)
    gts = [s >= above for s in scores]
    eqs = [(s >= thr) & jnp.logical_not(g) for s, g in zip(scores, gts)]
    need = n_sel - sum(_count(g) for g in gts)
    tied = [jnp.where(e, 1.0, 0.0) for e in eqs]
    n_tied = sum(jnp.sum(e, axis=-1, keepdims=True) for e in tied)

    def tie_step(i, j):
        cand = j | lax.shift_left(jnp.int32(1), POS_BITS - 1 - i)
        cnt = sum(jnp.sum(jnp.where(p < cand, e, 0.0), axis=-1, keepdims=True)
                  for e, p in zip(tied, poss))
        return jnp.where(cnt < need, cand, j)

    j = lax.cond(jnp.max(n_tied - need) > 0.0,
                 lambda: lax.fori_loop(0, POS_BITS, tie_step, jnp.zeros((tq, 1), I32)),
                 lambda: jnp.full((tq, 1), (1 << POS_BITS) - 1, I32))
    return [g | (e & (p <= j)) for g, e, p in zip(gts, eqs, poss)]


def _dsa_body(*refs, tq, seg_lens, seg_pos0, q_pos0, causal, n_sel, bucket):
    nseg = len(seg_lens)
    n_in = 4 + 3 * nseg
    assert max(seg_pos0[i] + seg_lens[i] for i in range(nseg)) <= 1 << POS_BITS
    if bucket is None:
        _dsa_run(refs, tq, seg_lens, seg_pos0, q_pos0, causal, n_sel, n_in)
        return
    assert causal and nseg == 1 and seg_pos0[0] == 0 and q_pos0 == 0 and bucket % tq == 0
    last = (pl.program_id(1) * tq) // bucket
    for v in range(seg_lens[0] // bucket):
        pl.when(last == v)(functools.partial(_dsa_run, refs, tq, ((v + 1) * bucket,), seg_pos0,
                                             q_pos0, causal, n_sel, n_in))


def _dsa_run(refs, tq, seg_lens, seg_pos0, q_pos0, causal, n_sel, n_in):
    nseg = len(seg_lens)
    qi_ref, wi_ref, q_ref, g_ref = refs[:4]
    kid_refs = refs[4:4 + nseg]
    k_refs = refs[4 + nseg:4 + 2 * nseg]
    v_refs = refs[4 + 2 * nseg:4 + 3 * nseg]
    o_ref = refs[n_in]
    bias_refs = refs[n_in + 1:]
    n_rep = H_B // KV_B

    @pl.when(pl.program_id(2) == 0)
    def _():
        wi = wi_ref[...]
        qpos = q_pos0 + pl.program_id(1) * tq
        scores, poss, adms = [], [], []
        for si in range(nseg):
            sl = seg_lens[si]
            kid = kid_refs[si][0:sl, :].astype(BF16)
            sc = jnp.zeros((tq, sl), F32)
            for p in range(H_IDX // 2):
                qp = qi_ref[:, p * LANES:(p + 1) * LANES]
                lane = lax.broadcasted_iota(I32, qp.shape, 1)
                qs = jnp.concatenate([jnp.where(lane < D_IDX, qp, 0.0),
                                      jnp.where(lane >= D_IDX, qp, 0.0)], axis=0).astype(BF16)
                lg = jnp.maximum(_dot_nt(qs, kid), 0.0)
                sc = sc + wi[:, 2 * p:2 * p + 1] * lg[:tq] + wi[:, 2 * p + 1:2 * p + 2] * lg[tq:]
            kpos = seg_pos0[si] + lax.broadcasted_iota(I32, (tq, sl), 1)
            if causal:
                qrow = qpos + lax.broadcasted_iota(I32, (tq, sl), 0)
                adm = _chunk_of(kpos) <= _chunk_of(qrow)
                sc = jnp.where(adm, sc, NEG_INF)
            else:
                adm = None
            scores.append(sc)
            poss.append(kpos)
            adms.append(adm)
        sels = _select_topk(scores, poss, n_sel)
        for si in range(nseg):
            bias = jnp.where(sels[si], 0.0, NEG_INF)
            if adms[si] is not None:
                bias = jnp.where(adms[si], bias, NEG_INF)
            bias_refs[si][:, 0:seg_lens[si]] = bias

    q = q_ref[...]
    qg = jnp.concatenate([q[:, r * LANES:(r + 1) * LANES] for r in range(n_rep)],
                         axis=0).astype(BF16)
    ss = []
    for si in range(nseg):
        b = bias_refs[si][:, 0:seg_lens[si]]
        s = _dot_nt(qg, k_refs[si][0:seg_lens[si], :].astype(BF16))
        ss.append(s + jnp.concatenate([b] * n_rep, axis=0))
    m = functools.reduce(jnp.maximum, [jnp.max(s, axis=-1, keepdims=True) for s in ss])
    l = 0.0
    acc = 0.0
    for si in range(nseg):
        p = jnp.exp(ss[si] - m)
        l = l + jnp.sum(p, axis=-1, keepdims=True)
        acc = acc + jnp.dot(p.astype(BF16), v_refs[si][0:seg_lens[si], :].astype(BF16),
                            preferred_element_type=F32)
    o = acc / l
    o = jnp.concatenate([o[r * tq:(r + 1) * tq] for r in range(n_rep)], axis=1)
    o_ref[...] = (o * _silu(g_ref[...])).astype(o_ref.dtype)


def _dsa_prompt(q_rot, k_p, v_p, qi_rot, kid, wi, proj, tq=256):
    nq = SEQ // tq
    gw = KV_B * HD_B
    row = lambda w, c0: pl.BlockSpec((tq, w), lambda b, i, g, c0=c0: (b * nq + i, c0 + g))
    row0 = lambda w: pl.BlockSpec((tq, w), lambda b, i, g: (b * nq + i, 0))
    seq = lambda c0: pl.BlockSpec((SEQ, LANES), lambda b, i, g, c0=c0: (b, c0 + g))
    n_sel = min(TOPK_MAX, SEQ // 4)
    return pl.pallas_call(
        functools.partial(_dsa_body, tq=tq, seg_lens=(SEQ,), seg_pos0=(0,), q_pos0=0,
                          causal=True, n_sel=n_sel, bucket=DSA_BUCKET),
        grid=(BATCH, nq, KV_B),
        in_specs=[row0(H_IDX * D_IDX), row0(LANES), row(gw, 0), row(gw, B_G0 // gw),
                  pl.BlockSpec((SEQ, LANES), lambda b, i, g: (b, 0)),
                  seq(0), seq(0)],
        out_specs=row(gw, 0),
        out_shape=jax.ShapeDtypeStruct((N_PROMPT, D_MODEL), OG_DTYPE),
        scratch_shapes=[pltpu.VMEM((tq, SEQ), F32)],
        compiler_params=_cparams(("parallel", "parallel", "arbitrary")),
        name="dsa_prompt",
    )(qi_rot, wi, q_rot, proj, kid, k_p, v_p)


def _dsa_sample(q_rot, k_s, v_s, qi_rot, kid, wi, proj, k_past, v_past, kid_past, layer):
    tq = DEC_SEQ
    r0 = N_PROMPT // tq
    gw = KV_B * HD_B
    row = lambda w, c0: pl.BlockSpec((tq, w), lambda b, i, g, c0=c0: (r0 + b, c0 + g))
    row0 = lambda w: pl.BlockSpec((tq, w), lambda b, i, g: (r0 + b, 0))
    past = pl.BlockSpec((PAST_LEN, LANES), lambda b, i, g: (layer * DEC_BATCH + b, g))
    own = pl.BlockSpec((tq, LANES), lambda b, i, g: (b, g))
    s_all = PAST_LEN + DEC_SEQ
    n_sel = min(TOPK_MAX, s_all // 4)
    return pl.pallas_call(
        functools.partial(_dsa_body, tq=tq, seg_lens=(PAST_LEN, DEC_SEQ),
                          seg_pos0=(0, PAST_LEN), q_pos0=PAST_LEN, causal=False, n_sel=n_sel,
                          bucket=None),
        grid=(DEC_BATCH, 1, KV_B),
        in_specs=[row0(H_IDX * D_IDX), row0(LANES), row(gw, 0), row(gw, B_G0 // gw),
                  pl.BlockSpec((PAST_LEN, LANES), lambda b, i, g: (layer * DEC_BATCH + b, 0)),
                  row0(LANES),
                  past, own,
                  past, own],
        out_specs=pl.BlockSpec((tq, gw), lambda b, i, g: (b, g)),
        out_shape=jax.ShapeDtypeStruct((N_SAMPLE, D_MODEL), OG_DTYPE),
        scratch_shapes=[pltpu.VMEM((tq, PAST_LEN), F32), pltpu.VMEM((tq, DEC_SEQ), F32)],
        compiler_params=_cparams(("parallel", "arbitrary", "arbitrary")),
        name="dsa_sample",
    )(qi_rot, wi, q_rot, proj, kid_past, kid, k_past, k_s, v_past, v_s)


C_TM = 64
C_TILES_PER_SEQ = SEQ // C_TM
C_PROMPT_TILES = N_PROMPT // C_TM


def _c_seq_id(i):
    return jnp.where(i < C_PROMPT_TILES, i // C_TILES_PER_SEQ, i - C_PROMPT_TILES + BATCH)


def _c_prep_body(x_ref, g_ref, mu_ref, sh_ref, l_ref, hl_ref, carry):
    i = pl.program_id(0)

    @pl.when(i == 0)
    def _():
        carry[...] = jnp.zeros_like(carry)

    h = _rms(x_ref[...], g_ref[...])
    start = jnp.logical_or(i >= C_PROMPT_TILES, i % C_TILES_PER_SEQ == 0)
    first = jnp.where(start, sh_ref[0], carry[...])
    row = lax.broadcasted_iota(I32, h.shape, 0)
    prev = jnp.where(row == 0, first, pltpu.roll(h, 1, 0))
    last = h[C_TM - 1:C_TM, :]
    carry[...] = last
    hl_ref[0] = last
    d = prev - h
    for n in range(6):
        l_ref[n] = (h + d * mu_ref[n:n + 1, :]).astype(BF16)


def _c_prep(x, g, mu, shift0):
    m, d = x.shape
    nseq = shift0.shape[0]
    return pl.pallas_call(
        _c_prep_body,
        grid=(m // C_TM,),
        in_specs=[pl.BlockSpec((C_TM, d), lambda i: (i, 0)),
                  pl.BlockSpec((1, d), lambda i: (0, 0)),
                  pl.BlockSpec((6, d), lambda i: (0, 0)),
                  pl.BlockSpec((1, 1, d), lambda i: (_c_seq_id(i), 0, 0))],
        out_specs=[pl.BlockSpec((6, C_TM, d), lambda i: (0, i, 0)),
                   pl.BlockSpec((1, 1, d), lambda i: (_c_seq_id(i), 0, 0))],
        out_shape=[jax.ShapeDtypeStruct((6, m, d), BF16),
                   jax.ShapeDtypeStruct((nseq, 1, d), F32)],
        scratch_shapes=[pltpu.VMEM((1, d), F32)],
        compiler_params=_cparams(("arbitrary",)),
        name="c_prep",
    )(x, g.reshape(1, d), mu, shift0.reshape(nseq, 1, d))


def _bmm_body(l_ref, w_ref, o_ref):
    o_ref[0] = jnp.dot(l_ref[0], w_ref[0], preferred_element_type=F32)


def _c_bmm(lerp, w, tm=PROJ_TM, tn=1024):
    _, m, d = lerp.shape
    nb, _, n = w.shape
    return pl.pallas_call(
        _bmm_body,
        grid=(nb, m // tm, n // tn),
        in_specs=[pl.BlockSpec((1, tm, d), lambda b, i, j: (b, i, 0)),
                  pl.BlockSpec((1, d, tn), lambda b, i, j: (b, 0, j))],
        out_specs=pl.BlockSpec((1, tm, tn), lambda b, i, j: (b, i, j)),
        out_shape=jax.ShapeDtypeStruct((nb, m, n), F32),
        compiler_params=_cparams(("parallel", "parallel", "arbitrary")),
        name="c_bmm",
    )(lerp, w)


def _c_lora_body(l4_ref, l5_ref, wla_ref, wlb_ref, ala_ref, alb_ref, w0_ref, a0_ref,
                 wl_ref, al_ref):
    tw = jnp.tanh(jnp.dot(l4_ref[0], wla_ref[...], preferred_element_type=F32))
    wl_ref[...] = w0_ref[...] + jnp.dot(tw.astype(BF16), wlb_ref[...],
                                        preferred_element_type=F32)
    ta = jnp.dot(l5_ref[0], ala_ref[...], preferred_element_type=F32)
    al_ref[...] = a0_ref[...] + jnp.dot(ta.astype(BF16), alb_ref[...],
                                        preferred_element_type=F32)


def _c_lora(lerp, w_la, w_lb, a_la, a_lb, w0, a0, tm=512):
    _, m, d = lerp.shape
    pad_in = lambda w: jnp.pad(w, ((0, 0), (0, LANES - w.shape[1]))).astype(BF16)
    pad_out = lambda w: jnp.pad(w, ((0, LANES - w.shape[0]), (0, 0))).astype(BF16)
    lin = lambda n: pl.BlockSpec((1, tm, d), lambda i, n=n: (n, i, 0))
    win = pl.BlockSpec((d, LANES), lambda i: (0, 0))
    wout = pl.BlockSpec((LANES, d), lambda i: (0, 0))
    vec = pl.BlockSpec((1, d), lambda i: (0, 0))
    out = pl.BlockSpec((tm, d), lambda i: (i, 0))
    return pl.pallas_call(
        _c_lora_body,
        grid=(m // tm,),
        in_specs=[lin(4), lin(5), win, wout, win, wout, vec, vec],
        out_specs=[out, out],
        out_shape=[jax.ShapeDtypeStruct((m, d), F32)] * 2,
        compiler_params=_cparams(("parallel",)),
        name="c_lora",
    )(lerp, lerp, pad_in(w_la), pad_out(w_lb), pad_in(a_la), pad_out(a_lb),
      w0.reshape(1, d), a0.reshape(1, d))


C_HB = 16
DIMS_NN = (((1,), (0,)), ((), ()))
DIMS_NT = (((1,), (1,)), ((), ()))
DIMS_TN = (((0,), (0,)), ((), ()))


def _dot1(a, b, dims):
    return lax.dot_general(a.astype(BF16), b.astype(BF16), dims, preferred_element_type=F32)


def _dot3(a, b, dims):
    (ca,), (cb,) = dims[0]
    ah = a.astype(BF16).astype(F32)
    bh = b.astype(BF16).astype(F32)
    sa = jnp.concatenate([ah, a - ah, ah], axis=ca).astype(BF16)
    sb = jnp.concatenate([bh, bh, b - bh], axis=cb).astype(BF16)
    return lax.dot_general(sa, sb, dims, preferred_element_type=F32)


def _cumsum_rows(x):
    row = lax.broadcasted_iota(I32, x.shape, 0)
    sh = 1
    while sh < x.shape[0]:
        x = x + jnp.where(row >= sh, pltpu.roll(x, sh, 0), 0.0)
        sh *= 2
    return x


def _rwkv_body(*refs, has_init):
    if has_init:
        (r_ref, k_ref, v_ref, g_ref, wl_ref, al_ref, kk_ref, ka_ref, rk_ref, lnw_ref, lnb_ref,
         s0_ref, og_ref, so_ref, s_scr) = refs
    else:
        (r_ref, k_ref, v_ref, g_ref, wl_ref, al_ref, kk_ref, ka_ref, rk_ref, lnw_ref, lnb_ref,
         og_ref, so_ref, s_scr) = refs
    c = pl.program_id(2)
    n = CHUNK
    hs = HS_C

    @pl.when(c == 0)
    def _():
        if has_init:
            s_scr[...] = s0_ref[0]
        else:
            s_scr[...] = jnp.zeros_like(s_scr)

    r, k, v, g = r_ref[0], k_ref[0], v_ref[0], g_ref[0]
    logw = -math.exp(-0.5) * _sigmoid(wl_ref[...])
    a = _sigmoid(al_ref[...])
    kkr = k * kk_ref[...]
    k2 = k * (1.0 + (a - 1.0) * ka_ref[...])
    bonus_in = r * k2 * rk_ref[...]

    cs_all = _cumsum_rows(logw)
    ec_all, eci_all, ecp_all = jnp.exp(cs_all), jnp.exp(-cs_all), jnp.exp(cs_all - logw)

    ti = lax.broadcasted_iota(I32, (n, n), 0)
    tj = lax.broadcasted_iota(I32, (n, n), 1)
    eye = jnp.where(tj == ti, 1.0, 0.0)
    gi = lax.broadcasted_iota(I32, (2 * n, 2 * n), 0)
    gj = lax.broadcasted_iota(I32, (2 * n, 2 * n), 1) & (n - 1)
    gmask = ((gi < n) & (gj < gi)) | ((gi >= n) & (gj <= gi - n))
    lvl_masks = []
    bs = 1
    while bs < n:
        sh = bs.bit_length()
        lvl_masks.append((jnp.right_shift(ti, sh) == jnp.right_shift(tj, sh))
                         & ((ti & bs) != 0) & ((tj & bs) == 0))
        bs *= 2

    heads = range(C_HB)
    sls = [slice(j * hs, (j + 1) * hs) for j in heads]
    vs = [v[:, sl] for sl in sls]
    zero = jnp.zeros_like(vs[0])
    gw = 4 * hs
    ei = lax.broadcasted_iota(I32, (gw, gw), 0)
    ej = lax.broadcasted_iota(I32, (gw, gw), 1)
    lg = hs.bit_length() - 1
    seg = jnp.where(jnp.right_shift(ei, lg) == jnp.right_shift(ej, lg), 1.0, 0.0).astype(BF16)
    sq = kkr * kkr
    sq_hi = sq.astype(BF16)
    sq_lo = (sq - sq_hi.astype(F32)).astype(BF16)
    ssq = jnp.concatenate(
        [jnp.dot(sq_hi[:, c:c + gw], seg, preferred_element_type=F32)
         + jnp.dot(sq_lo[:, c:c + gw], seg, preferred_element_type=F32)
         for c in range(0, C_HB * hs, gw)], axis=1)
    kkn = kkr / jnp.maximum(jnp.sqrt(ssq), 1e-12)
    p_all, q_all = kkn * ecp_all, kkn * a * eci_all
    kt_all, rt_all = k2 * eci_all, r * ec_all
    p_, q_ = [p_all[:, sl] for sl in sls], [q_all[:, sl] for sl in sls]
    kt, rt = [kt_all[:, sl] for sl in sls], [rt_all[:, sl] for sl in sls]
    gm = [jnp.where(gmask, _dot1(jnp.concatenate([p_[j], rt[j]], axis=0),
                                 jnp.concatenate([q_[j], kt[j]], axis=0), DIMS_NT), 0.0)
          for j in heads]
    av = [_dot1(gm[j][:n], jnp.concatenate([zero, vs[j]], axis=0), DIMS_NN)
          for j in heads]
    a_qp = [gm[j][:n, :n] for j in heads]
    x = [eye - jnp.where(lvl_masks[0], a_qp[j], 0.0) for j in heads]
    for msk in lvl_masks[1:]:
        ax = [_dot1(jnp.where(msk, a_qp[j], 0.0), x[j], DIMS_NN) for j in heads]
        x = [x[j] - _dot1(x[j], ax[j], DIMS_NN) for j in heads]
    xpw = [_dot1(x[j], jnp.concatenate([p_[j], av[j]], axis=1), DIMS_NN)
           for j in heads]
    low = [jnp.concatenate([zero, vs[j]], axis=1) for j in heads]
    tb = [_dot3(jnp.concatenate([xpw[j], low[j]], axis=0),
                jnp.concatenate([-q_[j], kt[j]], axis=0), DIMS_TN) for j in heads]
    ro = [_dot1(gm[j][n:], jnp.concatenate([-xpw[j], low[j]], axis=0), DIMS_NN) for j in heads]
    s0 = [s_scr[j] for j in heads]
    o = [_dot1(rt[j] + ro[j][:, :hs], s0[j], DIMS_NT) + ro[j][:, hs:] for j in heads]
    for j in heads:
        ecl = ec_all[n - 1:n, sls[j]]
        s_scr[j] = _dot3(s0[j], (eye + tb[j][:hs]) * ecl, DIMS_NN) + tb[j][hs:] * ecl
    outs = []
    for j in heads:
        mean = jnp.mean(o[j], axis=-1, keepdims=True)
        var = jnp.mean(jnp.square(o[j] - mean), axis=-1, keepdims=True)
        on = (o[j] - mean) * lax.rsqrt(var + GN_EPS) * lnw_ref[:, sls[j]] + lnb_ref[:, sls[j]]
        bonus = jnp.sum(bonus_in[:, sls[j]], axis=-1, keepdims=True) * vs[j]
        outs.append(on + bonus)
    og_ref[...] = (jnp.concatenate(outs, axis=1) * _silu(g)).astype(og_ref.dtype)

    @pl.when(c == pl.num_programs(2) - 1)
    def _():
        so_ref[0] = s_scr[...]


def _rwkv(rkvg, wl, al, k_k, k_a, r_k, ln_w, ln_b, s0, *, nseq, nchunk, row0):
    w = C_HB * HS_C
    tok = lambda n: pl.BlockSpec((1, CHUNK, w), lambda s, h, c, n=n: (n, row0 + s * nchunk + c, h))
    tok2 = pl.BlockSpec((CHUNK, w), lambda s, h, c: (row0 + s * nchunk + c, h))
    vec = pl.BlockSpec((1, w), lambda s, h, c: (0, h))
    st = pl.BlockSpec((1, C_HB, HS_C, HS_C), lambda s, h, c: (s, h, 0, 0))
    has_init = s0 is not None
    ins = [rkvg, rkvg, rkvg, rkvg, wl, al] + [p.reshape(1, D_MODEL) for p in (k_k, k_a, r_k, ln_w, ln_b)]
    specs = [tok(0), tok(1), tok(2), tok(3), tok2, tok2] + [vec] * 5
    if has_init:
        ins.append(s0)
        specs.append(st)
    return pl.pallas_call(
        functools.partial(_rwkv_body, has_init=has_init),
        grid=(nseq, H_C // C_HB, nchunk),
        in_specs=specs,
        out_specs=[pl.BlockSpec((CHUNK, w), lambda s, h, c: (s * nchunk + c, h)), st],
        out_shape=[jax.ShapeDtypeStruct((nseq * nchunk * CHUNK, D_MODEL), OG_DTYPE),
                   jax.ShapeDtypeStruct((nseq, H_C, HS_C, HS_C), F32)],
        scratch_shapes=[pltpu.VMEM((C_HB, HS_C, HS_C), F32)],
        compiler_params=_cparams(("parallel", "parallel", "arbitrary")),
        name="rwkv_scan",
    )(*ins)


def kernel(x_prompt, x_sample, cache_a_k, cache_a_v, cache_b_k, cache_b_v, cache_b_kidx, state_c_wkv, state_c_shift, norm_g, final_g, w_out, a_w_in, a_lam, a_subln_g, b_w_in, c_mu, c_w_rkvg, c_w0, c_w_la, c_w_lb, c_a0, c_a_la, c_a_lb, c_k_k, c_k_a, c_r_k, c_ln_w, c_ln_b):
    x = jnp.concatenate([x_prompt.reshape(N_PROMPT, D_MODEL),
                         x_sample.reshape(N_SAMPLE, D_MODEL)], axis=0)
    cos1, sn1, cos2, sn2 = _rope_tables()
    outs = {n: [] for n in ("akp", "avp", "aks", "avs", "bkp", "bvp", "bip", "bks", "bvs", "bis",
                            "cwp", "chp", "cws", "chs")}
    for i in range(DEPTH):
        kind, j = i % N_MIXERS, i // N_MIXERS
        if kind == 0:
            lam_init = 0.8 - 0.6 * math.exp(-0.3 * i)
            qkvg = _norm_proj(x, norm_g[i], a_w_in[j].astype(BF16), PROJ_TM, 1024)
            q_rot, kp, ks, vp, vs = _rope_a(qkvg, cos1, sn1)
            og_p = _attn_a_prompt(q_rot, kp, vp, qkvg, a_lam[j], a_subln_g[j], lam_init)
            og_s = _attn_a_sample(q_rot, ks, vs, qkvg,
                                  cache_a_k.reshape(-1, LANES), cache_a_v.reshape(-1, LANES), j,
                                  a_lam[j], a_subln_g[j], lam_init)
            outs["akp"].append(kp.reshape(BATCH, SEQ, H_A, 2 * DH_A))
            outs["avp"].append(vp.reshape(BATCH, SEQ, H_A, 2 * DH_A))
            outs["aks"].append(ks.reshape(DEC_BATCH, DEC_SEQ, H_A, 2 * DH_A))
            outs["avs"].append(vs.reshape(DEC_BATCH, DEC_SEQ, H_A, 2 * DH_A))
        elif kind == 1:
            w = b_w_in[j]
            w = jnp.concatenate([w[:, :4096], w[:, 4176:], w[:, 4096:4176],
                                 jnp.zeros((D_MODEL, B_COLS - w.shape[1]), w.dtype)], axis=1)
            proj = _norm_proj(x, norm_g[i], w.astype(BF16), PROJ_TM, 896)
            q_rot, qi_rot, kid, wi, kp, ks, vp, vs, ip, is_ = _rope_b(proj, cos1, sn1, cos2, sn2)
            og_p = _dsa_prompt(q_rot, kp, vp, qi_rot, kid, wi, proj)
            kidx_past = cache_b_kidx.reshape(-1, D_IDX)
            og_s = _dsa_sample(q_rot, ks, vs, qi_rot, kid, wi, proj,
                               cache_b_k.reshape(-1, KV_B * HD_B), cache_b_v.reshape(-1, KV_B * HD_B),
                               jnp.concatenate([kidx_past, kidx_past], axis=1), j)
            outs["bkp"].append(kp.reshape(BATCH, SEQ, KV_B, HD_B))
            outs["bvp"].append(vp.reshape(BATCH, SEQ, KV_B, HD_B))
            outs["bip"].append(ip.reshape(BATCH, SEQ, D_IDX))
            outs["bks"].append(ks.reshape(DEC_BATCH, DEC_SEQ, KV_B, HD_B))
            outs["bvs"].append(vs.reshape(DEC_BATCH, DEC_SEQ, KV_B, HD_B))
            outs["bis"].append(is_.reshape(DEC_BATCH, DEC_SEQ, D_IDX))
        else:
            shift0 = jnp.concatenate([jnp.zeros((BATCH, D_MODEL), F32), state_c_shift[j]], axis=0)
            lerp, hlast = _c_prep(x, norm_g[i], c_mu[j], shift0)
            rkvg = _c_bmm(lerp, c_w_rkvg[j].astype(BF16))
            wl, al = _c_lora(lerp, c_w_la[j], c_w_lb[j], c_a_la[j], c_a_lb[j], c_w0[j], c_a0[j])
            par = (c_k_k[j], c_k_a[j], c_r_k[j], c_ln_w[j], c_ln_b[j])
            og_p, st_p = _rwkv(rkvg, wl, al, *par, None, nseq=BATCH, nchunk=SEQ // CHUNK, row0=0)
            og_s, st_s = _rwkv(rkvg, wl, al, *par, state_c_wkv[j], nseq=DEC_BATCH, nchunk=1,
                               row0=N_PROMPT // CHUNK)
            outs["cwp"].append(st_p)
            outs["chp"].append(hlast[:BATCH, 0])
            outs["cws"].append(st_s)
            outs["chs"].append(hlast[BATCH:, 0])
        if i == DEPTH - 1:
            yp, ys = _proj_out(og_p, og_s, x, w_out[i].astype(BF16), final_g)
        else:
            x = _proj_out(og_p, og_s, x, w_out[i].astype(BF16))
    st = lambda n: jnp.stack(outs[n])
    return (yp.reshape(BATCH, SEQ, D_MODEL), ys.reshape(DEC_BATCH, DEC_SEQ, D_MODEL),
            st("akp"), st("avp"), st("aks"), st("avs"),
            st("bkp"), st("bvp"), st("bip"), st("bks"), st("bvs"), st("bis"),
            st("cwp"), st("chp"), st("cws"), st("chs"))
```

```python
import functools
import math

import jax
import jax.numpy as jnp
from jax import lax
from jax.experimental import pallas as pl
from jax.experimental.pallas import tpu as pltpu

F32 = jnp.float32
BF16 = jnp.bfloat16
I32 = jnp.int32

D_MODEL = 2048
BATCH = 4
SEQ = 2048
DEPTH = 4
DEC_BATCH = 8
DEC_SEQ = 64
PAST_LEN = 4096
CHUNK = 64
N_MIXERS = 3
ROPE_THETA = 10000.0
RMS_EPS = 1e-6
DH_A = 64
H_A = 16
SUBLN_EPS = 1e-5
HD_B = 128
H_B = 16
KV_B = 4
H_IDX = 16
D_IDX = 64
TOPK_MAX = 256
HS_C = 64
H_C = 32
R_DECAY = 96
R_ICLR = 96
GN_EPS = 64e-5

N_PROMPT = BATCH * SEQ
N_SAMPLE = DEC_BATCH * DEC_SEQ
M_ROWS = N_PROMPT + N_SAMPLE
LANES = 128
VMEM_LIMIT = 56 * 1024 * 1024
PROJ_TM = M_ROWS // 8
OG_DTYPE = jnp.bfloat16
NEG_INF = float("-inf")
INT_MIN = -2 ** 31
F32_MIN_NORMAL = float.fromhex("0x1p-126")
F32_LOWEST = -float.fromhex("0x1.fffffep127")

B_Q0, B_K0, B_V0, B_QI0, B_G0, B_KI0 = 0, 2048, 2560, 3072, 4096, 6144
B_COLS = 6272

assert PAST_LEN % CHUNK == 0 and DEC_SEQ == CHUNK
LOG2_CHUNK = CHUNK.bit_length() - 1
assert 1 << LOG2_CHUNK == CHUNK


def _chunk_of(pos):
    return jnp.right_shift(pos, LOG2_CHUNK)


def _cparams(sem):
    return pltpu.CompilerParams(dimension_semantics=sem, vmem_limit_bytes=VMEM_LIMIT)


def _sigmoid(x):
    return 1.0 / (1.0 + jnp.exp(-x))


def _silu(x):
    return x * _sigmoid(x)


def _rms(x, g):
    return x * lax.rsqrt(jnp.mean(x * x, axis=-1, keepdims=True) + RMS_EPS) * g


def _dot_nt(a, b, precision=None):
    return lax.dot_general(a, b, (((1,), (1,)), ((), ())), precision=precision,
                           preferred_element_type=F32)


def _norm_proj_body(x_ref, g_ref, w_ref, o_ref, h_scr):
    @pl.when(pl.program_id(1) == 0)
    def _():
        h_scr[...] = _rms(x_ref[...], g_ref[...]).astype(BF16)
    o_ref[...] = jnp.dot(h_scr[...], w_ref[...], preferred_element_type=F32)


def _cast_body(x_ref, o_ref):
    o_ref[...] = x_ref[...].astype(o_ref.dtype)


def _to_bf16(w, tr=1024, tc=2048):
    w2 = w.reshape(-1, w.shape[-1])
    r, c = w2.shape
    blk = pl.BlockSpec((tr, tc), lambda i, j: (i, j))
    return pl.pallas_call(
        _cast_body,
        grid=(r // tr, c // tc),
        in_specs=[blk],
        out_specs=blk,
        out_shape=jax.ShapeDtypeStruct((r, c), BF16),
        compiler_params=_cparams(("parallel", "parallel")),
        name="to_bf16",
    )(w2)


def _norm_proj(x, g, w, layer, tm, tn):
    m, d = x.shape
    n = w.shape[1]
    return pl.pallas_call(
        _norm_proj_body,
        grid=(m // tm, n // tn),
        in_specs=[pl.BlockSpec((tm, d), lambda i, j: (i, 0)),
                  pl.BlockSpec((1, d), lambda i, j: (0, 0)),
                  pl.BlockSpec((d, tn), lambda i, j: (layer, j))],
        out_specs=pl.BlockSpec((tm, tn), lambda i, j: (i, j)),
        out_shape=jax.ShapeDtypeStruct((m, n), F32),
        scratch_shapes=[pltpu.VMEM((tm, d), BF16)],
        compiler_params=_cparams(("parallel", "arbitrary")),
        name="norm_proj",
    )(x, g.reshape(1, d), w)


def _proj_out_body(ogp_ref, ogs_ref, x_ref, w_ref, o_ref, *, n_prompt_tiles):
    def emit(og_ref):
        o_ref[...] = x_ref[...] + jnp.dot(og_ref[...].astype(BF16), w_ref[...],
                                          preferred_element_type=F32)

    _when_group(n_prompt_tiles, emit, (ogp_ref,), (ogs_ref,))


def _proj_out_final_body(ogp_ref, ogs_ref, x_ref, w_ref, g_ref, yp_ref, ys_ref, *, n_prompt_tiles):
    def emit(og_ref, y_ref):
        xn = x_ref[...] + jnp.dot(og_ref[...].astype(BF16), w_ref[...], preferred_element_type=F32)
        y_ref[...] = _rms(xn, g_ref[...])

    _when_group(n_prompt_tiles, emit, (ogp_ref, yp_ref), (ogs_ref, ys_ref))


def _proj_out(og_prompt, og_sample, x, w, layer, final_g=None, tm=512):
    m, d = x.shape
    row = pl.BlockSpec((tm, d), lambda i: (i, 0))
    full = pl.BlockSpec((d, d), lambda i: (layer, 0))
    in_specs = _group_specs(tm, d) + [row, full]
    if final_g is None:
        body, extra, out_specs = _proj_out_body, (), row
        out_shape = jax.ShapeDtypeStruct((m, d), F32)
    else:
        body, extra = _proj_out_final_body, (final_g.reshape(1, d),)
        in_specs.append(pl.BlockSpec((1, d), lambda i: (0, 0)))
        out_specs, out_shape = _group_specs(tm, d), _group_shapes(d)
    return pl.pallas_call(
        functools.partial(body, n_prompt_tiles=N_PROMPT // tm),
        grid=(m // tm,),
        in_specs=in_specs,
        out_specs=out_specs,
        out_shape=out_shape,
        compiler_params=_cparams(("arbitrary",)),
        name="proj_out",
    )(og_prompt, og_sample, x, w, *extra)


def _rope_tables():
    pos = jnp.concatenate([jnp.tile(jnp.arange(SEQ), BATCH),
                           jnp.tile(PAST_LEN + jnp.arange(DEC_SEQ), DEC_BATCH)]).astype(F32)

    def table(dh, reps):
        half = dh // 2
        inv = jnp.power(ROPE_THETA, -jnp.arange(half, dtype=F32) * (2.0 / dh))
        ang = pos[:, None] * inv[None, :]
        cos, sin = jnp.cos(ang), jnp.sin(ang)
        return (jnp.tile(jnp.concatenate([cos, cos], axis=1), (1, reps)),
                jnp.tile(jnp.concatenate([-sin, sin], axis=1), (1, reps)))

    cos1, sn1 = table(64, 2)
    cos2, sn2 = table(128, 1)
    return cos1, sn1, cos2, sn2


def _rope64(x, cos, sn):
    lane = lax.broadcasted_iota(I32, x.shape, 1)
    partner = jnp.where((lane & 63) < 32, pltpu.roll(x, 96, 1), pltpu.roll(x, 32, 1))
    return x * cos + partner * sn


def _rope128(x, cos, sn):
    return x * cos + pltpu.roll(x, 64, 1) * sn


def _group_specs(tm, width):
    npt = N_PROMPT // tm
    return [pl.BlockSpec((tm, width), lambda i: (jnp.minimum(i, npt - 1), 0)),
            pl.BlockSpec((tm, width), lambda i: (jnp.maximum(i - npt, 0), 0))]


def _group_shapes(width):
    return [jax.ShapeDtypeStruct((N_PROMPT, width), F32), jax.ShapeDtypeStruct((N_SAMPLE, width), F32)]


def _when_group(n_prompt_tiles, emit, prompt_refs, sample_refs):
    is_prompt = pl.program_id(0) < n_prompt_tiles
    pl.when(is_prompt)(lambda: emit(*prompt_refs))
    pl.when(jnp.logical_not(is_prompt))(lambda: emit(*sample_refs))


def _rope_a_body(q_ref, k_ref, v_ref, cos_ref, sn_ref, qo_ref, kp_ref, ks_ref, vp_ref, vs_ref,
                 *, n_prompt_tiles):
    cos, sn = cos_ref[...], sn_ref[...]
    hsl = [slice(h * LANES, (h + 1) * LANES) for h in range(H_A)]
    for sl in hsl:
        qo_ref[:, sl] = _rope64(q_ref[:, sl], cos, sn) * (DH_A ** -0.5)

    def emit(ko_ref, vo_ref):
        for sl in hsl:
            ko_ref[:, sl] = _rope64(k_ref[:, sl], cos, sn)
        vo_ref[...] = v_ref[...]

    _when_group(n_prompt_tiles, emit, (kp_ref, vp_ref), (ks_ref, vs_ref))


def _rope_a(qkvg, cos1, sn1, tm=256):
    m = qkvg.shape[0]
    blk = lambda c: pl.BlockSpec((tm, D_MODEL), lambda i, c=c: (i, c))
    tab = pl.BlockSpec((tm, LANES), lambda i: (i, 0))
    return pl.pallas_call(
        functools.partial(_rope_a_body, n_prompt_tiles=N_PROMPT // tm),
        grid=(m // tm,),
        in_specs=[blk(0), blk(1), blk(2), tab, tab],
        out_specs=[blk(0)] + _group_specs(tm, D_MODEL) * 2,
        out_shape=[jax.ShapeDtypeStruct((m, D_MODEL), F32)] + _group_shapes(D_MODEL) * 2,
        compiler_params=_cparams(("arbitrary",)),
        name="rope_a",
    )(qkvg, qkvg, qkvg, cos1, sn1)


def _stack_maps_t(q):
    lane = lax.broadcasted_iota(I32, q.shape, 1)
    qs = jnp.concatenate([jnp.where(lane < DH_A, q, 0.0), jnp.where(lane >= DH_A, q, 0.0)], axis=0)
    return qs.T.astype(BF16)


def _flash_t_step(s, vt, carry):
    m, l, acc = carry
    m_new = jnp.maximum(m, jnp.max(s, axis=0, keepdims=True))
    alpha = jnp.exp(m - m_new)
    p = jnp.exp(s - m_new)
    l = alpha * l + jnp.sum(p, axis=0, keepdims=True)
    pv = lax.dot_general(vt, p.astype(BF16), DIMS_TN, preferred_element_type=F32)
    return m_new, l, alpha * acc + pv


def _flash_t_init(n):
    return (jnp.full((1, n), -1e30, F32), jnp.zeros((1, n), F32), jnp.zeros((LANES, n), F32))


def _diff_epilogue(l, acc, tq, lam_ref, sg, g, lam_init):
    o = (acc / l).T
    lp = lam_ref[...]
    lam = (jnp.exp(jnp.sum(lp[0:1] * lp[1:2], axis=-1, keepdims=True))
           - jnp.exp(jnp.sum(lp[2:3] * lp[3:4], axis=-1, keepdims=True)) + lam_init)
    o = o[:tq] - lam * o[tq:]
    o = o * lax.rsqrt(jnp.mean(o * o, axis=-1, keepdims=True) + SUBLN_EPS) * sg
    o = o * (1.0 - lam_init)
    return o * _silu(g)


A_HP = 4


def _attn_a_prompt_body(q_ref, k_ref, v_ref, g_ref, lam_ref, sg_ref, o_ref, kb_scr, vb_scr,
                        *, tq, lam_init):
    qi = pl.program_id(2)

    @pl.when(qi == 0)
    def _():
        kb_scr[...] = k_ref[...].astype(BF16)
        vb_scr[...] = v_ref[...].astype(BF16)

    hsl = [slice(h * LANES, (h + 1) * LANES) for h in range(A_HP)]
    qst = [_stack_maps_t(q_ref[:, sl]) for sl in hsl]
    n = 2 * tq
    krow = lax.broadcasted_iota(I32, (tq, n), 0)
    qcol = lax.broadcasted_iota(I32, (tq, n), 1) & (tq - 1)
    diag_mask = _chunk_of(krow) <= _chunk_of(qcol)

    def tile(kt, carries, masked):
        off = pl.multiple_of(kt * tq, tq)
        ss = [jnp.dot(kb_scr[pl.ds(off, tq), sl], qst[h], preferred_element_type=F32)
              for h, sl in enumerate(hsl)]
        out = []
        for h, sl in enumerate(hsl):
            s = jnp.where(diag_mask, ss[h], NEG_INF) if masked else ss[h]
            out.append(_flash_t_step(s, vb_scr[pl.ds(off, tq), sl], carries[h]))
        return tuple(out)

    carries = lax.fori_loop(0, qi, lambda kt, c: tile(kt, c, False),
                            tuple(_flash_t_init(n) for _ in hsl))
    carries = tile(qi, carries, True)
    for h, sl in enumerate(hsl):
        _, l, acc = carries[h]
        o_ref[:, sl] = _diff_epilogue(l, acc, tq, lam_ref, sg_ref[...], g_ref[:, sl],
                                      lam_init).astype(o_ref.dtype)


def _attn_a_prompt(q_rot, k_p, v_p, qkvg, lam_p, subln_g, lam_init, tq=256):
    nq = SEQ // tq
    w = A_HP * LANES
    nh = H_A // A_HP
    return pl.pallas_call(
        functools.partial(_attn_a_prompt_body, tq=tq, lam_init=lam_init),
        grid=(BATCH, nh, nq),
        in_specs=[pl.BlockSpec((tq, w), lambda b, h, i: (b * nq + i, h)),
                  pl.BlockSpec((SEQ, w), lambda b, h, i: (b, h)),
                  pl.BlockSpec((SEQ, w), lambda b, h, i: (b, h)),
                  pl.BlockSpec((tq, w), lambda b, h, i: (b * nq + i, 3 * nh + h)),
                  pl.BlockSpec((4, DH_A), lambda b, h, i: (0, 0)),
                  pl.BlockSpec((1, LANES), lambda b, h, i: (0, 0))],
        out_specs=pl.BlockSpec((tq, w), lambda b, h, i: (b * nq + i, h)),
        out_shape=jax.ShapeDtypeStruct((N_PROMPT, D_MODEL), OG_DTYPE),
        scratch_shapes=[pltpu.VMEM((SEQ, w), BF16), pltpu.VMEM((SEQ, w), BF16)],
        compiler_params=_cparams(("parallel", "parallel", "arbitrary")),
        name="attn_a_prompt",
    )(q_rot, k_p, v_p, qkvg, lam_p, subln_g.reshape(1, LANES))


def _attn_a_sample_body(q_ref, kp_ref, vp_ref, k_ref, v_ref, g_ref, lam_ref, sg_ref,
                        o_ref, qst_scr, ml_scr, acc_scr, *, tk, lam_init):
    kt = pl.program_id(1)
    tq = DEC_SEQ
    hsl = [slice(h * LANES, (h + 1) * LANES) for h in range(H_A)]

    @pl.when(kt == 0)
    def _():
        for h, sl in enumerate(hsl):
            qst_scr[h] = _stack_maps_t(q_ref[:, sl])
        m0, l0, acc0 = _flash_t_init(2 * tq)
        for h in range(H_A):
            ml_scr[h, 0:1, :] = m0
            ml_scr[h, 1:2, :] = l0
            acc_scr[h] = acc0

    def load(h):
        return ml_scr[h, 0:1, :], ml_scr[h, 1:2, :], acc_scr[h]

    def store(h, carry):
        ml_scr[h, 0:1, :], ml_scr[h, 1:2, :], acc_scr[h] = carry

    def head_group(i, _):
        hs = [A_HP * i + d for d in range(A_HP)]
        ss = [jnp.dot(kp_ref[pl.ds(h, tk, stride=H_A), :].astype(BF16), qst_scr[h],
                      preferred_element_type=F32) for h in hs]
        for h, s in zip(hs, ss):
            vb = vp_ref[pl.ds(h, tk, stride=H_A), :].astype(BF16)
            store(h, _flash_t_step(s, vb, load(h)))
        return 0

    lax.fori_loop(0, H_A // A_HP, head_group, 0)

    @pl.when(kt == pl.num_programs(1) - 1)
    def _():
        for h, sl in enumerate(hsl):
            s = jnp.dot(k_ref[:, sl].astype(BF16), qst_scr[h], preferred_element_type=F32)
            _, l, acc = _flash_t_step(s, v_ref[:, sl].astype(BF16), load(h))
            o_ref[:, sl] = _diff_epilogue(l, acc, tq, lam_ref, sg_ref[...], g_ref[:, sl],
                                          lam_init).astype(o_ref.dtype)


def _attn_a_sample(q_rot, k_s, v_s, qkvg, k_past, v_past, layer, lam_p, subln_g, lam_init, tk=512):
    r0 = N_PROMPT // DEC_SEQ
    nkt = PAST_LEN // tk
    new = lambda c: pl.BlockSpec((DEC_SEQ, D_MODEL), lambda b, t, c=c: (r0 + b, c))
    own = pl.BlockSpec((DEC_SEQ, D_MODEL), lambda b, t: (b, 0))
    past = pl.BlockSpec((tk * H_A, LANES), lambda b, t: ((layer * DEC_BATCH + b) * nkt + t, 0))
    return pl.pallas_call(
        functools.partial(_attn_a_sample_body, tk=tk, lam_init=lam_init),
        grid=(DEC_BATCH, nkt),
        in_specs=[new(0), past, past, own, own, new(3),
                  pl.BlockSpec((4, DH_A), lambda b, t: (0, 0)),
                  pl.BlockSpec((1, LANES), lambda b, t: (0, 0))],
        out_specs=own,
        out_shape=jax.ShapeDtypeStruct((N_SAMPLE, D_MODEL), OG_DTYPE),
        scratch_shapes=[pltpu.VMEM((H_A, LANES, 2 * DEC_SEQ), BF16),
                        pltpu.VMEM((H_A, 8, 2 * DEC_SEQ), F32),
                        pltpu.VMEM((H_A, LANES, 2 * DEC_SEQ), F32)],
        compiler_params=_cparams(("parallel", "arbitrary")),
        name="attn_a_sample",
    )(q_rot, k_past, v_past, k_s, v_s, qkvg, lam_p, subln_g.reshape(1, LANES))


def _rope_b_body(q_ref, k_ref, v_ref, qi_ref, sm_ref, c1_ref, s1_ref, c2_ref, s2_ref,
                 qo_ref, qio_ref, kid_ref, wi_ref, kp_ref, ks_ref, vp_ref, vs_ref, ip_ref, is_ref,
                 *, n_prompt_tiles):
    c1, s1, c2, s2 = c1_ref[...], s1_ref[...], c2_ref[...], s2_ref[...]
    for h in range(H_B):
        sl = slice(h * LANES, (h + 1) * LANES)
        qo_ref[:, sl] = _rope128(q_ref[:, sl], c2, s2) * (HD_B ** -0.5)
    for h in range(H_IDX * D_IDX // LANES):
        sl = slice(h * LANES, (h + 1) * LANES)
        qio_ref[:, sl] = _rope64(qi_ref[:, sl], c1, s1) * (D_IDX ** -0.5)
    sm = sm_ref[...]
    lane = lax.broadcasted_iota(I32, sm.shape, 1)
    kr = _rope64(sm, c1, s1)
    kid_ref[...] = jnp.where(lane < D_IDX, kr, pltpu.roll(kr, D_IDX, 1))
    wi_ref[...] = pltpu.roll(sm, D_IDX, 1) * (H_IDX ** -0.5)

    def emit(ko_ref, vo_ref, io_ref):
        for h in range(KV_B):
            sl = slice(h * LANES, (h + 1) * LANES)
            ko_ref[:, sl] = _rope128(k_ref[:, sl], c2, s2)
        vo_ref[...] = v_ref[...]
        io_ref[...] = kr[:, :D_IDX]

    _when_group(n_prompt_tiles, emit, (kp_ref, vp_ref, ip_ref), (ks_ref, vs_ref, is_ref))


def _rope_b(proj, cos1, sn1, cos2, sn2, tm=256):
    m = proj.shape[0]
    tab = pl.BlockSpec((tm, LANES), lambda i: (i, 0))
    blk = lambda w, c: pl.BlockSpec((tm, w), lambda i, c=c: (i, c))
    widths = (H_B * HD_B, H_IDX * D_IDX, LANES, LANES)
    kvw = KV_B * HD_B
    return pl.pallas_call(
        functools.partial(_rope_b_body, n_prompt_tiles=N_PROMPT // tm),
        grid=(m // tm,),
        in_specs=[blk(2048, 0), blk(kvw, B_K0 // kvw), blk(kvw, B_V0 // kvw),
                  blk(1024, B_QI0 // 1024), blk(LANES, B_KI0 // LANES), tab, tab, tab, tab],
        out_specs=([blk(w, 0) for w in widths] + _group_specs(tm, kvw) * 2
                   + _group_specs(tm, D_IDX)),
        out_shape=([jax.ShapeDtypeStruct((m, w), F32) for w in widths] + _group_shapes(kvw) * 2
                   + _group_shapes(D_IDX)),
        compiler_params=_cparams(("arbitrary",)),
        name="rope_b",
    )(proj, proj, proj, proj, proj, cos1, sn1, cos2, sn2)


POS_BITS = 13
DSA_BUCKET = 512


def _count(pred):
    return jnp.sum(jnp.where(pred, 1.0, 0.0), axis=-1, keepdims=True)


def _select_topk(scores, poss, n_sel):
    tq = scores[0].shape[0]
    n_sel = float(n_sel)

    def key_to_float(key):
        return lax.bitcast_convert_type(jnp.where(key < 0, key ^ 0x7FFFFFFF, key), F32)

    def thr_step(i, t):
        cand_bits = t | lax.shift_left(jnp.int32(1), 31 - i)
        cand = key_to_float(cand_bits ^ INT_MIN)
        cnt = sum(_count(s >= cand) for s in scores)
        return jnp.where(cnt >= n_sel, cand_bits, t)

    t = lax.fori_loop(0, 32, thr_step, jnp.zeros((tq, 1), I32))
    thr_key = t ^ INT_MIN
    thr, above = key_to_float(thr_key), key_to_float(thr_key + 1)
    above = jnp.where(thr == 0.0, F32_MIN_NORMAL, above)
    few = sum(_count(s > NEG_INF) for s in scores) <= n_sel
    thr = jnp.where(few, F32_LOWEST, thr)
    above = jnp.where(few, F32_LOWEST, above)
    gts = [s >= above for s in scores]
    eqs = [(s >= thr) & jnp.logical_not(g) for s, g in zip(scores, gts)]
    need = n_sel - sum(_count(g) for g in gts)
    tied = [jnp.where(e, 1.0, 0.0) for e in eqs]
    n_tied = sum(jnp.sum(e, axis=-1, keepdims=True) for e in tied)

    def tie_step(i, j):
        cand = j | lax.shift_left(jnp.int32(1), POS_BITS - 1 - i)
        cnt = sum(jnp.sum(jnp.where(p < cand, e, 0.0), axis=-1, keepdims=True)
                  for e, p in zip(tied, poss))
        return jnp.where(cnt < need, cand, j)

    j = lax.cond(jnp.max(n_tied - need) > 0.0,
                 lambda: lax.fori_loop(0, POS_BITS, tie_step, jnp.zeros((tq, 1), I32)),
                 lambda: jnp.full((tq, 1), (1 << POS_BITS) - 1, I32))
    return [g | (e & (p <= j)) for g, e, p in zip(gts, eqs, poss)]


def _dsa_body(*refs, tq, seg_lens, seg_pos0, q_pos0, causal, n_sel, bucket):
    nseg = len(seg_lens)
    n_in = 4 + 3 * nseg
    assert max(seg_pos0[i] + seg_lens[i] for i in range(nseg)) <= 1 << POS_BITS
    if bucket is None:
        _dsa_run(refs, tq, seg_lens, seg_pos0, q_pos0, causal, n_sel, n_in)
        return
    assert causal and nseg == 1 and seg_pos0[0] == 0 and q_pos0 == 0 and bucket % tq == 0
    last = (pl.program_id(1) * tq) // bucket
    for v in range(seg_lens[0] // bucket):
        pl.when(last == v)(functools.partial(_dsa_run, refs, tq, ((v + 1) * bucket,), seg_pos0,
                                             q_pos0, causal, n_sel, n_in))


def _dsa_run(refs, tq, seg_lens, seg_pos0, q_pos0, causal, n_sel, n_in):
    nseg = len(seg_lens)
    qi_ref, wi_ref, q_ref, g_ref = refs[:4]
    kid_refs = refs[4:4 + nseg]
    k_refs = refs[4 + nseg:4 + 2 * nseg]
    v_refs = refs[4 + 2 * nseg:4 + 3 * nseg]
    o_ref = refs[n_in]
    bias_refs = refs[n_in + 1:]
    n_rep = H_B // KV_B

    @pl.when(pl.program_id(2) == 0)
    def _():
        wi = wi_ref[...]
        qpos = q_pos0 + pl.program_id(1) * tq
        scores, poss, adms = [], [], []
        for si in range(nseg):
            sl = seg_lens[si]
            kid = kid_refs[si][0:sl, :].astype(BF16)
            sc = jnp.zeros((tq, sl), F32)
            for p in range(H_IDX // 2):
                qp = qi_ref[:, p * LANES:(p + 1) * LANES]
                lane = lax.broadcasted_iota(I32, qp.shape, 1)
                qs = jnp.concatenate([jnp.where(lane < D_IDX, qp, 0.0),
                                      jnp.where(lane >= D_IDX, qp, 0.0)], axis=0).astype(BF16)
                lg = jnp.maximum(_dot_nt(qs, kid), 0.0)
                sc = sc + wi[:, 2 * p:2 * p + 1] * lg[:tq] + wi[:, 2 * p + 1:2 * p + 2] * lg[tq:]
            kpos = seg_pos0[si] + lax.broadcasted_iota(I32, (tq, sl), 1)
            if causal:
                qrow = qpos + lax.broadcasted_iota(I32, (tq, sl), 0)
                adm = _chunk_of(kpos) <= _chunk_of(qrow)
                sc = jnp.where(adm, sc, NEG_INF)
            else:
                adm = None
            scores.append(sc)
            poss.append(kpos)
            adms.append(adm)
        sels = _select_topk(scores, poss, n_sel)
        for si in range(nseg):
            bias = jnp.where(sels[si], 0.0, NEG_INF)
            if adms[si] is not None:
                bias = jnp.where(adms[si], bias, NEG_INF)
            bias_refs[si][:, 0:seg_lens[si]] = bias

    q = q_ref[...]
    qg = jnp.concatenate([q[:, r * LANES:(r + 1) * LANES] for r in range(n_rep)],
                         axis=0).astype(BF16)
    ss = []
    for si in range(nseg):
        b = bias_refs[si][:, 0:seg_lens[si]]
        s = _dot_nt(qg, k_refs[si][0:seg_lens[si], :].astype(BF16))
        ss.append(s + jnp.concatenate([b] * n_rep, axis=0))
    m = functools.reduce(jnp.maximum, [jnp.max(s, axis=-1, keepdims=True) for s in ss])
    l = 0.0
    acc = 0.0
    for si in range(nseg):
        p = jnp.exp(ss[si] - m)
        l = l + jnp.sum(p, axis=-1, keepdims=True)
        acc = acc + jnp.dot(p.astype(BF16), v_refs[si][0:seg_lens[si], :].astype(BF16),
                            preferred_element_type=F32)
    o = acc / l
    o = jnp.concatenate([o[r * tq:(r + 1) * tq] for r in range(n_rep)], axis=1)
    o_ref[...] = (o * _silu(g_ref[...])).astype(o_ref.dtype)


def _dsa_prompt(q_rot, k_p, v_p, qi_rot, kid, wi, proj, tq=256):
    nq = SEQ // tq
    gw = KV_B * HD_B
    row = lambda w, c0: pl.BlockSpec((tq, w), lambda b, i, g, c0=c0: (b * nq + i, c0 + g))
    row0 = lambda w: pl.BlockSpec((tq, w), lambda b, i, g: (b * nq + i, 0))
    seq = lambda c0: pl.BlockSpec((SEQ, LANES), lambda b, i, g, c0=c0: (b, c0 + g))
    n_sel = min(TOPK_MAX, SEQ // 4)
    return pl.pallas_call(
        functools.partial(_dsa_body, tq=tq, seg_lens=(SEQ,), seg_pos0=(0,), q_pos0=0,
                          causal=True, n_sel=n_sel, bucket=DSA_BUCKET),
        grid=(BATCH, nq, KV_B),
        in_specs=[row0(H_IDX * D_IDX), row0(LANES), row(gw, 0), row(gw, B_G0 // gw),
                  pl.BlockSpec((SEQ, LANES), lambda b, i, g: (b, 0)),
                  seq(0), seq(0)],
        out_specs=row(gw, 0),
        out_shape=jax.ShapeDtypeStruct((N_PROMPT, D_MODEL), OG_DTYPE),
        scratch_shapes=[pltpu.VMEM((tq, SEQ), F32)],
        compiler_params=_cparams(("parallel", "parallel", "arbitrary")),
        name="dsa_prompt",
    )(qi_rot, wi, q_rot, proj, kid, k_p, v_p)


def _dsa_sample(q_rot, k_s, v_s, qi_rot, kid, wi, proj, k_past, v_past, kid_past, layer):
    tq = DEC_SEQ
    r0 = N_PROMPT // tq
    gw = KV_B * HD_B
    row = lambda w, c0: pl.BlockSpec((tq, w), lambda b, i, g, c0=c0: (r0 + b, c0 + g))
    row0 = lambda w: pl.BlockSpec((tq, w), lambda b, i, g: (r0 + b, 0))
    past = pl.BlockSpec((PAST_LEN, LANES), lambda b, i, g: (layer * DEC_BATCH + b, g))
    own = pl.BlockSpec((tq, LANES), lambda b, i, g: (b, g))
    s_all = PAST_LEN + DEC_SEQ
    n_sel = min(TOPK_MAX, s_all // 4)
    return pl.pallas_call(
        functools.partial(_dsa_body, tq=tq, seg_lens=(PAST_LEN, DEC_SEQ),
                          seg_pos0=(0, PAST_LEN), q_pos0=PAST_LEN, causal=False, n_sel=n_sel,
                          bucket=None),
        grid=(DEC_BATCH, 1, KV_B),
        in_specs=[row0(H_IDX * D_IDX), row0(LANES), row(gw, 0), row(gw, B_G0 // gw),
                  pl.BlockSpec((PAST_LEN, LANES), lambda b, i, g: (layer * DEC_BATCH + b, 0)),
                  row0(LANES),
                  past, own,
                  past, own],
        out_specs=pl.BlockSpec((tq, gw), lambda b, i, g: (b, g)),
        out_shape=jax.ShapeDtypeStruct((N_SAMPLE, D_MODEL), OG_DTYPE),
        scratch_shapes=[pltpu.VMEM((tq, PAST_LEN), F32), pltpu.VMEM((tq, DEC_SEQ), F32)],
        compiler_params=_cparams(("parallel", "arbitrary", "arbitrary")),
        name="dsa_sample",
    )(qi_rot, wi, q_rot, proj, kid_past, kid, k_past, k_s, v_past, v_s)


C_TM = 64
C_TILES_PER_SEQ = SEQ // C_TM
C_PROMPT_TILES = N_PROMPT // C_TM


def _c_seq_id(i):
    return jnp.where(i < C_PROMPT_TILES, i // C_TILES_PER_SEQ, i - C_PROMPT_TILES + BATCH)


def _c_prep_body(x_ref, g_ref, mu_ref, sh_ref, l_ref, hl_ref, carry):
    i = pl.program_id(0)

    @pl.when(i == 0)
    def _():
        carry[...] = jnp.zeros_like(carry)

    h = _rms(x_ref[...], g_ref[...])
    start = jnp.logical_or(i >= C_PROMPT_TILES, i % C_TILES_PER_SEQ == 0)
    first = jnp.where(start, sh_ref[0], carry[...])
    row = lax.broadcasted_iota(I32, h.shape, 0)
    prev = jnp.where(row == 0, first, pltpu.roll(h, 1, 0))
    last = h[C_TM - 1:C_TM, :]
    carry[...] = last
    hl_ref[0] = last
    d = prev - h
    for n in range(6):
        l_ref[n] = (h + d * mu_ref[n:n + 1, :]).astype(BF16)


def _c_prep(x, g, mu, shift0):
    m, d = x.shape
    nseq = shift0.shape[0]
    return pl.pallas_call(
        _c_prep_body,
        grid=(m // C_TM,),
        in_specs=[pl.BlockSpec((C_TM, d), lambda i: (i, 0)),
                  pl.BlockSpec((1, d), lambda i: (0, 0)),
                  pl.BlockSpec((6, d), lambda i: (0, 0)),
                  pl.BlockSpec((1, 1, d), lambda i: (_c_seq_id(i), 0, 0))],
        out_specs=[pl.BlockSpec((6, C_TM, d), lambda i: (0, i, 0)),
                   pl.BlockSpec((1, 1, d), lambda i: (_c_seq_id(i), 0, 0))],
        out_shape=[jax.ShapeDtypeStruct((6, m, d), BF16),
                   jax.ShapeDtypeStruct((nseq, 1, d), F32)],
        scratch_shapes=[pltpu.VMEM((1, d), F32)],
        compiler_params=_cparams(("arbitrary",)),
        name="c_prep",
    )(x, g.reshape(1, d), mu, shift0.reshape(nseq, 1, d))


def _bmm_body(l_ref, w_ref, o_ref):
    o_ref[0] = jnp.dot(l_ref[0], w_ref[0], preferred_element_type=F32)


def _c_bmm(lerp, w, layer, tm=PROJ_TM, tn=1024):
    _, m, d = lerp.shape
    nb, n = 4, w.shape[2]
    return pl.pallas_call(
        _bmm_body,
        grid=(nb, m // tm, n // tn),
        in_specs=[pl.BlockSpec((1, tm, d), lambda b, i, j: (b, i, 0)),
                  pl.BlockSpec((1, d, tn), lambda b, i, j: (layer * nb + b, 0, j))],
        out_specs=pl.BlockSpec((1, tm, tn), lambda b, i, j: (b, i, j)),
        out_shape=jax.ShapeDtypeStruct((nb, m, n), F32),
        compiler_params=_cparams(("parallel", "parallel", "arbitrary")),
        name="c_bmm",
    )(lerp, w)


def _c_lora_body(l4_ref, l5_ref, wla_ref, wlb_ref, ala_ref, alb_ref, w0_ref, a0_ref,
                 wl_ref, al_ref):
    tw = jnp.tanh(jnp.dot(l4_ref[0], wla_ref[...], preferred_element_type=F32))
    wl_ref[...] = w0_ref[...] + jnp.dot(tw.astype(BF16), wlb_ref[...],
                                        preferred_element_type=F32)
    ta = jnp.dot(l5_ref[0], ala_ref[...], preferred_element_type=F32)
    al_ref[...] = a0_ref[...] + jnp.dot(ta.astype(BF16), alb_ref[...],
                                        preferred_element_type=F32)


def _c_lora(lerp, w_la, w_lb, a_la, a_lb, w0, a0, tm=512):
    _, m, d = lerp.shape
    pad_in = lambda w: jnp.pad(w, ((0, 0), (0, LANES - w.shape[1]))).astype(BF16)
    pad_out = lambda w: jnp.pad(w, ((0, LANES - w.shape[0]), (0, 0))).astype(BF16)
    lin = lambda n: pl.BlockSpec((1, tm, d), lambda i, n=n: (n, i, 0))
    win = pl.BlockSpec((d, LANES), lambda i: (0, 0))
    wout = pl.BlockSpec((LANES, d), lambda i: (0, 0))
    vec = pl.BlockSpec((1, d), lambda i: (0, 0))
    out = pl.BlockSpec((tm, d), lambda i: (i, 0))
    return pl.pallas_call(
        _c_lora_body,
        grid=(m // tm,),
        in_specs=[lin(4), lin(5), win, wout, win, wout, vec, vec],
        out_specs=[out, out],
        out_shape=[jax.ShapeDtypeStruct((m, d), F32)] * 2,
        compiler_params=_cparams(("parallel",)),
        name="c_lora",
    )(lerp, lerp, pad_in(w_la), pad_out(w_lb), pad_in(a_la), pad_out(a_lb),
      w0.reshape(1, d), a0.reshape(1, d))


C_HB = 16
DIMS_NN = (((1,), (0,)), ((), ()))
DIMS_NT = (((1,), (1,)), ((), ()))
DIMS_TN = (((0,), (0,)), ((), ()))


def _dot1(a, b, dims):
    return lax.dot_general(a.astype(BF16), b.astype(BF16), dims, preferred_element_type=F32)


def _dot3(a, b, dims):
    (ca,), (cb,) = dims[0]
    ah = a.astype(BF16).astype(F32)
    bh = b.astype(BF16).astype(F32)
    sa = jnp.concatenate([ah, a - ah, ah], axis=ca).astype(BF16)
    sb = jnp.concatenate([bh, bh, b - bh], axis=cb).astype(BF16)
    return lax.dot_general(sa, sb, dims, preferred_element_type=F32)


def _cumsum_rows(x):
    row = lax.broadcasted_iota(I32, x.shape, 0)
    sh = 1
    while sh < x.shape[0]:
        x = x + jnp.where(row >= sh, pltpu.roll(x, sh, 0), 0.0)
        sh *= 2
    return x


def _rwkv_body(*refs, has_init):
    if has_init:
        (r_ref, k_ref, v_ref, g_ref, wl_ref, al_ref, kk_ref, ka_ref, rk_ref, lnw_ref, lnb_ref,
         s0_ref, og_ref, so_ref, s_scr) = refs
    else:
        (r_ref, k_ref, v_ref, g_ref, wl_ref, al_ref, kk_ref, ka_ref, rk_ref, lnw_ref, lnb_ref,
         og_ref, so_ref, s_scr) = refs
    c = pl.program_id(2)
    n = CHUNK
    hs = HS_C

    @pl.when(c == 0)
    def _():
        if has_init:
            s_scr[...] = s0_ref[0]
        else:
            s_scr[...] = jnp.zeros_like(s_scr)

    r, k, v, g = r_ref[0], k_ref[0], v_ref[0], g_ref[0]
    logw = -math.exp(-0.5) * _sigmoid(wl_ref[...])
    a = _sigmoid(al_ref[...])
    kkr = k * kk_ref[...]
    k2 = k * (1.0 + (a - 1.0) * ka_ref[...])
    bonus_in = r * k2 * rk_ref[...]

    cs_all = _cumsum_rows(logw)
    ec_all, eci_all, ecp_all = jnp.exp(cs_all), jnp.exp(-cs_all), jnp.exp(cs_all - logw)

    ti = lax.broadcasted_iota(I32, (n, n), 0)
    tj = lax.broadcasted_iota(I32, (n, n), 1)
    eye = jnp.where(tj == ti, 1.0, 0.0)
    gi = lax.broadcasted_iota(I32, (2 * n, 2 * n), 0)
    gj = lax.broadcasted_iota(I32, (2 * n, 2 * n), 1) & (n - 1)
    gmask = ((gi < n) & (gj < gi)) | ((gi >= n) & (gj <= gi - n))
    lvl_masks = []
    bs = 1
    while bs < n:
        sh = bs.bit_length()
        lvl_masks.append((jnp.right_shift(ti, sh) == jnp.right_shift(tj, sh))
                         & ((ti & bs) != 0) & ((tj & bs) == 0))
        bs *= 2

    heads = range(C_HB)
    sls = [slice(j * hs, (j + 1) * hs) for j in heads]
    vs = [v[:, sl] for sl in sls]
    zero = jnp.zeros_like(vs[0])
    gw = 4 * hs
    ei = lax.broadcasted_iota(I32, (gw, gw), 0)
    ej = lax.broadcasted_iota(I32, (gw, gw), 1)
    lg = hs.bit_length() - 1
    seg = jnp.where(jnp.right_shift(ei, lg) == jnp.right_shift(ej, lg), 1.0, 0.0).astype(BF16)
    sq = kkr * kkr
    sq_hi = sq.astype(BF16)
    sq_lo = (sq - sq_hi.astype(F32)).astype(BF16)
    ssq = jnp.concatenate(
        [jnp.dot(sq_hi[:, c:c + gw], seg, preferred_element_type=F32)
         + jnp.dot(sq_lo[:, c:c + gw], seg, preferred_element_type=F32)
         for c in range(0, C_HB * hs, gw)], axis=1)
    kkn = kkr / jnp.maximum(jnp.sqrt(ssq), 1e-12)
    p_all, q_all = kkn * ecp_all, kkn * a * eci_all
    kt_all, rt_all = k2 * eci_all, r * ec_all
    p_, q_ = [p_all[:, sl] for sl in sls], [q_all[:, sl] for sl in sls]
    kt, rt = [kt_all[:, sl] for sl in sls], [rt_all[:, sl] for sl in sls]
    gm = [jnp.where(gmask, _dot1(jnp.concatenate([p_[j], rt[j]], axis=0),
                                 jnp.concatenate([q_[j], kt[j]], axis=0), DIMS_NT), 0.0)
          for j in heads]
    av = [_dot1(gm[j][:n], jnp.concatenate([zero, vs[j]], axis=0), DIMS_NN)
          for j in heads]
    a_qp = [gm[j][:n, :n] for j in heads]
    x = [eye - jnp.where(lvl_masks[0], a_qp[j], 0.0) for j in heads]
    for msk in lvl_masks[1:]:
        ax = [_dot1(jnp.where(msk, a_qp[j], 0.0), x[j], DIMS_NN) for j in heads]
        x = [x[j] - _dot1(x[j], ax[j], DIMS_NN) for j in heads]
    xpw = [_dot1(x[j], jnp.concatenate([p_[j], av[j]], axis=1), DIMS_NN)
           for j in heads]
    low = [jnp.concatenate([zero, vs[j]], axis=1) for j in heads]
    tb = [_dot3(jnp.concatenate([xpw[j], low[j]], axis=0),
                jnp.concatenate([-q_[j], kt[j]], axis=0), DIMS_TN) for j in heads]
    ro = [_dot1(gm[j][n:], jnp.concatenate([-xpw[j], low[j]], axis=0), DIMS_NN) for j in heads]
    s0 = [s_scr[j] for j in heads]
    o = [_dot1(rt[j] + ro[j][:, :hs], s0[j], DIMS_NT) + ro[j][:, hs:] for j in heads]
    for j in heads:
        ecl = ec_all[n - 1:n, sls[j]]
        s_scr[j] = _dot3(s0[j], (eye + tb[j][:hs]) * ecl, DIMS_NN) + tb[j][hs:] * ecl
    outs = []
    for j in heads:
        mean = jnp.mean(o[j], axis=-1, keepdims=True)
        var = jnp.mean(jnp.square(o[j] - mean), axis=-1, keepdims=True)
        on = (o[j] - mean) * lax.rsqrt(var + GN_EPS) * lnw_ref[:, sls[j]] + lnb_ref[:, sls[j]]
        bonus = jnp.sum(bonus_in[:, sls[j]], axis=-1, keepdims=True) * vs[j]
        outs.append(on + bonus)
    og_ref[...] = (jnp.concatenate(outs, axis=1) * _silu(g)).astype(og_ref.dtype)

    @pl.when(c == pl.num_programs(2) - 1)
    def _():
        so_ref[0] = s_scr[...]


def _rwkv(rkvg, wl, al, k_k, k_a, r_k, ln_w, ln_b, s0, *, nseq, nchunk, row0):
    w = C_HB * HS_C
    tok = lambda n: pl.BlockSpec((1, CHUNK, w), lambda s, h, c, n=n: (n, row0 + s * nchunk + c, h))
    tok2 = pl.BlockSpec((CHUNK, w), lambda s, h, c: (row0 + s * nchunk + c, h))
    vec = pl.BlockSpec((1, w), lambda s, h, c: (0, h))
    st = pl.BlockSpec((1, C_HB, HS_C, HS_C), lambda s, h, c: (s, h, 0, 0))
    has_init = s0 is not None
    ins = [rkvg, rkvg, rkvg, rkvg, wl, al] + [p.reshape(1, D_MODEL) for p in (k_k, k_a, r_k, ln_w, ln_b)]
    specs = [tok(0), tok(1), tok(2), tok(3), tok2, tok2] + [vec] * 5
    if has_init:
        ins.append(s0)
        specs.append(st)
    return pl.pallas_call(
        functools.partial(_rwkv_body, has_init=has_init),
        grid=(nseq, H_C // C_HB, nchunk),
        in_specs=specs,
        out_specs=[pl.BlockSpec((CHUNK, w), lambda s, h, c: (s * nchunk + c, h)), st],
        out_shape=[jax.ShapeDtypeStruct((nseq * nchunk * CHUNK, D_MODEL), OG_DTYPE),
                   jax.ShapeDtypeStruct((nseq, H_C, HS_C, HS_C), F32)],
        scratch_shapes=[pltpu.VMEM((C_HB, HS_C, HS_C), F32)],
        compiler_params=_cparams(("parallel", "parallel", "arbitrary")),
        name="rwkv_scan",
    )(*ins)


def kernel(x_prompt, x_sample, cache_a_k, cache_a_v, cache_b_k, cache_b_v, cache_b_kidx, state_c_wkv, state_c_shift, norm_g, final_g, w_out, a_w_in, a_lam, a_subln_g, b_w_in, c_mu, c_w_rkvg, c_w0, c_w_la, c_w_lb, c_a0, c_a_la, c_a_lb, c_k_k, c_k_a, c_r_k, c_ln_w, c_ln_b):
    x = jnp.concatenate([x_prompt.reshape(N_PROMPT, D_MODEL),
                         x_sample.reshape(N_SAMPLE, D_MODEL)], axis=0)
    cos1, sn1, cos2, sn2 = _rope_tables()
    a_w = _to_bf16(a_w_in)
    c_w = _to_bf16(c_w_rkvg).reshape(-1, D_MODEL, D_MODEL)
    o_w = _to_bf16(w_out)
    outs = {n: [] for n in ("akp", "avp", "aks", "avs", "bkp", "bvp", "bip", "bks", "bvs", "bis",
                            "cwp", "chp", "cws", "chs")}
    for i in range(DEPTH):
        kind, j = i % N_MIXERS, i // N_MIXERS
        if kind == 0:
            lam_init = 0.8 - 0.6 * math.exp(-0.3 * i)
            qkvg = _norm_proj(x, norm_g[i], a_w, j, PROJ_TM, 1024)
            q_rot, kp, ks, vp, vs = _rope_a(qkvg, cos1, sn1)
            og_p = _attn_a_prompt(q_rot, kp, vp, qkvg, a_lam[j], a_subln_g[j], lam_init)
            og_s = _attn_a_sample(q_rot, ks, vs, qkvg,
                                  cache_a_k.reshape(-1, LANES), cache_a_v.reshape(-1, LANES), j,
                                  a_lam[j], a_subln_g[j], lam_init)
            outs["akp"].append(kp.reshape(BATCH, SEQ, H_A, 2 * DH_A))
            outs["avp"].append(vp.reshape(BATCH, SEQ, H_A, 2 * DH_A))
            outs["aks"].append(ks.reshape(DEC_BATCH, DEC_SEQ, H_A, 2 * DH_A))
            outs["avs"].append(vs.reshape(DEC_BATCH, DEC_SEQ, H_A, 2 * DH_A))
        elif kind == 1:
            w = b_w_in[j]
            w = jnp.concatenate([w[:, :4096], w[:, 4176:], w[:, 4096:4176],
                                 jnp.zeros((D_MODEL, B_COLS - w.shape[1]), w.dtype)], axis=1)
            proj = _norm_proj(x, norm_g[i], w.astype(BF16), 0, PROJ_TM, 896)
            q_rot, qi_rot, kid, wi, kp, ks, vp, vs, ip, is_ = _rope_b(proj, cos1, sn1, cos2, sn2)
            og_p = _dsa_prompt(q_rot, kp, vp, qi_rot, kid, wi, proj)
            kidx_past = cache_b_kidx.reshape(-1, D_IDX)
            og_s = _dsa_sample(q_rot, ks, vs, qi_rot, kid, wi, proj,
                               cache_b_k.reshape(-1, KV_B * HD_B), cache_b_v.reshape(-1, KV_B * HD_B),
                               jnp.concatenate([kidx_past, kidx_past], axis=1), j)
            outs["bkp"].append(kp.reshape(BATCH, SEQ, KV_B, HD_B))
            outs["bvp"].append(vp.reshape(BATCH, SEQ, KV_B, HD_B))
            outs["bip"].append(ip.reshape(BATCH, SEQ, D_IDX))
            outs["bks"].append(ks.reshape(DEC_BATCH, DEC_SEQ, KV_B, HD_B))
            outs["bvs"].append(vs.reshape(DEC_BATCH, DEC_SEQ, KV_B, HD_B))
            outs["bis"].append(is_.reshape(DEC_BATCH, DEC_SEQ, D_IDX))
        else:
            shift0 = jnp.concatenate([jnp.zeros((BATCH, D_MODEL), F32), state_c_shift[j]], axis=0)
            lerp, hlast = _c_prep(x, norm_g[i], c_mu[j], shift0)
            rkvg = _c_bmm(lerp, c_w, j)
            wl, al = _c_lora(lerp, c_w_la[j], c_w_lb[j], c_a_la[j], c_a_lb[j], c_w0[j], c_a0[j])
            par = (c_k_k[j], c_k_a[j], c_r_k[j], c_ln_w[j], c_ln_b[j])
            og_p, st_p = _rwkv(rkvg, wl, al, *par, None, nseq=BATCH, nchunk=SEQ // CHUNK, row0=0)
            og_s, st_s = _rwkv(rkvg, wl, al, *par, state_c_wkv[j], nseq=DEC_BATCH, nchunk=1,
                               row0=N_PROMPT // CHUNK)
            outs["cwp"].append(st_p)
            outs["chp"].append(hlast[:BATCH, 0])
            outs["cws"].append(st_s)
            outs["chs"].append(hlast[BATCH:, 0])
        if i == DEPTH - 1:
            yp, ys = _proj_out(og_p, og_s, x, o_w, i, final_g)
        else:
            x = _proj_out(og_p, og_s, x, o_w, i)
    st = lambda n: jnp.stack(outs[n])
    return (yp.reshape(BATCH, SEQ, D_MODEL), ys.reshape(DEC_BATCH, DEC_SEQ, D_MODEL),
            st("akp"), st("avp"), st("aks"), st("avs"),
            st("bkp"), st("bvp"), st("bip"), st("bks"), st("bvs"), st("bis"),
            st("cwp"), st("chp"), st("cws"), st("chs"))
```

```python
import functools
import math

import jax
import jax.numpy as jnp
from jax import lax
from jax.experimental import pallas as pl
from jax.experimental.pallas import tpu as pltpu

F32 = jnp.float32
BF16 = jnp.bfloat16
I32 = jnp.int32

D_MODEL = 2048
BATCH = 4
SEQ = 2048
DEPTH = 4
DEC_BATCH = 8
DEC_SEQ = 64
PAST_LEN = 4096
CHUNK = 64
N_MIXERS = 3
ROPE_THETA = 10000.0
RMS_EPS = 1e-6
DH_A = 64
H_A = 16
SUBLN_EPS = 1e-5
HD_B = 128
H_B = 16
KV_B = 4
H_IDX = 16
D_IDX = 64
TOPK_MAX = 256
HS_C = 64
H_C = 32
R_DECAY = 96
R_ICLR = 96
GN_EPS = 64e-5

N_PROMPT = BATCH * SEQ
N_SAMPLE = DEC_BATCH * DEC_SEQ
M_ROWS = N_PROMPT + N_SAMPLE
LANES = 128
VMEM_LIMIT = 56 * 1024 * 1024
PROJ_TM = M_ROWS // 8
OG_DTYPE = jnp.bfloat16
NEG_INF = float("-inf")
INT_MIN = -2 ** 31
F32_MIN_NORMAL = float.fromhex("0x1p-126")
F32_LOWEST = -float.fromhex("0x1.fffffep127")

B_Q0, B_K0, B_V0, B_QI0, B_G0, B_KI0 = 0, 2048, 2560, 3072, 4096, 6144
B_COLS = 6272

assert PAST_LEN % CHUNK == 0 and DEC_SEQ == CHUNK
LOG2_CHUNK = CHUNK.bit_length() - 1
assert 1 << LOG2_CHUNK == CHUNK


def _chunk_of(pos):
    return jnp.right_shift(pos, LOG2_CHUNK)


def _cparams(sem):
    return pltpu.CompilerParams(dimension_semantics=sem, vmem_limit_bytes=VMEM_LIMIT)


def _sigmoid(x):
    return 1.0 / (1.0 + jnp.exp(-x))


def _silu(x):
    return x * _sigmoid(x)


def _rms(x, g):
    return x * lax.rsqrt(jnp.mean(x * x, axis=-1, keepdims=True) + RMS_EPS) * g


def _dot_nt(a, b, precision=None):
    return lax.dot_general(a, b, (((1,), (1,)), ((), ())), precision=precision,
                           preferred_element_type=F32)


def _norm_proj_body(x_ref, g_ref, w_ref, o_ref, h_scr):
    @pl.when(pl.program_id(1) == 0)
    def _():
        h_scr[...] = _rms(x_ref[...], g_ref[...]).astype(BF16)
    o_ref[...] = jnp.dot(h_scr[...], w_ref[...], preferred_element_type=F32)


def _cast_body(x_ref, o_ref):
    o_ref[...] = x_ref[...].astype(o_ref.dtype)


def _to_bf16(w, tr=1024, tc=2048):
    w2 = w.reshape(-1, w.shape[-1])
    r, c = w2.shape
    blk = pl.BlockSpec((tr, tc), lambda i, j: (i, j))
    return pl.pallas_call(
        _cast_body,
        grid=(r // tr, c // tc),
        in_specs=[blk],
        out_specs=blk,
        out_shape=jax.ShapeDtypeStruct((r, c), BF16),
        compiler_params=_cparams(("parallel", "parallel")),
        name="to_bf16",
    )(w2)


def _norm_proj(x, g, w, layer, tm, tn):
    m, d = x.shape
    n = w.shape[1]
    return pl.pallas_call(
        _norm_proj_body,
        grid=(m // tm, n // tn),
        in_specs=[pl.BlockSpec((tm, d), lambda i, j: (i, 0)),
                  pl.BlockSpec((1, d), lambda i, j: (0, 0)),
                  pl.BlockSpec((d, tn), lambda i, j: (layer, j))],
        out_specs=pl.BlockSpec((tm, tn), lambda i, j: (i, j)),
        out_shape=jax.ShapeDtypeStruct((m, n), F32),
        scratch_shapes=[pltpu.VMEM((tm, d), BF16)],
        compiler_params=_cparams(("parallel", "arbitrary")),
        name="norm_proj",
    )(x, g.reshape(1, d), w)


def _proj_out_body(ogp_ref, ogs_ref, x_ref, w_ref, o_ref, *, n_prompt_tiles):
    def emit(og_ref):
        o_ref[...] = x_ref[...] + jnp.dot(og_ref[...].astype(BF16), w_ref[...],
                                          preferred_element_type=F32)

    _when_group(n_prompt_tiles, emit, (ogp_ref,), (ogs_ref,))


def _proj_out_final_body(ogp_ref, ogs_ref, x_ref, w_ref, g_ref, yp_ref, ys_ref, *, n_prompt_tiles):
    def emit(og_ref, y_ref):
        xn = x_ref[...] + jnp.dot(og_ref[...].astype(BF16), w_ref[...], preferred_element_type=F32)
        y_ref[...] = _rms(xn, g_ref[...])

    _when_group(n_prompt_tiles, emit, (ogp_ref, yp_ref), (ogs_ref, ys_ref))


def _proj_out(og_prompt, og_sample, x, w, layer, final_g=None, tm=512):
    m, d = x.shape
    row = pl.BlockSpec((tm, d), lambda i: (i, 0))
    full = pl.BlockSpec((d, d), lambda i: (layer, 0))
    in_specs = _group_specs(tm, d) + [row, full]
    if final_g is None:
        body, extra, out_specs = _proj_out_body, (), row
        out_shape = jax.ShapeDtypeStruct((m, d), F32)
    else:
        body, extra = _proj_out_final_body, (final_g.reshape(1, d),)
        in_specs.append(pl.BlockSpec((1, d), lambda i: (0, 0)))
        out_specs, out_shape = _group_specs(tm, d), _group_shapes(d)
    return pl.pallas_call(
        functools.partial(body, n_prompt_tiles=N_PROMPT // tm),
        grid=(m // tm,),
        in_specs=in_specs,
        out_specs=out_specs,
        out_shape=out_shape,
        compiler_params=_cparams(("arbitrary",)),
        name="proj_out",
    )(og_prompt, og_sample, x, w, *extra)


def _rope_tables():
    pos = jnp.concatenate([jnp.tile(jnp.arange(SEQ), BATCH),
                           jnp.tile(PAST_LEN + jnp.arange(DEC_SEQ), DEC_BATCH)]).astype(F32)

    def table(dh, reps):
        half = dh // 2
        inv = jnp.power(ROPE_THETA, -jnp.arange(half, dtype=F32) * (2.0 / dh))
        ang = pos[:, None] * inv[None, :]
        cos, sin = jnp.cos(ang), jnp.sin(ang)
        return (jnp.tile(jnp.concatenate([cos, cos], axis=1), (1, reps)),
                jnp.tile(jnp.concatenate([-sin, sin], axis=1), (1, reps)))

    cos1, sn1 = table(64, 2)
    cos2, sn2 = table(128, 1)
    return cos1, sn1, cos2, sn2


def _rope64(x, cos, sn):
    lane = lax.broadcasted_iota(I32, x.shape, 1)
    partner = jnp.where((lane & 63) < 32, pltpu.roll(x, 96, 1), pltpu.roll(x, 32, 1))
    return x * cos + partner * sn


def _rope128(x, cos, sn):
    return x * cos + pltpu.roll(x, 64, 1) * sn


def _group_specs(tm, width):
    npt = N_PROMPT // tm
    return [pl.BlockSpec((tm, width), lambda i: (jnp.minimum(i, npt - 1), 0)),
            pl.BlockSpec((tm, width), lambda i: (jnp.maximum(i - npt, 0), 0))]


def _group_shapes(width):
    return [jax.ShapeDtypeStruct((N_PROMPT, width), F32), jax.ShapeDtypeStruct((N_SAMPLE, width), F32)]


def _when_group(n_prompt_tiles, emit, prompt_refs, sample_refs):
    is_prompt = pl.program_id(0) < n_prompt_tiles
    pl.when(is_prompt)(lambda: emit(*prompt_refs))
    pl.when(jnp.logical_not(is_prompt))(lambda: emit(*sample_refs))


def _rope_a_body(q_ref, k_ref, v_ref, cos_ref, sn_ref, qo_ref, kp_ref, ks_ref, vp_ref, vs_ref,
                 *, n_prompt_tiles):
    cos, sn = cos_ref[...], sn_ref[...]
    hsl = [slice(h * LANES, (h + 1) * LANES) for h in range(H_A)]
    for sl in hsl:
        qo_ref[:, sl] = _rope64(q_ref[:, sl], cos, sn) * (DH_A ** -0.5)

    def emit(ko_ref, vo_ref):
        for sl in hsl:
            ko_ref[:, sl] = _rope64(k_ref[:, sl], cos, sn)
        vo_ref[...] = v_ref[...]

    _when_group(n_prompt_tiles, emit, (kp_ref, vp_ref), (ks_ref, vs_ref))


def _rope_a(qkvg, cos1, sn1, tm=256):
    m = qkvg.shape[0]
    blk = lambda c: pl.BlockSpec((tm, D_MODEL), lambda i, c=c: (i, c))
    tab = pl.BlockSpec((tm, LANES), lambda i: (i, 0))
    return pl.pallas_call(
        functools.partial(_rope_a_body, n_prompt_tiles=N_PROMPT // tm),
        grid=(m // tm,),
        in_specs=[blk(0), blk(1), blk(2), tab, tab],
        out_specs=[blk(0)] + _group_specs(tm, D_MODEL) * 2,
        out_shape=[jax.ShapeDtypeStruct((m, D_MODEL), F32)] + _group_shapes(D_MODEL) * 2,
        compiler_params=_cparams(("arbitrary",)),
        name="rope_a",
    )(qkvg, qkvg, qkvg, cos1, sn1)


def _stack_maps_t(q):
    lane = lax.broadcasted_iota(I32, q.shape, 1)
    qs = jnp.concatenate([jnp.where(lane < DH_A, q, 0.0), jnp.where(lane >= DH_A, q, 0.0)], axis=0)
    return qs.T.astype(BF16)


def _flash_t_step(s, vt, carry):
    m, l, acc = carry
    m_new = jnp.maximum(m, jnp.max(s, axis=0, keepdims=True))
    alpha = jnp.exp(m - m_new)
    p = jnp.exp(s - m_new)
    l = alpha * l + jnp.sum(p, axis=0, keepdims=True)
    pv = lax.dot_general(vt, p.astype(BF16), DIMS_TN, preferred_element_type=F32)
    return m_new, l, alpha * acc + pv


def _flash_t_init(n):
    return (jnp.full((1, n), -1e30, F32), jnp.zeros((1, n), F32), jnp.zeros((LANES, n), F32))


def _diff_epilogue(l, acc, tq, lam_ref, sg, g, lam_init):
    o = (acc / l).T
    lp = lam_ref[...]
    lam = (jnp.exp(jnp.sum(lp[0:1] * lp[1:2], axis=-1, keepdims=True))
           - jnp.exp(jnp.sum(lp[2:3] * lp[3:4], axis=-1, keepdims=True)) + lam_init)
    o = o[:tq] - lam * o[tq:]
    o = o * lax.rsqrt(jnp.mean(o * o, axis=-1, keepdims=True) + SUBLN_EPS) * sg
    o = o * (1.0 - lam_init)
    return o * _silu(g)


A_HP = 8


def _attn_a_prompt_body(q_ref, k_ref, v_ref, g_ref, lam_ref, sg_ref, o_ref, kb_scr, vb_scr,
                        *, tq, lam_init):
    qi = pl.program_id(2)

    @pl.when(qi == 0)
    def _():
        kb_scr[...] = k_ref[...].astype(BF16)
        vb_scr[...] = v_ref[...].astype(BF16)

    hsl = [slice(h * LANES, (h + 1) * LANES) for h in range(A_HP)]
    qst = [_stack_maps_t(q_ref[:, sl]) for sl in hsl]
    n = 2 * tq
    krow = lax.broadcasted_iota(I32, (tq, n), 0)
    qcol = lax.broadcasted_iota(I32, (tq, n), 1) & (tq - 1)
    diag_mask = _chunk_of(krow) <= _chunk_of(qcol)

    def tile(kt, carries, masked):
        off = pl.multiple_of(kt * tq, tq)
        ss = [jnp.dot(kb_scr[pl.ds(off, tq), sl], qst[h], preferred_element_type=F32)
              for h, sl in enumerate(hsl)]
        out = []
        for h, sl in enumerate(hsl):
            s = jnp.where(diag_mask, ss[h], NEG_INF) if masked else ss[h]
            out.append(_flash_t_step(s, vb_scr[pl.ds(off, tq), sl], carries[h]))
        return tuple(out)

    carries = lax.fori_loop(0, qi, lambda kt, c: tile(kt, c, False),
                            tuple(_flash_t_init(n) for _ in hsl))
    carries = tile(qi, carries, True)
    for h, sl in enumerate(hsl):
        _, l, acc = carries[h]
        o_ref[:, sl] = _diff_epilogue(l, acc, tq, lam_ref, sg_ref[...], g_ref[:, sl],
                                      lam_init).astype(o_ref.dtype)


def _attn_a_prompt(q_rot, k_p, v_p, qkvg, lam_p, subln_g, lam_init, tq=256):
    nq = SEQ // tq
    w = A_HP * LANES
    nh = H_A // A_HP
    return pl.pallas_call(
        functools.partial(_attn_a_prompt_body, tq=tq, lam_init=lam_init),
        grid=(BATCH, nh, nq),
        in_specs=[pl.BlockSpec((tq, w), lambda b, h, i: (b * nq + i, h)),
                  pl.BlockSpec((SEQ, w), lambda b, h, i: (b, h)),
                  pl.BlockSpec((SEQ, w), lambda b, h, i: (b, h)),
                  pl.BlockSpec((tq, w), lambda b, h, i: (b * nq + i, 3 * nh + h)),
                  pl.BlockSpec((4, DH_A), lambda b, h, i: (0, 0)),
                  pl.BlockSpec((1, LANES), lambda b, h, i: (0, 0))],
        out_specs=pl.BlockSpec((tq, w), lambda b, h, i: (b * nq + i, h)),
        out_shape=jax.ShapeDtypeStruct((N_PROMPT, D_MODEL), OG_DTYPE),
        scratch_shapes=[pltpu.VMEM((SEQ, w), BF16), pltpu.VMEM((SEQ, w), BF16)],
        compiler_params=_cparams(("parallel", "parallel", "arbitrary")),
        name="attn_a_prompt",
    )(q_rot, k_p, v_p, qkvg, lam_p, subln_g.reshape(1, LANES))


def _attn_a_sample_body(q_ref, kp_ref, vp_ref, k_ref, v_ref, g_ref, lam_ref, sg_ref,
                        o_ref, qst_scr, ml_scr, acc_scr, *, tk, lam_init):
    kt = pl.program_id(1)
    tq = DEC_SEQ
    hsl = [slice(h * LANES, (h + 1) * LANES) for h in range(H_A)]

    @pl.when(kt == 0)
    def _():
        for h, sl in enumerate(hsl):
            qst_scr[h] = _stack_maps_t(q_ref[:, sl])
        m0, l0, acc0 = _flash_t_init(2 * tq)
        for h in range(H_A):
            ml_scr[h, 0:1, :] = m0
            ml_scr[h, 1:2, :] = l0
            acc_scr[h] = acc0

    def load(h):
        return ml_scr[h, 0:1, :], ml_scr[h, 1:2, :], acc_scr[h]

    def store(h, carry):
        ml_scr[h, 0:1, :], ml_scr[h, 1:2, :], acc_scr[h] = carry

    def head_group(i, _):
        hs = [A_HP * i + d for d in range(A_HP)]
        ss = [jnp.dot(kp_ref[pl.ds(h, tk, stride=H_A), :].astype(BF16), qst_scr[h],
                      preferred_element_type=F32) for h in hs]
        for h, s in zip(hs, ss):
            vb = vp_ref[pl.ds(h, tk, stride=H_A), :].astype(BF16)
            store(h, _flash_t_step(s, vb, load(h)))
        return 0

    lax.fori_loop(0, H_A // A_HP, head_group, 0)

    @pl.when(kt == pl.num_programs(1) - 1)
    def _():
        for h, sl in enumerate(hsl):
            s = jnp.dot(k_ref[:, sl].astype(BF16), qst_scr[h], preferred_element_type=F32)
            _, l, acc = _flash_t_step(s, v_ref[:, sl].astype(BF16), load(h))
            o_ref[:, sl] = _diff_epilogue(l, acc, tq, lam_ref, sg_ref[...], g_ref[:, sl],
                                          lam_init).astype(o_ref.dtype)


def _attn_a_sample(q_rot, k_s, v_s, qkvg, k_past, v_past, layer, lam_p, subln_g, lam_init, tk=512):
    r0 = N_PROMPT // DEC_SEQ
    nkt = PAST_LEN // tk
    new = lambda c: pl.BlockSpec((DEC_SEQ, D_MODEL), lambda b, t, c=c: (r0 + b, c))
    own = pl.BlockSpec((DEC_SEQ, D_MODEL), lambda b, t: (b, 0))
    past = pl.BlockSpec((tk * H_A, LANES), lambda b, t: ((layer * DEC_BATCH + b) * nkt + t, 0))
    return pl.pallas_call(
        functools.partial(_attn_a_sample_body, tk=tk, lam_init=lam_init),
        grid=(DEC_BATCH, nkt),
        in_specs=[new(0), past, past, own, own, new(3),
                  pl.BlockSpec((4, DH_A), lambda b, t: (0, 0)),
                  pl.BlockSpec((1, LANES), lambda b, t: (0, 0))],
        out_specs=own,
        out_shape=jax.ShapeDtypeStruct((N_SAMPLE, D_MODEL), OG_DTYPE),
        scratch_shapes=[pltpu.VMEM((H_A, LANES, 2 * DEC_SEQ), BF16),
                        pltpu.VMEM((H_A, 8, 2 * DEC_SEQ), F32),
                        pltpu.VMEM((H_A, LANES, 2 * DEC_SEQ), F32)],
        compiler_params=_cparams(("parallel", "arbitrary")),
        name="attn_a_sample",
    )(q_rot, k_past, v_past, k_s, v_s, qkvg, lam_p, subln_g.reshape(1, LANES))


def _rope_b_body(q_ref, k_ref, v_ref, qi_ref, sm_ref, c1_ref, s1_ref, c2_ref, s2_ref,
                 qo_ref, qio_ref, kid_ref, wi_ref, kp_ref, ks_ref, vp_ref, vs_ref, ip_ref, is_ref,
                 *, n_prompt_tiles):
    c1, s1, c2, s2 = c1_ref[...], s1_ref[...], c2_ref[...], s2_ref[...]
    for h in range(H_B):
        sl = slice(h * LANES, (h + 1) * LANES)
        qo_ref[:, sl] = _rope128(q_ref[:, sl], c2, s2) * (HD_B ** -0.5)
    for h in range(H_IDX * D_IDX // LANES):
        sl = slice(h * LANES, (h + 1) * LANES)
        qio_ref[:, sl] = _rope64(qi_ref[:, sl], c1, s1) * (D_IDX ** -0.5)
    sm = sm_ref[...]
    lane = lax.broadcasted_iota(I32, sm.shape, 1)
    kr = _rope64(sm, c1, s1)
    kid_ref[...] = jnp.where(lane < D_IDX, kr, pltpu.roll(kr, D_IDX, 1))
    wi_ref[...] = pltpu.roll(sm, D_IDX, 1) * (H_IDX ** -0.5)

    def emit(ko_ref, vo_ref, io_ref):
        for h in range(KV_B):
            sl = slice(h * LANES, (h + 1) * LANES)
            ko_ref[:, sl] = _rope128(k_ref[:, sl], c2, s2)
        vo_ref[...] = v_ref[...]
        io_ref[...] = kr[:, :D_IDX]

    _when_group(n_prompt_tiles, emit, (kp_ref, vp_ref, ip_ref), (ks_ref, vs_ref, is_ref))


def _rope_b(proj, cos1, sn1, cos2, sn2, tm=256):
    m = proj.shape[0]
    tab = pl.BlockSpec((tm, LANES), lambda i: (i, 0))
    blk = lambda w, c: pl.BlockSpec((tm, w), lambda i, c=c: (i, c))
    widths = (H_B * HD_B, H_IDX * D_IDX, LANES, LANES)
    kvw = KV_B * HD_B
    return pl.pallas_call(
        functools.partial(_rope_b_body, n_prompt_tiles=N_PROMPT // tm),
        grid=(m // tm,),
        in_specs=[blk(2048, 0), blk(kvw, B_K0 // kvw), blk(kvw, B_V0 // kvw),
                  blk(1024, B_QI0 // 1024), blk(LANES, B_KI0 // LANES), tab, tab, tab, tab],
        out_specs=([blk(w, 0) for w in widths] + _group_specs(tm, kvw) * 2
                   + _group_specs(tm, D_IDX)),
        out_shape=([jax.ShapeDtypeStruct((m, w), F32) for w in widths] + _group_shapes(kvw) * 2
                   + _group_shapes(D_IDX)),
        compiler_params=_cparams(("arbitrary",)),
        name="rope_b",
    )(proj, proj, proj, proj, proj, cos1, sn1, cos2, sn2)


POS_BITS = 13
DSA_BUCKET = 512


def _count(pred):
    return jnp.sum(jnp.where(pred, 1.0, 0.0), axis=-1, keepdims=True)


def _select_topk(scores, poss, n_sel):
    tq = scores[0].shape[0]
    n_sel = float(n_sel)

    def key_to_float(key):
        return lax.bitcast_convert_type(jnp.where(key < 0, key ^ 0x7FFFFFFF, key), F32)

    def thr_step(i, t):
        cand_bits = t | lax.shift_left(jnp.int32(1), 31 - i)
        cand = key_to_float(cand_bits ^ INT_MIN)
        cnt = sum(_count(s >= cand) for s in scores)
        return jnp.where(cnt >= n_sel, cand_bits, t)

    t = lax.fori_loop(0, 32, thr_step, jnp.zeros((tq, 1), I32))
    thr_key = t ^ INT_MIN
    thr, above = key_to_float(thr_key), key_to_float(thr_key + 1)
    above = jnp.where(thr == 0.0, F32_MIN_NORMAL, above)
    few = sum(_count(s > NEG_INF) for s in scores) <= n_sel
    thr = jnp.where(few, F32_LOWEST, thr)
    above = jnp.where(few, F32_LOWEST, above)
    gts = [s >= above for s in scores]
    eqs = [(s >= thr) & jnp.logical_not(g) for s, g in zip(scores, gts)]
    need = n_sel - sum(_count(g) for g in gts)
    tied = [jnp.where(e, 1.0, 0.0) for e in eqs]
    n_tied = sum(jnp.sum(e, axis=-1, keepdims=True) for e in tied)

    def tie_step(i, j):
        cand = j | lax.shift_left(jnp.int32(1), POS_BITS - 1 - i)
        cnt = sum(jnp.sum(jnp.where(p < cand, e, 0.0), axis=-1, keepdims=True)
                  for e, p in zip(tied, poss))
        return jnp.where(cnt < need, cand, j)

    j = lax.cond(jnp.max(n_tied - need) > 0.0,
                 lambda: lax.fori_loop(0, POS_BITS, tie_step, jnp.zeros((tq, 1), I32)),
                 lambda: jnp.full((tq, 1), (1 << POS_BITS) - 1, I32))
    return [g | (e & (p <= j)) for g, e, p in zip(gts, eqs, poss)]


def _dsa_body(*refs, tq, seg_lens, seg_pos0, q_pos0, causal, n_sel, bucket):
    nseg = len(seg_lens)
    n_in = 4 + 3 * nseg
    assert max(seg_pos0[i] + seg_lens[i] for i in range(nseg)) <= 1 << POS_BITS
    if bucket is None:
        _dsa_run(refs, tq, seg_lens, seg_pos0, q_pos0, causal, n_sel, n_in)
        return
    assert causal and nseg == 1 and seg_pos0[0] == 0 and q_pos0 == 0 and bucket % tq == 0
    last = (pl.program_id(1) * tq) // bucket
    for v in range(seg_lens[0] // bucket):
        pl.when(last == v)(functools.partial(_dsa_run, refs, tq, ((v + 1) * bucket,), seg_pos0,
                                             q_pos0, causal, n_sel, n_in))


def _dsa_run(refs, tq, seg_lens, seg_pos0, q_pos0, causal, n_sel, n_in):
    nseg = len(seg_lens)
    qi_ref, wi_ref, q_ref, g_ref = refs[:4]
    kid_refs = refs[4:4 + nseg]
    k_refs = refs[4 + nseg:4 + 2 * nseg]
    v_refs = refs[4 + 2 * nseg:4 + 3 * nseg]
    o_ref = refs[n_in]
    bias_refs = refs[n_in + 1:]
    n_rep = H_B // KV_B

    @pl.when(pl.program_id(2) == 0)
    def _():
        wi = wi_ref[...]
        qpos = q_pos0 + pl.program_id(1) * tq
        scores, poss, adms = [], [], []
        for si in range(nseg):
            sl = seg_lens[si]
            kid = kid_refs[si][0:sl, :].astype(BF16)
            sc = jnp.zeros((tq, sl), F32)
            for p in range(H_IDX // 2):
                qp = qi_ref[:, p * LANES:(p + 1) * LANES]
                lane = lax.broadcasted_iota(I32, qp.shape, 1)
                qs = jnp.concatenate([jnp.where(lane < D_IDX, qp, 0.0),
                                      jnp.where(lane >= D_IDX, qp, 0.0)], axis=0).astype(BF16)
                lg = jnp.maximum(_dot_nt(qs, kid), 0.0)
                sc = sc + wi[:, 2 * p:2 * p + 1] * lg[:tq] + wi[:, 2 * p + 1:2 * p + 2] * lg[tq:]
            kpos = seg_pos0[si] + lax.broadcasted_iota(I32, (tq, sl), 1)
            if causal:
                qrow = qpos + lax.broadcasted_iota(I32, (tq, sl), 0)
                adm = _chunk_of(kpos) <= _chunk_of(qrow)
                sc = jnp.where(adm, sc, NEG_INF)
            else:
                adm = None
            scores.append(sc)
            poss.append(kpos)
            adms.append(adm)
        sels = _select_topk(scores, poss, n_sel)
        for si in range(nseg):
            bias = jnp.where(sels[si], 0.0, NEG_INF)
            if adms[si] is not None:
                bias = jnp.where(adms[si], bias, NEG_INF)
            bias_refs[si][:, 0:seg_lens[si]] = bias

    q = q_ref[...]
    qg = jnp.concatenate([q[:, r * LANES:(r + 1) * LANES] for r in range(n_rep)],
                         axis=0).astype(BF16)
    ss = []
    for si in range(nseg):
        b = bias_refs[si][:, 0:seg_lens[si]]
        s = _dot_nt(qg, k_refs[si][0:seg_lens[si], :].astype(BF16))
        ss.append(s + jnp.concatenate([b] * n_rep, axis=0))
    m = functools.reduce(jnp.maximum, [jnp.max(s, axis=-1, keepdims=True) for s in ss])
    l = 0.0
    acc = 0.0
    for si in range(nseg):
        p = jnp.exp(ss[si] - m)
        l = l + jnp.sum(p, axis=-1, keepdims=True)
        acc = acc + jnp.dot(p.astype(BF16), v_refs[si][0:seg_lens[si], :].astype(BF16),
                            preferred_element_type=F32)
    o = acc / l
    o = jnp.concatenate([o[r * tq:(r + 1) * tq] for r in range(n_rep)], axis=1)
    o_ref[...] = (o * _silu(g_ref[...])).astype(o_ref.dtype)


def _dsa_prompt(q_rot, k_p, v_p, qi_rot, kid, wi, proj, tq=256):
    nq = SEQ // tq
    gw = KV_B * HD_B
    row = lambda w, c0: pl.BlockSpec((tq, w), lambda b, i, g, c0=c0: (b * nq + i, c0 + g))
    row0 = lambda w: pl.BlockSpec((tq, w), lambda b, i, g: (b * nq + i, 0))
    seq = lambda c0: pl.BlockSpec((SEQ, LANES), lambda b, i, g, c0=c0: (b, c0 + g))
    n_sel = min(TOPK_MAX, SEQ // 4)
    return pl.pallas_call(
        functools.partial(_dsa_body, tq=tq, seg_lens=(SEQ,), seg_pos0=(0,), q_pos0=0,
                          causal=True, n_sel=n_sel, bucket=DSA_BUCKET),
        grid=(BATCH, nq, KV_B),
        in_specs=[row0(H_IDX * D_IDX), row0(LANES), row(gw, 0), row(gw, B_G0 // gw),
                  pl.BlockSpec((SEQ, LANES), lambda b, i, g: (b, 0)),
                  seq(0), seq(0)],
        out_specs=row(gw, 0),
        out_shape=jax.ShapeDtypeStruct((N_PROMPT, D_MODEL), OG_DTYPE),
        scratch_shapes=[pltpu.VMEM((tq, SEQ), F32)],
        compiler_params=_cparams(("parallel", "parallel", "arbitrary")),
        name="dsa_prompt",
    )(qi_rot, wi, q_rot, proj, kid, k_p, v_p)


def _dsa_sample(q_rot, k_s, v_s, qi_rot, kid, wi, proj, k_past, v_past, kid_past, layer):
    tq = DEC_SEQ
    r0 = N_PROMPT // tq
    gw = KV_B * HD_B
    row = lambda w, c0: pl.BlockSpec((tq, w), lambda b, i, g, c0=c0: (r0 + b, c0 + g))
    row0 = lambda w: pl.BlockSpec((tq, w), lambda b, i, g: (r0 + b, 0))
    past = pl.BlockSpec((PAST_LEN, LANES), lambda b, i, g: (layer * DEC_BATCH + b, g))
    own = pl.BlockSpec((tq, LANES), lambda b, i, g: (b, g))
    s_all = PAST_LEN + DEC_SEQ
    n_sel = min(TOPK_MAX, s_all // 4)
    return pl.pallas_call(
        functools.partial(_dsa_body, tq=tq, seg_lens=(PAST_LEN, DEC_SEQ),
                          seg_pos0=(0, PAST_LEN), q_pos0=PAST_LEN, causal=False, n_sel=n_sel,
                          bucket=None),
        grid=(DEC_BATCH, 1, KV_B),
        in_specs=[row0(H_IDX * D_IDX), row0(LANES), row(gw, 0), row(gw, B_G0 // gw),
                  pl.BlockSpec((PAST_LEN, LANES), lambda b, i, g: (layer * DEC_BATCH + b, 0)),
                  row0(LANES),
                  past, own,
                  past, own],
        out_specs=pl.BlockSpec((tq, gw), lambda b, i, g: (b, g)),
        out_shape=jax.ShapeDtypeStruct((N_SAMPLE, D_MODEL), OG_DTYPE),
        scratch_shapes=[pltpu.VMEM((tq, PAST_LEN), F32), pltpu.VMEM((tq, DEC_SEQ), F32)],
        compiler_params=_cparams(("parallel", "arbitrary", "arbitrary")),
        name="dsa_sample",
    )(qi_rot, wi, q_rot, proj, kid_past, kid, k_past, k_s, v_past, v_s)


C_TM = 64
C_TILES_PER_SEQ = SEQ // C_TM
C_PROMPT_TILES = N_PROMPT // C_TM


def _c_seq_id(i):
    return jnp.where(i < C_PROMPT_TILES, i // C_TILES_PER_SEQ, i - C_PROMPT_TILES + BATCH)


def _c_prep_body(x_ref, g_ref, mu_ref, sh_ref, l_ref, hl_ref, carry):
    i = pl.program_id(0)

    @pl.when(i == 0)
    def _():
        carry[...] = jnp.zeros_like(carry)

    h = _rms(x_ref[...], g_ref[...])
    start = jnp.logical_or(i >= C_PROMPT_TILES, i % C_TILES_PER_SEQ == 0)
    first = jnp.where(start, sh_ref[0], carry[...])
    row = lax.broadcasted_iota(I32, h.shape, 0)
    prev = jnp.where(row == 0, first, pltpu.roll(h, 1, 0))
    last = h[C_TM - 1:C_TM, :]
    carry[...] = last
    hl_ref[0] = last
    d = prev - h
    for n in range(6):
        l_ref[n] = (h + d * mu_ref[n:n + 1, :]).astype(BF16)


def _c_prep(x, g, mu, shift0):
    m, d = x.shape
    nseq = shift0.shape[0]
    return pl.pallas_call(
        _c_prep_body,
        grid=(m // C_TM,),
        in_specs=[pl.BlockSpec((C_TM, d), lambda i: (i, 0)),
                  pl.BlockSpec((1, d), lambda i: (0, 0)),
                  pl.BlockSpec((6, d), lambda i: (0, 0)),
                  pl.BlockSpec((1, 1, d), lambda i: (_c_seq_id(i), 0, 0))],
        out_specs=[pl.BlockSpec((6, C_TM, d), lambda i: (0, i, 0)),
                   pl.BlockSpec((1, 1, d), lambda i: (_c_seq_id(i), 0, 0))],
        out_shape=[jax.ShapeDtypeStruct((6, m, d), BF16),
                   jax.ShapeDtypeStruct((nseq, 1, d), F32)],
        scratch_shapes=[pltpu.VMEM((1, d), F32)],
        compiler_params=_cparams(("arbitrary",)),
        name="c_prep",
    )(x, g.reshape(1, d), mu, shift0.reshape(nseq, 1, d))


def _bmm_body(l_ref, w_ref, o_ref):
    o_ref[0] = jnp.dot(l_ref[0], w_ref[0], preferred_element_type=F32)


def _c_bmm(lerp, w, layer, tm=PROJ_TM, tn=1024):
    _, m, d = lerp.shape
    nb, n = 4, w.shape[2]
    return pl.pallas_call(
        _bmm_body,
        grid=(nb, m // tm, n // tn),
        in_specs=[pl.BlockSpec((1, tm, d), lambda b, i, j: (b, i, 0)),
                  pl.BlockSpec((1, d, tn), lambda b, i, j: (layer * nb + b, 0, j))],
        out_specs=pl.BlockSpec((1, tm, tn), lambda b, i, j: (b, i, j)),
        out_shape=jax.ShapeDtypeStruct((nb, m, n), F32),
        compiler_params=_cparams(("parallel", "parallel", "arbitrary")),
        name="c_bmm",
    )(lerp, w)


def _c_lora_body(l4_ref, l5_ref, wla_ref, wlb_ref, ala_ref, alb_ref, w0_ref, a0_ref,
                 wl_ref, al_ref):
    tw = jnp.tanh(jnp.dot(l4_ref[0], wla_ref[...], preferred_element_type=F32))
    wl_ref[...] = w0_ref[...] + jnp.dot(tw.astype(BF16), wlb_ref[...],
                                        preferred_element_type=F32)
    ta = jnp.dot(l5_ref[0], ala_ref[...], preferred_element_type=F32)
    al_ref[...] = a0_ref[...] + jnp.dot(ta.astype(BF16), alb_ref[...],
                                        preferred_element_type=F32)


def _c_lora(lerp, w_la, w_lb, a_la, a_lb, w0, a0, tm=512):
    _, m, d = lerp.shape
    pad_in = lambda w: jnp.pad(w, ((0, 0), (0, LANES - w.shape[1]))).astype(BF16)
    pad_out = lambda w: jnp.pad(w, ((0, LANES - w.shape[0]), (0, 0))).astype(BF16)
    lin = lambda n: pl.BlockSpec((1, tm, d), lambda i, n=n: (n, i, 0))
    win = pl.BlockSpec((d, LANES), lambda i: (0, 0))
    wout = pl.BlockSpec((LANES, d), lambda i: (0, 0))
    vec = pl.BlockSpec((1, d), lambda i: (0, 0))
    out = pl.BlockSpec((tm, d), lambda i: (i, 0))
    return pl.pallas_call(
        _c_lora_body,
        grid=(m // tm,),
        in_specs=[lin(4), lin(5), win, wout, win, wout, vec, vec],
        out_specs=[out, out],
        out_shape=[jax.ShapeDtypeStruct((m, d), F32)] * 2,
        compiler_params=_cparams(("parallel",)),
        name="c_lora",
    )(lerp, lerp, pad_in(w_la), pad_out(w_lb), pad_in(a_la), pad_out(a_lb),
      w0.reshape(1, d), a0.reshape(1, d))


C_HB = 16
DIMS_NN = (((1,), (0,)), ((), ()))
DIMS_NT = (((1,), (1,)), ((), ()))
DIMS_TN = (((0,), (0,)), ((), ()))


def _dot1(a, b, dims):
    return lax.dot_general(a.astype(BF16), b.astype(BF16), dims, preferred_element_type=F32)


def _dot3(a, b, dims):
    (ca,), (cb,) = dims[0]
    ah = a.astype(BF16).astype(F32)
    bh = b.astype(BF16).astype(F32)
    sa = jnp.concatenate([ah, a - ah, ah], axis=ca).astype(BF16)
    sb = jnp.concatenate([bh, bh, b - bh], axis=cb).astype(BF16)
    return lax.dot_general(sa, sb, dims, preferred_element_type=F32)


def _cumsum_rows(x):
    row = lax.broadcasted_iota(I32, x.shape, 0)
    sh = 1
    while sh < x.shape[0]:
        x = x + jnp.where(row >= sh, pltpu.roll(x, sh, 0), 0.0)
        sh *= 2
    return x


def _rwkv_body(*refs, has_init):
    if has_init:
        (r_ref, k_ref, v_ref, g_ref, wl_ref, al_ref, kk_ref, ka_ref, rk_ref, lnw_ref, lnb_ref,
         s0_ref, og_ref, so_ref, s_scr) = refs
    else:
        (r_ref, k_ref, v_ref, g_ref, wl_ref, al_ref, kk_ref, ka_ref, rk_ref, lnw_ref, lnb_ref,
         og_ref, so_ref, s_scr) = refs
    c = pl.program_id(2)
    n = CHUNK
    hs = HS_C

    @pl.when(c == 0)
    def _():
        if has_init:
            s_scr[...] = s0_ref[0]
        else:
            s_scr[...] = jnp.zeros_like(s_scr)

    r, k, v, g = r_ref[0], k_ref[0], v_ref[0], g_ref[0]
    logw = -math.exp(-0.5) * _sigmoid(wl_ref[...])
    a = _sigmoid(al_ref[...])
    kkr = k * kk_ref[...]
    k2 = k * (1.0 + (a - 1.0) * ka_ref[...])
    bonus_in = r * k2 * rk_ref[...]

    cs_all = _cumsum_rows(logw)
    ec_all, eci_all, ecp_all = jnp.exp(cs_all), jnp.exp(-cs_all), jnp.exp(cs_all - logw)

    ti = lax.broadcasted_iota(I32, (n, n), 0)
    tj = lax.broadcasted_iota(I32, (n, n), 1)
    eye = jnp.where(tj == ti, 1.0, 0.0)
    gi = lax.broadcasted_iota(I32, (2 * n, 2 * n), 0)
    gj = lax.broadcasted_iota(I32, (2 * n, 2 * n), 1) & (n - 1)
    gmask = ((gi < n) & (gj < gi)) | ((gi >= n) & (gj <= gi - n))
    lvl_masks = []
    bs = 1
    while bs < n:
        sh = bs.bit_length()
        lvl_masks.append((jnp.right_shift(ti, sh) == jnp.right_shift(tj, sh))
                         & ((ti & bs) != 0) & ((tj & bs) == 0))
        bs *= 2

    heads = range(C_HB)
    sls = [slice(j * hs, (j + 1) * hs) for j in heads]
    vs = [v[:, sl] for sl in sls]
    zero = jnp.zeros_like(vs[0])
    gw = 4 * hs
    ei = lax.broadcasted_iota(I32, (gw, gw), 0)
    ej = lax.broadcasted_iota(I32, (gw, gw), 1)
    lg = hs.bit_length() - 1
    seg = jnp.where(jnp.right_shift(ei, lg) == jnp.right_shift(ej, lg), 1.0, 0.0).astype(BF16)
    sq = kkr * kkr
    sq_hi = sq.astype(BF16)
    sq_lo = (sq - sq_hi.astype(F32)).astype(BF16)
    ssq = jnp.concatenate(
        [jnp.dot(sq_hi[:, c:c + gw], seg, preferred_element_type=F32)
         + jnp.dot(sq_lo[:, c:c + gw], seg, preferred_element_type=F32)
         for c in range(0, C_HB * hs, gw)], axis=1)
    kkn = kkr / jnp.maximum(jnp.sqrt(ssq), 1e-12)
    p_all, q_all = kkn * ecp_all, kkn * a * eci_all
    kt_all, rt_all = k2 * eci_all, r * ec_all
    p_, q_ = [p_all[:, sl] for sl in sls], [q_all[:, sl] for sl in sls]
    kt, rt = [kt_all[:, sl] for sl in sls], [rt_all[:, sl] for sl in sls]
    gm = [jnp.where(gmask, _dot1(jnp.concatenate([p_[j], rt[j]], axis=0),
                                 jnp.concatenate([q_[j], kt[j]], axis=0), DIMS_NT), 0.0)
          for j in heads]
    av = [_dot1(gm[j][:n], jnp.concatenate([zero, vs[j]], axis=0), DIMS_NN)
          for j in heads]
    a_qp = [gm[j][:n, :n] for j in heads]
    x = [eye - jnp.where(lvl_masks[0], a_qp[j], 0.0) for j in heads]
    for msk in lvl_masks[1:]:
        ax = [_dot1(jnp.where(msk, a_qp[j], 0.0), x[j], DIMS_NN) for j in heads]
        x = [x[j] - _dot1(x[j], ax[j], DIMS_NN) for j in heads]
    xpw = [_dot1(x[j], jnp.concatenate([p_[j], av[j]], axis=1), DIMS_NN)
           for j in heads]
    low = [jnp.concatenate([zero, vs[j]], axis=1) for j in heads]
    tb = [_dot3(jnp.concatenate([xpw[j], low[j]], axis=0),
                jnp.concatenate([-q_[j], kt[j]], axis=0), DIMS_TN) for j in heads]
    ro = [_dot1(gm[j][n:], jnp.concatenate([-xpw[j], low[j]], axis=0), DIMS_NN) for j in heads]
    s0 = [s_scr[j] for j in heads]
    o = [_dot1(rt[j] + ro[j][:, :hs], s0[j], DIMS_NT) + ro[j][:, hs:] for j in heads]
    for j in heads:
        ecl = ec_all[n - 1:n, sls[j]]
        s_scr[j] = _dot3(s0[j], (eye + tb[j][:hs]) * ecl, DIMS_NN) + tb[j][hs:] * ecl
    outs = []
    for j in heads:
        mean = jnp.mean(o[j], axis=-1, keepdims=True)
        var = jnp.mean(jnp.square(o[j] - mean), axis=-1, keepdims=True)
        on = (o[j] - mean) * lax.rsqrt(var + GN_EPS) * lnw_ref[:, sls[j]] + lnb_ref[:, sls[j]]
        bonus = jnp.sum(bonus_in[:, sls[j]], axis=-1, keepdims=True) * vs[j]
        outs.append(on + bonus)
    og_ref[...] = (jnp.concatenate(outs, axis=1) * _silu(g)).astype(og_ref.dtype)

    @pl.when(c == pl.num_programs(2) - 1)
    def _():
        so_ref[0] = s_scr[...]


def _rwkv(rkvg, wl, al, k_k, k_a, r_k, ln_w, ln_b, s0, *, nseq, nchunk, row0):
    w = C_HB * HS_C
    tok = lambda n: pl.BlockSpec((1, CHUNK, w), lambda s, h, c, n=n: (n, row0 + s * nchunk + c, h))
    tok2 = pl.BlockSpec((CHUNK, w), lambda s, h, c: (row0 + s * nchunk + c, h))
    vec = pl.BlockSpec((1, w), lambda s, h, c: (0, h))
    st = pl.BlockSpec((1, C_HB, HS_C, HS_C), lambda s, h, c: (s, h, 0, 0))
    has_init = s0 is not None
    ins = [rkvg, rkvg, rkvg, rkvg, wl, al] + [p.reshape(1, D_MODEL) for p in (k_k, k_a, r_k, ln_w, ln_b)]
    specs = [tok(0), tok(1), tok(2), tok(3), tok2, tok2] + [vec] * 5
    if has_init:
        ins.append(s0)
        specs.append(st)
    return pl.pallas_call(
        functools.partial(_rwkv_body, has_init=has_init),
        grid=(nseq, H_C // C_HB, nchunk),
        in_specs=specs,
        out_specs=[pl.BlockSpec((CHUNK, w), lambda s, h, c: (s * nchunk + c, h)), st],
        out_shape=[jax.ShapeDtypeStruct((nseq * nchunk * CHUNK, D_MODEL), OG_DTYPE),
                   jax.ShapeDtypeStruct((nseq, H_C, HS_C, HS_C), F32)],
        scratch_shapes=[pltpu.VMEM((C_HB, HS_C, HS_C), F32)],
        compiler_params=_cparams(("parallel", "parallel", "arbitrary")),
        name="rwkv_scan",
    )(*ins)


def kernel(x_prompt, x_sample, cache_a_k, cache_a_v, cache_b_k, cache_b_v, cache_b_kidx, state_c_wkv, state_c_shift, norm_g, final_g, w_out, a_w_in, a_lam, a_subln_g, b_w_in, c_mu, c_w_rkvg, c_w0, c_w_la, c_w_lb, c_a0, c_a_la, c_a_lb, c_k_k, c_k_a, c_r_k, c_ln_w, c_ln_b):
    x = jnp.concatenate([x_prompt.reshape(N_PROMPT, D_MODEL),
                         x_sample.reshape(N_SAMPLE, D_MODEL)], axis=0)
    cos1, sn1, cos2, sn2 = _rope_tables()
    a_w = _to_bf16(a_w_in)
    c_w = _to_bf16(c_w_rkvg).reshape(-1, D_MODEL, D_MODEL)
    o_w = _to_bf16(w_out)
    outs = {n: [] for n in ("akp", "avp", "aks", "avs", "bkp", "bvp", "bip", "bks", "bvs", "bis",
                            "cwp", "chp", "cws", "chs")}
    for i in range(DEPTH):
        kind, j = i % N_MIXERS, i // N_MIXERS
        if kind == 0:
            lam_init = 0.8 - 0.6 * math.exp(-0.3 * i)
            qkvg = _norm_proj(x, norm_g[i], a_w, j, PROJ_TM, 1024)
            q_rot, kp, ks, vp, vs = _rope_a(qkvg, cos1, sn1)
            og_p = _attn_a_prompt(q_rot, kp, vp, qkvg, a_lam[j], a_subln_g[j], lam_init)
            og_s = _attn_a_sample(q_rot, ks, vs, qkvg,
                                  cache_a_k.reshape(-1, LANES), cache_a_v.reshape(-1, LANES), j,
                                  a_lam[j], a_subln_g[j], lam_init)
            outs["akp"].append(kp.reshape(BATCH, SEQ, H_A, 2 * DH_A))
            outs["avp"].append(vp.reshape(BATCH, SEQ, H_A, 2 * DH_A))
            outs["aks"].append(ks.reshape(DEC_BATCH, DEC_SEQ, H_A, 2 * DH_A))
            outs["avs"].append(vs.reshape(DEC_BATCH, DEC_SEQ, H_A, 2 * DH_A))
        elif kind == 1:
            w = b_w_in[j]
            w = jnp.concatenate([w[:, :4096], w[:, 4176:], w[:, 4096:4176],
                                 jnp.zeros((D_MODEL, B_COLS - w.shape[1]), w.dtype)], axis=1)
            proj = _norm_proj(x, norm_g[i], w.astype(BF16), 0, PROJ_TM, 896)
            q_rot, qi_rot, kid, wi, kp, ks, vp, vs, ip, is_ = _rope_b(proj, cos1, sn1, cos2, sn2)
            og_p = _dsa_prompt(q_rot, kp, vp, qi_rot, kid, wi, proj)
            kidx_past = cache_b_kidx.reshape(-1, D_IDX)
            og_s = _dsa_sample(q_rot, ks, vs, qi_rot, kid, wi, proj,
                               cache_b_k.reshape(-1, KV_B * HD_B), cache_b_v.reshape(-1, KV_B * HD_B),
                               jnp.concatenate([kidx_past, kidx_past], axis=1), j)
            outs["bkp"].append(kp.reshape(BATCH, SEQ, KV_B, HD_B))
            outs["bvp"].append(vp.reshape(BATCH, SEQ, KV_B, HD_B))
            outs["bip"].append(ip.reshape(BATCH, SEQ, D_IDX))
            outs["bks"].append(ks.reshape(DEC_BATCH, DEC_SEQ, KV_B, HD_B))
            outs["bvs"].append(vs.reshape(DEC_BATCH, DEC_SEQ, KV_B, HD_B))
            outs["bis"].append(is_.reshape(DEC_BATCH, DEC_SEQ, D_IDX))
        else:
            shift0 = jnp.concatenate([jnp.zeros((BATCH, D_MODEL), F32), state_c_shift[j]], axis=0)
            lerp, hlast = _c_prep(x, norm_g[i], c_mu[j], shift0)
            rkvg = _c_bmm(lerp, c_w, j)
            wl, al = _c_lora(lerp, c_w_la[j], c_w_lb[j], c_a_la[j], c_a_lb[j], c_w0[j], c_a0[j])
            par = (c_k_k[j], c_k_a[j], c_r_k[j], c_ln_w[j], c_ln_b[j])
            og_p, st_p = _rwkv(rkvg, wl, al, *par, None, nseq=BATCH, nchunk=SEQ // CHUNK, row0=0)
            og_s, st_s = _rwkv(rkvg, wl, al, *par, state_c_wkv[j], nseq=DEC_BATCH, nchunk=1,
                               row0=N_PROMPT // CHUNK)
            outs["cwp"].append(st_p)
            outs["chp"].append(hlast[:BATCH, 0])
            outs["cws"].append(st_s)
            outs["chs"].append(hlast[BATCH:, 0])
        if i == DEPTH - 1:
            yp, ys = _proj_out(og_p, og_s, x, o_w, i, final_g)
        else:
            x = _proj_out(og_p, og_s, x, o_w, i)
    st = lambda n: jnp.stack(outs[n])
    return (yp.reshape(BATCH, SEQ, D_MODEL), ys.reshape(DEC_BATCH, DEC_SEQ, D_MODEL),
            st("akp"), st("avp"), st("aks"), st("avs"),
            st("bkp"), st("bvp"), st("bip"), st("bks"), st("bvs"), st("bis"),
            st("cwp"), st("chp"), st("cws"), st("chs"))
```

```python
import functools
import math

import jax
import jax.numpy as jnp
from jax import lax
from jax.experimental import pallas as pl
from jax.experimental.pallas import tpu as pltpu

F32 = jnp.float32
BF16 = jnp.bfloat16
I32 = jnp.int32

D_MODEL = 2048
BATCH = 4
SEQ = 2048
DEPTH = 4
DEC_BATCH = 8
DEC_SEQ = 64
PAST_LEN = 4096
CHUNK = 64
N_MIXERS = 3
ROPE_THETA = 10000.0
RMS_EPS = 1e-6
DH_A = 64
H_A = 16
SUBLN_EPS = 1e-5
HD_B = 128
H_B = 16
KV_B = 4
H_IDX = 16
D_IDX = 64
TOPK_MAX = 256
HS_C = 64
H_C = 32
R_DECAY = 96
R_ICLR = 96
GN_EPS = 64e-5

N_PROMPT = BATCH * SEQ
N_SAMPLE = DEC_BATCH * DEC_SEQ
M_ROWS = N_PROMPT + N_SAMPLE
LANES = 128
VMEM_LIMIT = 56 * 1024 * 1024
PROJ_TM = M_ROWS // 8
OG_DTYPE = jnp.bfloat16
NEG_INF = float("-inf")
INT_MIN = -2 ** 31
F32_MIN_NORMAL = float.fromhex("0x1p-126")
F32_LOWEST = -float.fromhex("0x1.fffffep127")

B_Q0, B_K0, B_V0, B_QI0, B_G0, B_KI0 = 0, 2048, 2560, 3072, 4096, 6144
B_COLS = 6272

assert PAST_LEN % CHUNK == 0 and DEC_SEQ == CHUNK
LOG2_CHUNK = CHUNK.bit_length() - 1
assert 1 << LOG2_CHUNK == CHUNK


def _chunk_of(pos):
    return jnp.right_shift(pos, LOG2_CHUNK)


def _cparams(sem):
    return pltpu.CompilerParams(dimension_semantics=sem, vmem_limit_bytes=VMEM_LIMIT)


def _sigmoid(x):
    return 1.0 / (1.0 + jnp.exp(-x))


def _silu(x):
    return x * _sigmoid(x)


def _rms(x, g):
    return x * lax.rsqrt(jnp.mean(x * x, axis=-1, keepdims=True) + RMS_EPS) * g


def _dot_nt(a, b, precision=None):
    return lax.dot_general(a, b, (((1,), (1,)), ((), ())), precision=precision,
                           preferred_element_type=F32)


def _norm_proj_body(x_ref, g_ref, w_ref, o_ref, h_scr):
    @pl.when(pl.program_id(1) == 0)
    def _():
        h_scr[...] = _rms(x_ref[...], g_ref[...]).astype(BF16)
    o_ref[...] = jnp.dot(h_scr[...], w_ref[...], preferred_element_type=F32)


def _cast_body(x_ref, o_ref):
    o_ref[...] = x_ref[...].astype(o_ref.dtype)


def _to_bf16(w, tr=1024, tc=2048):
    w2 = w.reshape(-1, w.shape[-1])
    r, c = w2.shape
    blk = pl.BlockSpec((tr, tc), lambda i, j: (i, j))
    return pl.pallas_call(
        _cast_body,
        grid=(r // tr, c // tc),
        in_specs=[blk],
        out_specs=blk,
        out_shape=jax.ShapeDtypeStruct((r, c), BF16),
        compiler_params=_cparams(("parallel", "parallel")),
        name="to_bf16",
    )(w2)


def _norm_proj(x, g, w, layer, tm, tn):
    m, d = x.shape
    n = w.shape[1]
    return pl.pallas_call(
        _norm_proj_body,
        grid=(m // tm, n // tn),
        in_specs=[pl.BlockSpec((tm, d), lambda i, j: (i, 0)),
                  pl.BlockSpec((1, d), lambda i, j: (0, 0)),
                  pl.BlockSpec((d, tn), lambda i, j: (layer, j))],
        out_specs=pl.BlockSpec((tm, tn), lambda i, j: (i, j)),
        out_shape=jax.ShapeDtypeStruct((m, n), F32),
        scratch_shapes=[pltpu.VMEM((tm, d), BF16)],
        compiler_params=_cparams(("parallel", "arbitrary")),
        name="norm_proj",
    )(x, g.reshape(1, d), w)


def _proj_out_body(ogp_ref, ogs_ref, x_ref, w_ref, o_ref, *, n_prompt_tiles):
    def emit(og_ref):
        o_ref[...] = x_ref[...] + jnp.dot(og_ref[...].astype(BF16), w_ref[...],
                                          preferred_element_type=F32)

    _when_group(n_prompt_tiles, emit, (ogp_ref,), (ogs_ref,))


def _proj_out_final_body(ogp_ref, ogs_ref, x_ref, w_ref, g_ref, yp_ref, ys_ref, *, n_prompt_tiles):
    def emit(og_ref, y_ref):
        xn = x_ref[...] + jnp.dot(og_ref[...].astype(BF16), w_ref[...], preferred_element_type=F32)
        y_ref[...] = _rms(xn, g_ref[...])

    _when_group(n_prompt_tiles, emit, (ogp_ref, yp_ref), (ogs_ref, ys_ref))


def _proj_out(og_prompt, og_sample, x, w, layer, final_g=None, tm=512):
    m, d = x.shape
    row = pl.BlockSpec((tm, d), lambda i: (i, 0))
    full = pl.BlockSpec((d, d), lambda i: (layer, 0))
    in_specs = _group_specs(tm, d) + [row, full]
    if final_g is None:
        body, extra, out_specs = _proj_out_body, (), row
        out_shape = jax.ShapeDtypeStruct((m, d), F32)
    else:
        body, extra = _proj_out_final_body, (final_g.reshape(1, d),)
        in_specs.append(pl.BlockSpec((1, d), lambda i: (0, 0)))
        out_specs, out_shape = _group_specs(tm, d), _group_shapes(d)
    return pl.pallas_call(
        functools.partial(body, n_prompt_tiles=N_PROMPT // tm),
        grid=(m // tm,),
        in_specs=in_specs,
        out_specs=out_specs,
        out_shape=out_shape,
        compiler_params=_cparams(("arbitrary",)),
        name="proj_out",
    )(og_prompt, og_sample, x, w, *extra)


def _rope_tables():
    pos = jnp.concatenate([jnp.tile(jnp.arange(SEQ), BATCH),
                           jnp.tile(PAST_LEN + jnp.arange(DEC_SEQ), DEC_BATCH)]).astype(F32)

    def table(dh, reps):
        half = dh // 2
        inv = jnp.power(ROPE_THETA, -jnp.arange(half, dtype=F32) * (2.0 / dh))
        ang = pos[:, None] * inv[None, :]
        cos, sin = jnp.cos(ang), jnp.sin(ang)
        return (jnp.tile(jnp.concatenate([cos, cos], axis=1), (1, reps)),
                jnp.tile(jnp.concatenate([-sin, sin], axis=1), (1, reps)))

    cos1, sn1 = table(64, 2)
    cos2, sn2 = table(128, 1)
    return cos1, sn1, cos2, sn2


def _rope64(x, cos, sn):
    lane = lax.broadcasted_iota(I32, x.shape, 1)
    partner = jnp.where((lane & 63) < 32, pltpu.roll(x, 96, 1), pltpu.roll(x, 32, 1))
    return x * cos + partner * sn


def _rope128(x, cos, sn):
    return x * cos + pltpu.roll(x, 64, 1) * sn


def _group_specs(tm, width):
    npt = N_PROMPT // tm
    return [pl.BlockSpec((tm, width), lambda i: (jnp.minimum(i, npt - 1), 0)),
            pl.BlockSpec((tm, width), lambda i: (jnp.maximum(i - npt, 0), 0))]


def _group_shapes(width):
    return [jax.ShapeDtypeStruct((N_PROMPT, width), F32), jax.ShapeDtypeStruct((N_SAMPLE, width), F32)]


def _when_group(n_prompt_tiles, emit, prompt_refs, sample_refs):
    is_prompt = pl.program_id(0) < n_prompt_tiles
    pl.when(is_prompt)(lambda: emit(*prompt_refs))
    pl.when(jnp.logical_not(is_prompt))(lambda: emit(*sample_refs))


def _rope_a_body(q_ref, k_ref, v_ref, cos_ref, sn_ref, qo_ref, kp_ref, ks_ref, vp_ref, vs_ref,
                 *, n_prompt_tiles):
    cos, sn = cos_ref[...], sn_ref[...]
    hsl = [slice(h * LANES, (h + 1) * LANES) for h in range(H_A)]
    for sl in hsl:
        qo_ref[:, sl] = _rope64(q_ref[:, sl], cos, sn) * (DH_A ** -0.5)

    def emit(ko_ref, vo_ref):
        for sl in hsl:
            ko_ref[:, sl] = _rope64(k_ref[:, sl], cos, sn)
        vo_ref[...] = v_ref[...]

    _when_group(n_prompt_tiles, emit, (kp_ref, vp_ref), (ks_ref, vs_ref))


def _rope_a(qkvg, cos1, sn1, tm=256):
    m = qkvg.shape[0]
    blk = lambda c: pl.BlockSpec((tm, D_MODEL), lambda i, c=c: (i, c))
    tab = pl.BlockSpec((tm, LANES), lambda i: (i, 0))
    return pl.pallas_call(
        functools.partial(_rope_a_body, n_prompt_tiles=N_PROMPT // tm),
        grid=(m // tm,),
        in_specs=[blk(0), blk(1), blk(2), tab, tab],
        out_specs=[blk(0)] + _group_specs(tm, D_MODEL) * 2,
        out_shape=[jax.ShapeDtypeStruct((m, D_MODEL), F32)] + _group_shapes(D_MODEL) * 2,
        compiler_params=_cparams(("arbitrary",)),
        name="rope_a",
    )(qkvg, qkvg, qkvg, cos1, sn1)


def _stack_maps_t(q):
    lane = lax.broadcasted_iota(I32, q.shape, 1)
    qs = jnp.concatenate([jnp.where(lane < DH_A, q, 0.0), jnp.where(lane >= DH_A, q, 0.0)], axis=0)
    return qs.T.astype(BF16)


def _flash_t_step(s, vt, carry):
    m, l, acc = carry
    m_new = jnp.maximum(m, jnp.max(s, axis=0, keepdims=True))
    alpha = jnp.exp(m - m_new)
    p = jnp.exp(s - m_new)
    l = alpha * l + jnp.sum(p, axis=0, keepdims=True)
    pv = lax.dot_general(vt, p.astype(BF16), DIMS_TN, preferred_element_type=F32)
    return m_new, l, alpha * acc + pv


def _flash_t_init(n):
    return (jnp.full((1, n), -1e30, F32), jnp.zeros((1, n), F32), jnp.zeros((LANES, n), F32))


def _diff_epilogue(l, acc, tq, lam_ref, sg, g, lam_init):
    o = (acc / l).T
    lp = lam_ref[...]
    lam = (jnp.exp(jnp.sum(lp[0:1] * lp[1:2], axis=-1, keepdims=True))
           - jnp.exp(jnp.sum(lp[2:3] * lp[3:4], axis=-1, keepdims=True)) + lam_init)
    o = o[:tq] - lam * o[tq:]
    o = o * lax.rsqrt(jnp.mean(o * o, axis=-1, keepdims=True) + SUBLN_EPS) * sg
    o = o * (1.0 - lam_init)
    return o * _silu(g)


A_HP = 8


def _attn_a_prompt_body(q_ref, k_ref, v_ref, g_ref, lam_ref, sg_ref, o_ref, kb_scr, vb_scr,
                        *, tq, lam_init):
    qi = pl.program_id(2)

    @pl.when(qi == 0)
    def _():
        kb_scr[...] = k_ref[...].astype(BF16)
        vb_scr[...] = v_ref[...].astype(BF16)

    hsl = [slice(h * LANES, (h + 1) * LANES) for h in range(A_HP)]
    qst = [_stack_maps_t(q_ref[:, sl]) for sl in hsl]
    n = 2 * tq
    krow = lax.broadcasted_iota(I32, (tq, n), 0)
    qcol = lax.broadcasted_iota(I32, (tq, n), 1) & (tq - 1)
    diag_mask = _chunk_of(krow) <= _chunk_of(qcol)

    def tile(kt, carries, masked):
        off = pl.multiple_of(kt * tq, tq)
        ss = [jnp.dot(kb_scr[pl.ds(off, tq), sl], qst[h], preferred_element_type=F32)
              for h, sl in enumerate(hsl)]
        out = []
        for h, sl in enumerate(hsl):
            s = jnp.where(diag_mask, ss[h], NEG_INF) if masked else ss[h]
            out.append(_flash_t_step(s, vb_scr[pl.ds(off, tq), sl], carries[h]))
        return tuple(out)

    carries = lax.fori_loop(0, qi, lambda kt, c: tile(kt, c, False),
                            tuple(_flash_t_init(n) for _ in hsl))
    carries = tile(qi, carries, True)
    for h, sl in enumerate(hsl):
        _, l, acc = carries[h]
        o_ref[:, sl] = _diff_epilogue(l, acc, tq, lam_ref, sg_ref[...], g_ref[:, sl],
                                      lam_init).astype(o_ref.dtype)


def _attn_a_prompt(q_rot, k_p, v_p, qkvg, lam_p, subln_g, lam_init, tq=256):
    nq = SEQ // tq
    w = A_HP * LANES
    nh = H_A // A_HP
    return pl.pallas_call(
        functools.partial(_attn_a_prompt_body, tq=tq, lam_init=lam_init),
        grid=(BATCH, nh, nq),
        in_specs=[pl.BlockSpec((tq, w), lambda b, h, i: (b * nq + i, h)),
                  pl.BlockSpec((SEQ, w), lambda b, h, i: (b, h)),
                  pl.BlockSpec((SEQ, w), lambda b, h, i: (b, h)),
                  pl.BlockSpec((tq, w), lambda b, h, i: (b * nq + i, 3 * nh + h)),
                  pl.BlockSpec((4, DH_A), lambda b, h, i: (0, 0)),
                  pl.BlockSpec((1, LANES), lambda b, h, i: (0, 0))],
        out_specs=pl.BlockSpec((tq, w), lambda b, h, i: (b * nq + i, h)),
        out_shape=jax.ShapeDtypeStruct((N_PROMPT, D_MODEL), OG_DTYPE),
        scratch_shapes=[pltpu.VMEM((SEQ, w), BF16), pltpu.VMEM((SEQ, w), BF16)],
        compiler_params=_cparams(("parallel", "parallel", "arbitrary")),
        name="attn_a_prompt",
    )(q_rot, k_p, v_p, qkvg, lam_p, subln_g.reshape(1, LANES))


def _attn_a_sample_body(q_ref, kp_ref, vp_ref, k_ref, v_ref, g_ref, lam_ref, sg_ref,
                        o_ref, qst_scr, ml_scr, acc_scr, *, tk, lam_init):
    kt = pl.program_id(1)
    tq = DEC_SEQ
    hsl = [slice(h * LANES, (h + 1) * LANES) for h in range(H_A)]

    @pl.when(kt == 0)
    def _():
        for h, sl in enumerate(hsl):
            qst_scr[h] = _stack_maps_t(q_ref[:, sl])
        m0, l0, acc0 = _flash_t_init(2 * tq)
        for h in range(H_A):
            ml_scr[h, 0:1, :] = m0
            ml_scr[h, 1:2, :] = l0
            acc_scr[h] = acc0

    def load(h):
        return ml_scr[h, 0:1, :], ml_scr[h, 1:2, :], acc_scr[h]

    def store(h, carry):
        ml_scr[h, 0:1, :], ml_scr[h, 1:2, :], acc_scr[h] = carry

    def head_group(i, _):
        hs = [A_HP * i + d for d in range(A_HP)]
        ss = [jnp.dot(kp_ref[pl.ds(h, tk, stride=H_A), :].astype(BF16), qst_scr[h],
                      preferred_element_type=F32) for h in hs]
        for h, s in zip(hs, ss):
            vb = vp_ref[pl.ds(h, tk, stride=H_A), :].astype(BF16)
            store(h, _flash_t_step(s, vb, load(h)))
        return 0

    lax.fori_loop(0, H_A // A_HP, head_group, 0)

    @pl.when(kt == pl.num_programs(1) - 1)
    def _():
        for h, sl in enumerate(hsl):
            s = jnp.dot(k_ref[:, sl].astype(BF16), qst_scr[h], preferred_element_type=F32)
            _, l, acc = _flash_t_step(s, v_ref[:, sl].astype(BF16), load(h))
            o_ref[:, sl] = _diff_epilogue(l, acc, tq, lam_ref, sg_ref[...], g_ref[:, sl],
                                          lam_init).astype(o_ref.dtype)


def _attn_a_sample(q_rot, k_s, v_s, qkvg, k_past, v_past, layer, lam_p, subln_g, lam_init, tk=512):
    r0 = N_PROMPT // DEC_SEQ
    nkt = PAST_LEN // tk
    new = lambda c: pl.BlockSpec((DEC_SEQ, D_MODEL), lambda b, t, c=c: (r0 + b, c))
    own = pl.BlockSpec((DEC_SEQ, D_MODEL), lambda b, t: (b, 0))
    past = pl.BlockSpec((tk * H_A, LANES), lambda b, t: ((layer * DEC_BATCH + b) * nkt + t, 0))
    return pl.pallas_call(
        functools.partial(_attn_a_sample_body, tk=tk, lam_init=lam_init),
        grid=(DEC_BATCH, nkt),
        in_specs=[new(0), past, past, own, own, new(3),
                  pl.BlockSpec((4, DH_A), lambda b, t: (0, 0)),
                  pl.BlockSpec((1, LANES), lambda b, t: (0, 0))],
        out_specs=own,
        out_shape=jax.ShapeDtypeStruct((N_SAMPLE, D_MODEL), OG_DTYPE),
        scratch_shapes=[pltpu.VMEM((H_A, LANES, 2 * DEC_SEQ), BF16),
                        pltpu.VMEM((H_A, 8, 2 * DEC_SEQ), F32),
                        pltpu.VMEM((H_A, LANES, 2 * DEC_SEQ), F32)],
        compiler_params=_cparams(("parallel", "arbitrary")),
        name="attn_a_sample",
    )(q_rot, k_past, v_past, k_s, v_s, qkvg, lam_p, subln_g.reshape(1, LANES))


def _rope_b_body(q_ref, k_ref, v_ref, qi_ref, sm_ref, c1_ref, s1_ref, c2_ref, s2_ref,
                 qo_ref, qio_ref, kid_ref, wi_ref, kp_ref, ks_ref, vp_ref, vs_ref, ip_ref, is_ref,
                 *, n_prompt_tiles):
    c1, s1, c2, s2 = c1_ref[...], s1_ref[...], c2_ref[...], s2_ref[...]
    for h in range(H_B):
        sl = slice(h * LANES, (h + 1) * LANES)
        qo_ref[:, sl] = _rope128(q_ref[:, sl], c2, s2) * (HD_B ** -0.5)
    for h in range(H_IDX * D_IDX // LANES):
        sl = slice(h * LANES, (h + 1) * LANES)
        qio_ref[:, sl] = _rope64(qi_ref[:, sl], c1, s1) * (D_IDX ** -0.5)
    sm = sm_ref[...]
    lane = lax.broadcasted_iota(I32, sm.shape, 1)
    kr = _rope64(sm, c1, s1)
    kid_ref[...] = jnp.where(lane < D_IDX, kr, pltpu.roll(kr, D_IDX, 1))
    wi_ref[...] = pltpu.roll(sm, D_IDX, 1) * (H_IDX ** -0.5)

    def emit(ko_ref, vo_ref, io_ref):
        for h in range(KV_B):
            sl = slice(h * LANES, (h + 1) * LANES)
            ko_ref[:, sl] = _rope128(k_ref[:, sl], c2, s2)
        vo_ref[...] = v_ref[...]
        io_ref[...] = kr[:, :D_IDX]

    _when_group(n_prompt_tiles, emit, (kp_ref, vp_ref, ip_ref), (ks_ref, vs_ref, is_ref))


def _rope_b(proj, cos1, sn1, cos2, sn2, tm=256):
    m = proj.shape[0]
    tab = pl.BlockSpec((tm, LANES), lambda i: (i, 0))
    blk = lambda w, c: pl.BlockSpec((tm, w), lambda i, c=c: (i, c))
    widths = (H_B * HD_B, H_IDX * D_IDX, LANES, LANES)
    kvw = KV_B * HD_B
    return pl.pallas_call(
        functools.partial(_rope_b_body, n_prompt_tiles=N_PROMPT // tm),
        grid=(m // tm,),
        in_specs=[blk(2048, 0), blk(kvw, B_K0 // kvw), blk(kvw, B_V0 // kvw),
                  blk(1024, B_QI0 // 1024), blk(LANES, B_KI0 // LANES), tab, tab, tab, tab],
        out_specs=([blk(w, 0) for w in widths] + _group_specs(tm, kvw) * 2
                   + _group_specs(tm, D_IDX)),
        out_shape=([jax.ShapeDtypeStruct((m, w), F32) for w in widths] + _group_shapes(kvw) * 2
                   + _group_shapes(D_IDX)),
        compiler_params=_cparams(("arbitrary",)),
        name="rope_b",
    )(proj, proj, proj, proj, proj, cos1, sn1, cos2, sn2)


POS_BITS = 13
DSA_BUCKET = 512


def _count(pred):
    return jnp.sum(jnp.where(pred, 1.0, 0.0), axis=-1, keepdims=True)


def _select_topk(scores, poss, n_sel):
    tq = scores[0].shape[0]
    n_sel = float(n_sel)

    def key_to_float(key):
        return lax.bitcast_convert_type(jnp.where(key < 0, key ^ 0x7FFFFFFF, key), F32)

    def thr_step(i, t):
        cand_bits = t | lax.shift_left(jnp.int32(1), 31 - i)
        cand = key_to_float(cand_bits ^ INT_MIN)
        cnt = sum(_count(s >= cand) for s in scores)
        return jnp.where(cnt >= n_sel, cand_bits, t)

    t = lax.fori_loop(0, 32, thr_step, jnp.zeros((tq, 1), I32))
    thr_key = t ^ INT_MIN
    thr, above = key_to_float(thr_key), key_to_float(thr_key + 1)
    above = jnp.where(thr == 0.0, F32_MIN_NORMAL, above)
    few = sum(_count(s > NEG_INF) for s in scores) <= n_sel
    thr = jnp.where(few, F32_LOWEST, thr)
    above = jnp.where(few, F32_LOWEST, above)
    gts = [s >= above for s in scores]
    eqs = [(s >= thr) & jnp.logical_not(g) for s, g in zip(scores, gts)]
    need = n_sel - sum(_count(g) for g in gts)
    tied = [jnp.where(e, 1.0, 0.0) for e in eqs]
    n_tied = sum(jnp.sum(e, axis=-1, keepdims=True) for e in tied)

    def tie_step(i, j):
        cand = j | lax.shift_left(jnp.int32(1), POS_BITS - 1 - i)
        cnt = sum(jnp.sum(jnp.where(p < cand, e, 0.0), axis=-1, keepdims=True)
                  for e, p in zip(tied, poss))
        return jnp.where(cnt < need, cand, j)

    j = lax.cond(jnp.max(n_tied - need) > 0.0,
                 lambda: lax.fori_loop(0, POS_BITS, tie_step, jnp.zeros((tq, 1), I32)),
                 lambda: jnp.full((tq, 1), (1 << POS_BITS) - 1, I32))
    return [g | (e & (p <= j)) for g, e, p in zip(gts, eqs, poss)]


def _dsa_body(*refs, tq, seg_lens, seg_pos0, q_pos0, causal, n_sel, bucket):
    nseg = len(seg_lens)
    n_in = 4 + 3 * nseg
    assert max(seg_pos0[i] + seg_lens[i] for i in range(nseg)) <= 1 << POS_BITS
    if bucket is None:
        _dsa_run(refs, tq, seg_lens, seg_pos0, q_pos0, causal, n_sel, n_in)
        return
    assert causal and nseg == 1 and seg_pos0[0] == 0 and q_pos0 == 0 and bucket % tq == 0
    last = (pl.program_id(1) * tq) // bucket
    for v in range(seg_lens[0] // bucket):
        pl.when(last == v)(functools.partial(_dsa_run, refs, tq, ((v + 1) * bucket,), seg_pos0,
                                             q_pos0, causal, n_sel, n_in))


def _dsa_run(refs, tq, seg_lens, seg_pos0, q_pos0, causal, n_sel, n_in):
    nseg = len(seg_lens)
    qi_ref, wi_ref, q_ref, g_ref = refs[:4]
    kid_refs = refs[4:4 + nseg]
    k_refs = refs[4 + nseg:4 + 2 * nseg]
    v_refs = refs[4 + 2 * nseg:4 + 3 * nseg]
    o_ref = refs[n_in]
    bias_refs = refs[n_in + 1:]
    n_rep = H_B // KV_B

    @pl.when(pl.program_id(2) == 0)
    def _():
        wi = wi_ref[...]
        qpos = q_pos0 + pl.program_id(1) * tq
        scores, poss, adms = [], [], []
        for si in range(nseg):
            sl = seg_lens[si]
            kid = kid_refs[si][0:sl, :].astype(BF16)
            sc = jnp.zeros((tq, sl), F32)
            for p in range(H_IDX // 2):
                qp = qi_ref[:, p * LANES:(p + 1) * LANES]
                lane = lax.broadcasted_iota(I32, qp.shape, 1)
                qs = jnp.concatenate([jnp.where(lane < D_IDX, qp, 0.0),
                                      jnp.where(lane >= D_IDX, qp, 0.0)], axis=0).astype(BF16)
                lg = jnp.maximum(_dot_nt(qs, kid), 0.0)
                sc = sc + wi[:, 2 * p:2 * p + 1] * lg[:tq] + wi[:, 2 * p + 1:2 * p + 2] * lg[tq:]
            kpos = seg_pos0[si] + lax.broadcasted_iota(I32, (tq, sl), 1)
            if causal:
                qrow = qpos + lax.broadcasted_iota(I32, (tq, sl), 0)
                adm = _chunk_of(kpos) <= _chunk_of(qrow)
                sc = jnp.where(adm, sc, NEG_INF)
            else:
                adm = None
            scores.append(sc)
            poss.append(kpos)
            adms.append(adm)
        sels = _select_topk(scores, poss, n_sel)
        for si in range(nseg):
            bias = jnp.where(sels[si], 0.0, NEG_INF)
            if adms[si] is not None:
                bias = jnp.where(adms[si], bias, NEG_INF)
            bias_refs[si][:, 0:seg_lens[si]] = bias

    q = q_ref[...]
    qg = jnp.concatenate([q[:, r * LANES:(r + 1) * LANES] for r in range(n_rep)],
                         axis=0).astype(BF16)
    ss = []
    for si in range(nseg):
        b = bias_refs[si][:, 0:seg_lens[si]]
        s = _dot_nt(qg, k_refs[si][0:seg_lens[si], :].astype(BF16))
        ss.append(s + jnp.concatenate([b] * n_rep, axis=0))
    m = functools.reduce(jnp.maximum, [jnp.max(s, axis=-1, keepdims=True) for s in ss])
    l = 0.0
    acc = 0.0
    for si in range(nseg):
        p = jnp.exp(ss[si] - m)
        l = l + jnp.sum(p, axis=-1, keepdims=True)
        acc = acc + jnp.dot(p.astype(BF16), v_refs[si][0:seg_lens[si], :].astype(BF16),
                            preferred_element_type=F32)
    o = acc / l
    o = jnp.concatenate([o[r * tq:(r + 1) * tq] for r in range(n_rep)], axis=1)
    o_ref[...] = (o * _silu(g_ref[...])).astype(o_ref.dtype)


def _dsa_prompt(q_rot, k_p, v_p, qi_rot, kid, wi, proj, tq=256):
    nq = SEQ // tq
    gw = KV_B * HD_B
    row = lambda w, c0: pl.BlockSpec((tq, w), lambda b, i, g, c0=c0: (b * nq + i, c0 + g))
    row0 = lambda w: pl.BlockSpec((tq, w), lambda b, i, g: (b * nq + i, 0))
    seq = lambda c0: pl.BlockSpec((SEQ, LANES), lambda b, i, g, c0=c0: (b, c0 + g))
    n_sel = min(TOPK_MAX, SEQ // 4)
    return pl.pallas_call(
        functools.partial(_dsa_body, tq=tq, seg_lens=(SEQ,), seg_pos0=(0,), q_pos0=0,
                          causal=True, n_sel=n_sel, bucket=DSA_BUCKET),
        grid=(BATCH, nq, KV_B),
        in_specs=[row0(H_IDX * D_IDX), row0(LANES), row(gw, 0), row(gw, B_G0 // gw),
                  pl.BlockSpec((SEQ, LANES), lambda b, i, g: (b, 0)),
                  seq(0), seq(0)],
        out_specs=row(gw, 0),
        out_shape=jax.ShapeDtypeStruct((N_PROMPT, D_MODEL), OG_DTYPE),
        scratch_shapes=[pltpu.VMEM((tq, SEQ), F32)],
        compiler_params=_cparams(("parallel", "parallel", "arbitrary")),
        name="dsa_prompt",
    )(qi_rot, wi, q_rot, proj, kid, k_p, v_p)


def _dsa_sample(q_rot, k_s, v_s, qi_rot, kid, wi, proj, k_past, v_past, kid_past, layer):
    tq = DEC_SEQ
    r0 = N_PROMPT // tq
    gw = KV_B * HD_B
    row = lambda w, c0: pl.BlockSpec((tq, w), lambda b, i, g, c0=c0: (r0 + b, c0 + g))
    row0 = lambda w: pl.BlockSpec((tq, w), lambda b, i, g: (r0 + b, 0))
    past = pl.BlockSpec((PAST_LEN, LANES), lambda b, i, g: (layer * DEC_BATCH + b, g))
    own = pl.BlockSpec((tq, LANES), lambda b, i, g: (b, g))
    s_all = PAST_LEN + DEC_SEQ
    n_sel = min(TOPK_MAX, s_all // 4)
    return pl.pallas_call(
        functools.partial(_dsa_body, tq=tq, seg_lens=(PAST_LEN, DEC_SEQ),
                          seg_pos0=(0, PAST_LEN), q_pos0=PAST_LEN, causal=False, n_sel=n_sel,
                          bucket=None),
        grid=(DEC_BATCH, 1, KV_B),
        in_specs=[row0(H_IDX * D_IDX), row0(LANES), row(gw, 0), row(gw, B_G0 // gw),
                  pl.BlockSpec((PAST_LEN, LANES), lambda b, i, g: (layer * DEC_BATCH + b, 0)),
                  row0(LANES),
                  past, own,
                  past, own],
        out_specs=pl.BlockSpec((tq, gw), lambda b, i, g: (b, g)),
        out_shape=jax.ShapeDtypeStruct((N_SAMPLE, D_MODEL), OG_DTYPE),
        scratch_shapes=[pltpu.VMEM((tq, PAST_LEN), F32), pltpu.VMEM((tq, DEC_SEQ), F32)],
        compiler_params=_cparams(("parallel", "arbitrary", "arbitrary")),
        name="dsa_sample",
    )(qi_rot, wi, q_rot, proj, kid_past, kid, k_past, k_s, v_past, v_s)


C_TM = 64
C_TILES_PER_SEQ = SEQ // C_TM
C_PROMPT_TILES = N_PROMPT // C_TM


def _c_seq_id(i):
    return jnp.where(i < C_PROMPT_TILES, i // C_TILES_PER_SEQ, i - C_PROMPT_TILES + BATCH)


def _c_prep_body(x_ref, g_ref, mu_ref, sh_ref, l_ref, hl_ref, carry):
    i = pl.program_id(0)

    @pl.when(i == 0)
    def _():
        carry[...] = jnp.zeros_like(carry)

    h = _rms(x_ref[...], g_ref[...])
    start = jnp.logical_or(i >= C_PROMPT_TILES, i % C_TILES_PER_SEQ == 0)
    first = jnp.where(start, sh_ref[0], carry[...])
    row = lax.broadcasted_iota(I32, h.shape, 0)
    prev = jnp.where(row == 0, first, pltpu.roll(h, 1, 0))
    last = h[C_TM - 1:C_TM, :]
    carry[...] = last
    hl_ref[0] = last
    d = prev - h
    for n in range(6):
        l_ref[n] = (h + d * mu_ref[n:n + 1, :]).astype(BF16)


def _c_prep(x, g, mu, shift0):
    m, d = x.shape
    nseq = shift0.shape[0]
    return pl.pallas_call(
        _c_prep_body,
        grid=(m // C_TM,),
        in_specs=[pl.BlockSpec((C_TM, d), lambda i: (i, 0)),
                  pl.BlockSpec((1, d), lambda i: (0, 0)),
                  pl.BlockSpec((6, d), lambda i: (0, 0)),
                  pl.BlockSpec((1, 1, d), lambda i: (_c_seq_id(i), 0, 0))],
        out_specs=[pl.BlockSpec((6, C_TM, d), lambda i: (0, i, 0)),
                   pl.BlockSpec((1, 1, d), lambda i: (_c_seq_id(i), 0, 0))],
        out_shape=[jax.ShapeDtypeStruct((6, m, d), BF16),
                   jax.ShapeDtypeStruct((nseq, 1, d), F32)],
        scratch_shapes=[pltpu.VMEM((1, d), F32)],
        compiler_params=_cparams(("arbitrary",)),
        name="c_prep",
    )(x, g.reshape(1, d), mu, shift0.reshape(nseq, 1, d))


def _bmm_body(l_ref, w_ref, o_ref):
    o_ref[0] = jnp.dot(l_ref[0], w_ref[0], preferred_element_type=F32)


def _c_bmm(lerp, w, layer, tm=PROJ_TM, tn=1024):
    _, m, d = lerp.shape
    nb, n = 4, w.shape[2]
    return pl.pallas_call(
        _bmm_body,
        grid=(nb, m // tm, n // tn),
        in_specs=[pl.BlockSpec((1, tm, d), lambda b, i, j: (b, i, 0)),
                  pl.BlockSpec((1, d, tn), lambda b, i, j: (layer * nb + b, 0, j))],
        out_specs=pl.BlockSpec((1, tm, tn), lambda b, i, j: (b, i, j)),
        out_shape=jax.ShapeDtypeStruct((nb, m, n), F32),
        compiler_params=_cparams(("parallel", "parallel", "arbitrary")),
        name="c_bmm",
    )(lerp, w)


def _c_lora_body(l4_ref, l5_ref, wla_ref, wlb_ref, ala_ref, alb_ref, w0_ref, a0_ref,
                 wl_ref, al_ref):
    tw = jnp.tanh(jnp.dot(l4_ref[0], wla_ref[...], preferred_element_type=F32))
    wl_ref[...] = w0_ref[...] + jnp.dot(tw.astype(BF16), wlb_ref[...],
                                        preferred_element_type=F32)
    ta = jnp.dot(l5_ref[0], ala_ref[...], preferred_element_type=F32)
    al_ref[...] = a0_ref[...] + jnp.dot(ta.astype(BF16), alb_ref[...],
                                        preferred_element_type=F32)


def _c_lora(lerp, w_la, w_lb, a_la, a_lb, w0, a0, tm=512):
    _, m, d = lerp.shape
    pad_in = lambda w: jnp.pad(w, ((0, 0), (0, LANES - w.shape[1]))).astype(BF16)
    pad_out = lambda w: jnp.pad(w, ((0, LANES - w.shape[0]), (0, 0))).astype(BF16)
    lin = lambda n: pl.BlockSpec((1, tm, d), lambda i, n=n: (n, i, 0))
    win = pl.BlockSpec((d, LANES), lambda i: (0, 0))
    wout = pl.BlockSpec((LANES, d), lambda i: (0, 0))
    vec = pl.BlockSpec((1, d), lambda i: (0, 0))
    out = pl.BlockSpec((tm, d), lambda i: (i, 0))
    return pl.pallas_call(
        _c_lora_body,
        grid=(m // tm,),
        in_specs=[lin(4), lin(5), win, wout, win, wout, vec, vec],
        out_specs=[out, out],
        out_shape=[jax.ShapeDtypeStruct((m, d), F32)] * 2,
        compiler_params=_cparams(("parallel",)),
        name="c_lora",
    )(lerp, lerp, pad_in(w_la), pad_out(w_lb), pad_in(a_la), pad_out(a_lb),
      w0.reshape(1, d), a0.reshape(1, d))


C_HB = 32
DIMS_NN = (((1,), (0,)), ((), ()))
DIMS_NT = (((1,), (1,)), ((), ()))
DIMS_TN = (((0,), (0,)), ((), ()))


def _dot1(a, b, dims):
    return lax.dot_general(a.astype(BF16), b.astype(BF16), dims, preferred_element_type=F32)


def _dot3(a, b, dims):
    (ca,), (cb,) = dims[0]
    ah = a.astype(BF16).astype(F32)
    bh = b.astype(BF16).astype(F32)
    sa = jnp.concatenate([ah, a - ah, ah], axis=ca).astype(BF16)
    sb = jnp.concatenate([bh, bh, b - bh], axis=cb).astype(BF16)
    return lax.dot_general(sa, sb, dims, preferred_element_type=F32)


def _cumsum_rows(x):
    row = lax.broadcasted_iota(I32, x.shape, 0)
    sh = 1
    while sh < x.shape[0]:
        x = x + jnp.where(row >= sh, pltpu.roll(x, sh, 0), 0.0)
        sh *= 2
    return x


def _rwkv_body(*refs, has_init):
    if has_init:
        (r_ref, k_ref, v_ref, g_ref, wl_ref, al_ref, kk_ref, ka_ref, rk_ref, lnw_ref, lnb_ref,
         s0_ref, og_ref, so_ref, s_scr) = refs
    else:
        (r_ref, k_ref, v_ref, g_ref, wl_ref, al_ref, kk_ref, ka_ref, rk_ref, lnw_ref, lnb_ref,
         og_ref, so_ref, s_scr) = refs
    c = pl.program_id(2)
    n = CHUNK
    hs = HS_C

    @pl.when(c == 0)
    def _():
        if has_init:
            s_scr[...] = s0_ref[0]
        else:
            s_scr[...] = jnp.zeros_like(s_scr)

    r, k, v, g = r_ref[0], k_ref[0], v_ref[0], g_ref[0]
    logw = -math.exp(-0.5) * _sigmoid(wl_ref[...])
    a = _sigmoid(al_ref[...])
    kkr = k * kk_ref[...]
    k2 = k * (1.0 + (a - 1.0) * ka_ref[...])
    bonus_in = r * k2 * rk_ref[...]

    cs_all = _cumsum_rows(logw)
    ec_all, eci_all, ecp_all = jnp.exp(cs_all), jnp.exp(-cs_all), jnp.exp(cs_all - logw)

    ti = lax.broadcasted_iota(I32, (n, n), 0)
    tj = lax.broadcasted_iota(I32, (n, n), 1)
    eye = jnp.where(tj == ti, 1.0, 0.0)
    gi = lax.broadcasted_iota(I32, (2 * n, 2 * n), 0)
    gj = lax.broadcasted_iota(I32, (2 * n, 2 * n), 1) & (n - 1)
    gmask = ((gi < n) & (gj < gi)) | ((gi >= n) & (gj <= gi - n))
    lvl_masks = []
    bs = 1
    while bs < n:
        sh = bs.bit_length()
        lvl_masks.append((jnp.right_shift(ti, sh) == jnp.right_shift(tj, sh))
                         & ((ti & bs) != 0) & ((tj & bs) == 0))
        bs *= 2

    heads = range(C_HB)
    sls = [slice(j * hs, (j + 1) * hs) for j in heads]
    vs = [v[:, sl] for sl in sls]
    zero = jnp.zeros_like(vs[0])
    gw = 4 * hs
    ei = lax.broadcasted_iota(I32, (gw, gw), 0)
    ej = lax.broadcasted_iota(I32, (gw, gw), 1)
    lg = hs.bit_length() - 1
    seg = jnp.where(jnp.right_shift(ei, lg) == jnp.right_shift(ej, lg), 1.0, 0.0).astype(BF16)
    sq = kkr * kkr
    sq_hi = sq.astype(BF16)
    sq_lo = (sq - sq_hi.astype(F32)).astype(BF16)
    ssq = jnp.concatenate(
        [jnp.dot(sq_hi[:, c:c + gw], seg, preferred_element_type=F32)
         + jnp.dot(sq_lo[:, c:c + gw], seg, preferred_element_type=F32)
         for c in range(0, C_HB * hs, gw)], axis=1)
    kkn = kkr / jnp.maximum(jnp.sqrt(ssq), 1e-12)
    p_all, q_all = kkn * ecp_all, kkn * a * eci_all
    kt_all, rt_all = k2 * eci_all, r * ec_all
    p_, q_ = [p_all[:, sl] for sl in sls], [q_all[:, sl] for sl in sls]
    kt, rt = [kt_all[:, sl] for sl in sls], [rt_all[:, sl] for sl in sls]
    gm = [jnp.where(gmask, _dot1(jnp.concatenate([p_[j], rt[j]], axis=0),
                                 jnp.concatenate([q_[j], kt[j]], axis=0), DIMS_NT), 0.0)
          for j in heads]
    av = [_dot1(gm[j][:n], jnp.concatenate([zero, vs[j]], axis=0), DIMS_NN)
          for j in heads]
    a_qp = [gm[j][:n, :n] for j in heads]
    x = [eye - jnp.where(lvl_masks[0], a_qp[j], 0.0) for j in heads]
    for msk in lvl_masks[1:]:
        ax = [_dot1(jnp.where(msk, a_qp[j], 0.0), x[j], DIMS_NN) for j in heads]
        x = [x[j] - _dot1(x[j], ax[j], DIMS_NN) for j in heads]
    xpw = [_dot1(x[j], jnp.concatenate([p_[j], av[j]], axis=1), DIMS_NN)
           for j in heads]
    low = [jnp.concatenate([zero, vs[j]], axis=1) for j in heads]
    tb = [_dot3(jnp.concatenate([xpw[j], low[j]], axis=0),
                jnp.concatenate([-q_[j], kt[j]], axis=0), DIMS_TN) for j in heads]
    ro = [_dot1(gm[j][n:], jnp.concatenate([-xpw[j], low[j]], axis=0), DIMS_NN) for j in heads]
    s0 = [s_scr[j] for j in heads]
    o = [_dot1(rt[j] + ro[j][:, :hs], s0[j], DIMS_NT) + ro[j][:, hs:] for j in heads]
    for j in heads:
        ecl = ec_all[n - 1:n, sls[j]]
        s_scr[j] = _dot3(s0[j], (eye + tb[j][:hs]) * ecl, DIMS_NN) + tb[j][hs:] * ecl
    outs = []
    for j in heads:
        mean = jnp.mean(o[j], axis=-1, keepdims=True)
        var = jnp.mean(jnp.square(o[j] - mean), axis=-1, keepdims=True)
        on = (o[j] - mean) * lax.rsqrt(var + GN_EPS) * lnw_ref[:, sls[j]] + lnb_ref[:, sls[j]]
        bonus = jnp.sum(bonus_in[:, sls[j]], axis=-1, keepdims=True) * vs[j]
        outs.append(on + bonus)
    og_ref[...] = (jnp.concatenate(outs, axis=1) * _silu(g)).astype(og_ref.dtype)

    @pl.when(c == pl.num_programs(2) - 1)
    def _():
        so_ref[0] = s_scr[...]


def _rwkv(rkvg, wl, al, k_k, k_a, r_k, ln_w, ln_b, s0, *, nseq, nchunk, row0):
    w = C_HB * HS_C
    tok = lambda n: pl.BlockSpec((1, CHUNK, w), lambda s, h, c, n=n: (n, row0 + s * nchunk + c, h))
    tok2 = pl.BlockSpec((CHUNK, w), lambda s, h, c: (row0 + s * nchunk + c, h))
    vec = pl.BlockSpec((1, w), lambda s, h, c: (0, h))
    st = pl.BlockSpec((1, C_HB, HS_C, HS_C), lambda s, h, c: (s, h, 0, 0))
    has_init = s0 is not None
    ins = [rkvg, rkvg, rkvg, rkvg, wl, al] + [p.reshape(1, D_MODEL) for p in (k_k, k_a, r_k, ln_w, ln_b)]
    specs = [tok(0), tok(1), tok(2), tok(3), tok2, tok2] + [vec] * 5
    if has_init:
        ins.append(s0)
        specs.append(st)
    return pl.pallas_call(
        functools.partial(_rwkv_body, has_init=has_init),
        grid=(nseq, H_C // C_HB, nchunk),
        in_specs=specs,
        out_specs=[pl.BlockSpec((CHUNK, w), lambda s, h, c: (s * nchunk + c, h)), st],
        out_shape=[jax.ShapeDtypeStruct((nseq * nchunk * CHUNK, D_MODEL), OG_DTYPE),
                   jax.ShapeDtypeStruct((nseq, H_C, HS_C, HS_C), F32)],
        scratch_shapes=[pltpu.VMEM((C_HB, HS_C, HS_C), F32)],
        compiler_params=_cparams(("parallel", "parallel", "arbitrary")),
        name="rwkv_scan",
    )(*ins)


def kernel(x_prompt, x_sample, cache_a_k, cache_a_v, cache_b_k, cache_b_v, cache_b_kidx, state_c_wkv, state_c_shift, norm_g, final_g, w_out, a_w_in, a_lam, a_subln_g, b_w_in, c_mu, c_w_rkvg, c_w0, c_w_la, c_w_lb, c_a0, c_a_la, c_a_lb, c_k_k, c_k_a, c_r_k, c_ln_w, c_ln_b):
    x = jnp.concatenate([x_prompt.reshape(N_PROMPT, D_MODEL),
                         x_sample.reshape(N_SAMPLE, D_MODEL)], axis=0)
    cos1, sn1, cos2, sn2 = _rope_tables()
    a_w = _to_bf16(a_w_in)
    c_w = _to_bf16(c_w_rkvg).reshape(-1, D_MODEL, D_MODEL)
    o_w = _to_bf16(w_out)
    outs = {n: [] for n in ("akp", "avp", "aks", "avs", "bkp", "bvp", "bip", "bks", "bvs", "bis",
                            "cwp", "chp", "cws", "chs")}
    for i in range(DEPTH):
        kind, j = i % N_MIXERS, i // N_MIXERS
        if kind == 0:
            lam_init = 0.8 - 0.6 * math.exp(-0.3 * i)
            qkvg = _norm_proj(x, norm_g[i], a_w, j, PROJ_TM, 1024)
            q_rot, kp, ks, vp, vs = _rope_a(qkvg, cos1, sn1)
            og_p = _attn_a_prompt(q_rot, kp, vp, qkvg, a_lam[j], a_subln_g[j], lam_init)
            og_s = _attn_a_sample(q_rot, ks, vs, qkvg,
                                  cache_a_k.reshape(-1, LANES), cache_a_v.reshape(-1, LANES), j,
                                  a_lam[j], a_subln_g[j], lam_init)
            outs["akp"].append(kp.reshape(BATCH, SEQ, H_A, 2 * DH_A))
            outs["avp"].append(vp.reshape(BATCH, SEQ, H_A, 2 * DH_A))
            outs["aks"].append(ks.reshape(DEC_BATCH, DEC_SEQ, H_A, 2 * DH_A))
            outs["avs"].append(vs.reshape(DEC_BATCH, DEC_SEQ, H_A, 2 * DH_A))
        elif kind == 1:
            w = b_w_in[j]
            w = jnp.concatenate([w[:, :4096], w[:, 4176:], w[:, 4096:4176],
                                 jnp.zeros((D_MODEL, B_COLS - w.shape[1]), w.dtype)], axis=1)
            proj = _norm_proj(x, norm_g[i], w.astype(BF16), 0, PROJ_TM, 896)
            q_rot, qi_rot, kid, wi, kp, ks, vp, vs, ip, is_ = _rope_b(proj, cos1, sn1, cos2, sn2)
            og_p = _dsa_prompt(q_rot, kp, vp, qi_rot, kid, wi, proj)
            kidx_past = cache_b_kidx.reshape(-1, D_IDX)
            og_s = _dsa_sample(q_rot, ks, vs, qi_rot, kid, wi, proj,
                               cache_b_k.reshape(-1, KV_B * HD_B), cache_b_v.reshape(-1, KV_B * HD_B),
                               jnp.concatenate([kidx_past, kidx_past], axis=1), j)
            outs["bkp"].append(kp.reshape(BATCH, SEQ, KV_B, HD_B))
            outs["bvp"].append(vp.reshape(BATCH, SEQ, KV_B, HD_B))
            outs["bip"].append(ip.reshape(BATCH, SEQ, D_IDX))
            outs["bks"].append(ks.reshape(DEC_BATCH, DEC_SEQ, KV_B, HD_B))
            outs["bvs"].append(vs.reshape(DEC_BATCH, DEC_SEQ, KV_B, HD_B))
            outs["bis"].append(is_.reshape(DEC_BATCH, DEC_SEQ, D_IDX))
        else:
            shift0 = jnp.concatenate([jnp.zeros((BATCH, D_MODEL), F32), state_c_shift[j]], axis=0)
            lerp, hlast = _c_prep(x, norm_g[i], c_mu[j], shift0)
            rkvg = _c_bmm(lerp, c_w, j)
            wl, al = _c_lora(lerp, c_w_la[j], c_w_lb[j], c_a_la[j], c_a_lb[j], c_w0[j], c_a0[j])
            par = (c_k_k[j], c_k_a[j], c_r_k[j], c_ln_w[j], c_ln_b[j])
            og_p, st_p = _rwkv(rkvg, wl, al, *par, None, nseq=BATCH, nchunk=SEQ // CHUNK, row0=0)
            og_s, st_s = _rwkv(rkvg, wl, al, *par, state_c_wkv[j], nseq=DEC_BATCH, nchunk=1,
                               row0=N_PROMPT // CHUNK)
            outs["cwp"].append(st_p)
            outs["chp"].append(hlast[:BATCH, 0])
            outs["cws"].append(st_s)
            outs["chs"].append(hlast[BATCH:, 0])
        if i == DEPTH - 1:
            yp, ys = _proj_out(og_p, og_s, x, o_w, i, final_g)
        else:
            x = _proj_out(og_p, og_s, x, o_w, i)
    st = lambda n: jnp.stack(outs[n])
    return (yp.reshape(BATCH, SEQ, D_MODEL), ys.reshape(DEC_BATCH, DEC_SEQ, D_MODEL),
            st("akp"), st("avp"), st("aks"), st("avs"),
            st("bkp"), st("bvp"), st("bip"), st("bks"), st("bvs"), st("bis"),
            st("cwp"), st("chp"), st("cws"), st("chs"))
```

```python
import functools
import math

import jax
import jax.numpy as jnp
from jax import lax
from jax.experimental import pallas as pl
from jax.experimental.pallas import tpu as pltpu

F32 = jnp.float32
BF16 = jnp.bfloat16
I32 = jnp.int32

D_MODEL = 2048
BATCH = 4
SEQ = 2048
DEPTH = 4
DEC_BATCH = 8
DEC_SEQ = 64
PAST_LEN = 4096
CHUNK = 64
N_MIXERS = 3
ROPE_THETA = 10000.0
RMS_EPS = 1e-6
DH_A = 64
H_A = 16
SUBLN_EPS = 1e-5
HD_B = 128
H_B = 16
KV_B = 4
H_IDX = 16
D_IDX = 64
TOPK_MAX = 256
HS_C = 64
H_C = 32
R_DECAY = 96
R_ICLR = 96
GN_EPS = 64e-5

N_PROMPT = BATCH * SEQ
N_SAMPLE = DEC_BATCH * DEC_SEQ
M_ROWS = N_PROMPT + N_SAMPLE
LANES = 128
VMEM_LIMIT = 56 * 1024 * 1024
PROJ_TM = M_ROWS // 8
OG_DTYPE = jnp.bfloat16
NEG_INF = float("-inf")
INT_MIN = -2 ** 31
F32_MIN_NORMAL = float.fromhex("0x1p-126")
F32_LOWEST = -float.fromhex("0x1.fffffep127")

B_Q0, B_K0, B_V0, B_QI0, B_G0, B_KI0 = 0, 2048, 2560, 3072, 4096, 6144
B_COLS = 6272

assert PAST_LEN % CHUNK == 0 and DEC_SEQ == CHUNK
LOG2_CHUNK = CHUNK.bit_length() - 1
assert 1 << LOG2_CHUNK == CHUNK


def _chunk_of(pos):
    return jnp.right_shift(pos, LOG2_CHUNK)


def _cparams(sem):
    return pltpu.CompilerParams(dimension_semantics=sem, vmem_limit_bytes=VMEM_LIMIT)


def _sigmoid(x):
    return 1.0 / (1.0 + jnp.exp(-x))


def _silu(x):
    return x * _sigmoid(x)


def _rms(x, g):
    return x * lax.rsqrt(jnp.mean(x * x, axis=-1, keepdims=True) + RMS_EPS) * g


def _dot_nt(a, b, precision=None):
    return lax.dot_general(a, b, (((1,), (1,)), ((), ())), precision=precision,
                           preferred_element_type=F32)


def _norm_proj_body(x_ref, g_ref, w_ref, o_ref, h_scr):
    @pl.when(pl.program_id(1) == 0)
    def _():
        h_scr[...] = _rms(x_ref[...], g_ref[...]).astype(BF16)
    o_ref[...] = jnp.dot(h_scr[...], w_ref[...], preferred_element_type=F32)


def _cast_body(x_ref, o_ref):
    o_ref[...] = x_ref[...].astype(o_ref.dtype)


def _to_bf16(w, tr=1024, tc=2048):
    w2 = w.reshape(-1, w.shape[-1])
    r, c = w2.shape
    blk = pl.BlockSpec((tr, tc), lambda i, j: (i, j))
    return pl.pallas_call(
        _cast_body,
        grid=(r // tr, c // tc),
        in_specs=[blk],
        out_specs=blk,
        out_shape=jax.ShapeDtypeStruct((r, c), BF16),
        compiler_params=_cparams(("parallel", "parallel")),
        name="to_bf16",
    )(w2)


def _norm_proj(x, g, w, layer, tm, tn):
    m, d = x.shape
    n = w.shape[1]
    return pl.pallas_call(
        _norm_proj_body,
        grid=(m // tm, n // tn),
        in_specs=[pl.BlockSpec((tm, d), lambda i, j: (i, 0)),
                  pl.BlockSpec((1, d), lambda i, j: (0, 0)),
                  pl.BlockSpec((d, tn), lambda i, j: (layer, j))],
        out_specs=pl.BlockSpec((tm, tn), lambda i, j: (i, j)),
        out_shape=jax.ShapeDtypeStruct((m, n), F32),
        scratch_shapes=[pltpu.VMEM((tm, d), BF16)],
        compiler_params=_cparams(("parallel", "arbitrary")),
        name="norm_proj",
    )(x, g.reshape(1, d), w)


def _proj_out_body(ogp_ref, ogs_ref, x_ref, w_ref, o_ref, *, n_prompt_tiles):
    def emit(og_ref):
        o_ref[...] = x_ref[...] + jnp.dot(og_ref[...].astype(BF16), w_ref[...],
                                          preferred_element_type=F32)

    _when_group(n_prompt_tiles, emit, (ogp_ref,), (ogs_ref,))


def _proj_out_final_body(ogp_ref, ogs_ref, x_ref, w_ref, g_ref, yp_ref, ys_ref, *, n_prompt_tiles):
    def emit(og_ref, y_ref):
        xn = x_ref[...] + jnp.dot(og_ref[...].astype(BF16), w_ref[...], preferred_element_type=F32)
        y_ref[...] = _rms(xn, g_ref[...])

    _when_group(n_prompt_tiles, emit, (ogp_ref, yp_ref), (ogs_ref, ys_ref))


def _proj_out(og_prompt, og_sample, x, w, layer, final_g=None, tm=512):
    m, d = x.shape
    row = pl.BlockSpec((tm, d), lambda i: (i, 0))
    full = pl.BlockSpec((d, d), lambda i: (layer, 0))
    in_specs = _group_specs(tm, d) + [row, full]
    if final_g is None:
        body, extra, out_specs = _proj_out_body, (), row
        out_shape = jax.ShapeDtypeStruct((m, d), F32)
    else:
        body, extra = _proj_out_final_body, (final_g.reshape(1, d),)
        in_specs.append(pl.BlockSpec((1, d), lambda i: (0, 0)))
        out_specs, out_shape = _group_specs(tm, d), _group_shapes(d)
    return pl.pallas_call(
        functools.partial(body, n_prompt_tiles=N_PROMPT // tm),
        grid=(m // tm,),
        in_specs=in_specs,
        out_specs=out_specs,
        out_shape=out_shape,
        compiler_params=_cparams(("arbitrary",)),
        name="proj_out",
    )(og_prompt, og_sample, x, w, *extra)


def _rope_tables():
    pos = jnp.concatenate([jnp.tile(jnp.arange(SEQ), BATCH),
                           jnp.tile(PAST_LEN + jnp.arange(DEC_SEQ), DEC_BATCH)]).astype(F32)

    def table(dh, reps):
        half = dh // 2
        inv = jnp.power(ROPE_THETA, -jnp.arange(half, dtype=F32) * (2.0 / dh))
        ang = pos[:, None] * inv[None, :]
        cos, sin = jnp.cos(ang), jnp.sin(ang)
        return (jnp.tile(jnp.concatenate([cos, cos], axis=1), (1, reps)),
                jnp.tile(jnp.concatenate([-sin, sin], axis=1), (1, reps)))

    cos1, sn1 = table(64, 2)
    cos2, sn2 = table(128, 1)
    return cos1, sn1, cos2, sn2


def _rope64(x, cos, sn):
    lane = lax.broadcasted_iota(I32, x.shape, 1)
    partner = jnp.where((lane & 63) < 32, pltpu.roll(x, 96, 1), pltpu.roll(x, 32, 1))
    return x * cos + partner * sn


def _rope128(x, cos, sn):
    return x * cos + pltpu.roll(x, 64, 1) * sn


def _group_specs(tm, width):
    npt = N_PROMPT // tm
    return [pl.BlockSpec((tm, width), lambda i: (jnp.minimum(i, npt - 1), 0)),
            pl.BlockSpec((tm, width), lambda i: (jnp.maximum(i - npt, 0), 0))]


def _group_shapes(width):
    return [jax.ShapeDtypeStruct((N_PROMPT, width), F32), jax.ShapeDtypeStruct((N_SAMPLE, width), F32)]


def _when_group(n_prompt_tiles, emit, prompt_refs, sample_refs):
    is_prompt = pl.program_id(0) < n_prompt_tiles
    pl.when(is_prompt)(lambda: emit(*prompt_refs))
    pl.when(jnp.logical_not(is_prompt))(lambda: emit(*sample_refs))


def _rope_a_body(q_ref, k_ref, v_ref, cos_ref, sn_ref, qo_ref, kp_ref, ks_ref, vp_ref, vs_ref,
                 *, n_prompt_tiles):
    cos, sn = cos_ref[...], sn_ref[...]
    hsl = [slice(h * LANES, (h + 1) * LANES) for h in range(H_A)]
    for sl in hsl:
        qo_ref[:, sl] = _rope64(q_ref[:, sl], cos, sn) * (DH_A ** -0.5)

    def emit(ko_ref, vo_ref):
        for sl in hsl:
            ko_ref[:, sl] = _rope64(k_ref[:, sl], cos, sn)
        vo_ref[...] = v_ref[...]

    _when_group(n_prompt_tiles, emit, (kp_ref, vp_ref), (ks_ref, vs_ref))


def _rope_a(qkvg, cos1, sn1, tm=256):
    m = qkvg.shape[0]
    blk = lambda c: pl.BlockSpec((tm, D_MODEL), lambda i, c=c: (i, c))
    tab = pl.BlockSpec((tm, LANES), lambda i: (i, 0))
    return pl.pallas_call(
        functools.partial(_rope_a_body, n_prompt_tiles=N_PROMPT // tm),
        grid=(m // tm,),
        in_specs=[blk(0), blk(1), blk(2), tab, tab],
        out_specs=[blk(0)] + _group_specs(tm, D_MODEL) * 2,
        out_shape=[jax.ShapeDtypeStruct((m, D_MODEL), F32)] + _group_shapes(D_MODEL) * 2,
        compiler_params=_cparams(("arbitrary",)),
        name="rope_a",
    )(qkvg, qkvg, qkvg, cos1, sn1)


def _stack_maps_t(q):
    lane = lax.broadcasted_iota(I32, q.shape, 1)
    qs = jnp.concatenate([jnp.where(lane < DH_A, q, 0.0), jnp.where(lane >= DH_A, q, 0.0)], axis=0)
    return qs.T.astype(BF16)


def _flash_t_step(s, vt, carry):
    m, l, acc = carry
    m_new = jnp.maximum(m, jnp.max(s, axis=0, keepdims=True))
    alpha = jnp.exp(m - m_new)
    p = jnp.exp(s - m_new)
    l = alpha * l + jnp.sum(p, axis=0, keepdims=True)
    pv = lax.dot_general(vt, p.astype(BF16), DIMS_TN, preferred_element_type=F32)
    return m_new, l, alpha * acc + pv


def _flash_t_init(n):
    return (jnp.full((1, n), -1e30, F32), jnp.zeros((1, n), F32), jnp.zeros((LANES, n), F32))


def _diff_epilogue(l, acc, tq, lam_ref, sg, g, lam_init):
    o = (acc / l).T
    lp = lam_ref[...]
    lam = (jnp.exp(jnp.sum(lp[0:1] * lp[1:2], axis=-1, keepdims=True))
           - jnp.exp(jnp.sum(lp[2:3] * lp[3:4], axis=-1, keepdims=True)) + lam_init)
    o = o[:tq] - lam * o[tq:]
    o = o * lax.rsqrt(jnp.mean(o * o, axis=-1, keepdims=True) + SUBLN_EPS) * sg
    o = o * (1.0 - lam_init)
    return o * _silu(g)


A_HP = 8


def _attn_a_prompt_body(q_ref, k_ref, v_ref, g_ref, lam_ref, sg_ref, o_ref, kb_scr, vb_scr,
                        *, tq, lam_init):
    qi = pl.program_id(2)

    @pl.when(qi == 0)
    def _():
        kb_scr[...] = k_ref[...].astype(BF16)
        vb_scr[...] = v_ref[...].astype(BF16)

    hsl = [slice(h * LANES, (h + 1) * LANES) for h in range(A_HP)]
    qst, vsl = [], []
    for sl in hsl:
        qt = _stack_maps_t(q_ref[:, sl])
        qst += [qt[:, :tq], qt[:, tq:]]
        vsl += [sl, sl]
    krow = lax.broadcasted_iota(I32, (tq, tq), 0)
    qcol = lax.broadcasted_iota(I32, (tq, tq), 1)
    diag_mask = _chunk_of(krow) <= _chunk_of(qcol)

    def tile(kt, carries, masked):
        off = pl.multiple_of(kt * tq, tq)
        score = lambda i: jnp.dot(kb_scr[pl.ds(off, tq), vsl[i]], qst[i],
                                  preferred_element_type=F32)
        ahead = len(vsl)
        ss = [score(i) for i in range(ahead)]
        out = []
        for i, sl in enumerate(vsl):
            if i + ahead < len(vsl):
                ss.append(score(i + ahead))
            s = jnp.where(diag_mask, ss[i], NEG_INF) if masked else ss[i]
            out.append(_flash_t_step(s, vb_scr[pl.ds(off, tq), sl], carries[i]))
        return tuple(out)

    carries = lax.fori_loop(0, qi, lambda kt, c: tile(kt, c, False),
                            tuple(_flash_t_init(tq) for _ in vsl))
    carries = tile(qi, carries, True)
    for h, sl in enumerate(hsl):
        (_, l1, acc1), (_, l2, acc2) = carries[2 * h], carries[2 * h + 1]
        l = jnp.concatenate([l1, l2], axis=1)
        acc = jnp.concatenate([acc1, acc2], axis=1)
        o_ref[:, sl] = _diff_epilogue(l, acc, tq, lam_ref, sg_ref[...], g_ref[:, sl],
                                      lam_init).astype(o_ref.dtype)


def _attn_a_prompt(q_rot, k_p, v_p, qkvg, lam_p, subln_g, lam_init, tq=256):
    nq = SEQ // tq
    w = A_HP * LANES
    nh = H_A // A_HP
    return pl.pallas_call(
        functools.partial(_attn_a_prompt_body, tq=tq, lam_init=lam_init),
        grid=(BATCH, nh, nq),
        in_specs=[pl.BlockSpec((tq, w), lambda b, h, i: (b * nq + i, h)),
                  pl.BlockSpec((SEQ, w), lambda b, h, i: (b, h)),
                  pl.BlockSpec((SEQ, w), lambda b, h, i: (b, h)),
                  pl.BlockSpec((tq, w), lambda b, h, i: (b * nq + i, 3 * nh + h)),
                  pl.BlockSpec((4, DH_A), lambda b, h, i: (0, 0)),
                  pl.BlockSpec((1, LANES), lambda b, h, i: (0, 0))],
        out_specs=pl.BlockSpec((tq, w), lambda b, h, i: (b * nq + i, h)),
        out_shape=jax.ShapeDtypeStruct((N_PROMPT, D_MODEL), OG_DTYPE),
        scratch_shapes=[pltpu.VMEM((SEQ, w), BF16), pltpu.VMEM((SEQ, w), BF16)],
        compiler_params=_cparams(("parallel", "parallel", "arbitrary")),
        name="attn_a_prompt",
    )(q_rot, k_p, v_p, qkvg, lam_p, subln_g.reshape(1, LANES))


def _attn_a_sample_body(q_ref, kp_ref, vp_ref, k_ref, v_ref, g_ref, lam_ref, sg_ref,
                        o_ref, qst_scr, ml_scr, acc_scr, *, tk, lam_init):
    kt = pl.program_id(1)
    tq = DEC_SEQ
    hsl = [slice(h * LANES, (h + 1) * LANES) for h in range(H_A)]

    @pl.when(kt == 0)
    def _():
        for h, sl in enumerate(hsl):
            qst_scr[h] = _stack_maps_t(q_ref[:, sl])
        m0, l0, acc0 = _flash_t_init(2 * tq)
        for h in range(H_A):
            ml_scr[h, 0:1, :] = m0
            ml_scr[h, 1:2, :] = l0
            acc_scr[h] = acc0

    def load(h):
        return ml_scr[h, 0:1, :], ml_scr[h, 1:2, :], acc_scr[h]

    def store(h, carry):
        ml_scr[h, 0:1, :], ml_scr[h, 1:2, :], acc_scr[h] = carry

    def head_group(i, _):
        hs = [A_HP * i + d for d in range(A_HP)]
        ss = [jnp.dot(kp_ref[pl.ds(h, tk, stride=H_A), :].astype(BF16), qst_scr[h],
                      preferred_element_type=F32) for h in hs]
        for h, s in zip(hs, ss):
            vb = vp_ref[pl.ds(h, tk, stride=H_A), :].astype(BF16)
            store(h, _flash_t_step(s, vb, load(h)))
        return 0

    lax.fori_loop(0, H_A // A_HP, head_group, 0)

    @pl.when(kt == pl.num_programs(1) - 1)
    def _():
        for h, sl in enumerate(hsl):
            s = jnp.dot(k_ref[:, sl].astype(BF16), qst_scr[h], preferred_element_type=F32)
            _, l, acc = _flash_t_step(s, v_ref[:, sl].astype(BF16), load(h))
            o_ref[:, sl] = _diff_epilogue(l, acc, tq, lam_ref, sg_ref[...], g_ref[:, sl],
                                          lam_init).astype(o_ref.dtype)


def _attn_a_sample(q_rot, k_s, v_s, qkvg, k_past, v_past, layer, lam_p, subln_g, lam_init, tk=512):
    r0 = N_PROMPT // DEC_SEQ
    nkt = PAST_LEN // tk
    new = lambda c: pl.BlockSpec((DEC_SEQ, D_MODEL), lambda b, t, c=c: (r0 + b, c))
    own = pl.BlockSpec((DEC_SEQ, D_MODEL), lambda b, t: (b, 0))
    past = pl.BlockSpec((tk * H_A, LANES), lambda b, t: ((layer * DEC_BATCH + b) * nkt + t, 0))
    return pl.pallas_call(
        functools.partial(_attn_a_sample_body, tk=tk, lam_init=lam_init),
        grid=(DEC_BATCH, nkt),
        in_specs=[new(0), past, past, own, own, new(3),
                  pl.BlockSpec((4, DH_A), lambda b, t: (0, 0)),
                  pl.BlockSpec((1, LANES), lambda b, t: (0, 0))],
        out_specs=own,
        out_shape=jax.ShapeDtypeStruct((N_SAMPLE, D_MODEL), OG_DTYPE),
        scratch_shapes=[pltpu.VMEM((H_A, LANES, 2 * DEC_SEQ), BF16),
                        pltpu.VMEM((H_A, 8, 2 * DEC_SEQ), F32),
                        pltpu.VMEM((H_A, LANES, 2 * DEC_SEQ), F32)],
        compiler_params=_cparams(("parallel", "arbitrary")),
        name="attn_a_sample",
    )(q_rot, k_past, v_past, k_s, v_s, qkvg, lam_p, subln_g.reshape(1, LANES))


def _rope_b_body(q_ref, k_ref, v_ref, qi_ref, sm_ref, c1_ref, s1_ref, c2_ref, s2_ref,
                 qo_ref, qio_ref, kid_ref, wi_ref, kp_ref, ks_ref, vp_ref, vs_ref, ip_ref, is_ref,
                 *, n_prompt_tiles):
    c1, s1, c2, s2 = c1_ref[...], s1_ref[...], c2_ref[...], s2_ref[...]
    for h in range(H_B):
        sl = slice(h * LANES, (h + 1) * LANES)
        qo_ref[:, sl] = _rope128(q_ref[:, sl], c2, s2) * (HD_B ** -0.5)
    for h in range(H_IDX * D_IDX // LANES):
        sl = slice(h * LANES, (h + 1) * LANES)
        qio_ref[:, sl] = _rope64(qi_ref[:, sl], c1, s1) * (D_IDX ** -0.5)
    sm = sm_ref[...]
    lane = lax.broadcasted_iota(I32, sm.shape, 1)
    kr = _rope64(sm, c1, s1)
    kid_ref[...] = jnp.where(lane < D_IDX, kr, pltpu.roll(kr, D_IDX, 1))
    wi_ref[...] = pltpu.roll(sm, D_IDX, 1) * (H_IDX ** -0.5)

    def emit(ko_ref, vo_ref, io_ref):
        for h in range(KV_B):
            sl = slice(h * LANES, (h + 1) * LANES)
            ko_ref[:, sl] = _rope128(k_ref[:, sl], c2, s2)
        vo_ref[...] = v_ref[...]
        io_ref[...] = kr[:, :D_IDX]

    _when_group(n_prompt_tiles, emit, (kp_ref, vp_ref, ip_ref), (ks_ref, vs_ref, is_ref))


def _rope_b(proj, cos1, sn1, cos2, sn2, tm=256):
    m = proj.shape[0]
    tab = pl.BlockSpec((tm, LANES), lambda i: (i, 0))
    blk = lambda w, c: pl.BlockSpec((tm, w), lambda i, c=c: (i, c))
    widths = (H_B * HD_B, H_IDX * D_IDX, LANES, LANES)
    kvw = KV_B * HD_B
    return pl.pallas_call(
        functools.partial(_rope_b_body, n_prompt_tiles=N_PROMPT // tm),
        grid=(m // tm,),
        in_specs=[blk(2048, 0), blk(kvw, B_K0 // kvw), blk(kvw, B_V0 // kvw),
                  blk(1024, B_QI0 // 1024), blk(LANES, B_KI0 // LANES), tab, tab, tab, tab],
        out_specs=([blk(w, 0) for w in widths] + _group_specs(tm, kvw) * 2
                   + _group_specs(tm, D_IDX)),
        out_shape=([jax.ShapeDtypeStruct((m, w), F32) for w in widths] + _group_shapes(kvw) * 2
                   + _group_shapes(D_IDX)),
        compiler_params=_cparams(("arbitrary",)),
        name="rope_b",
    )(proj, proj, proj, proj, proj, cos1, sn1, cos2, sn2)


POS_BITS = 13
DSA_BUCKET = 512


def _count(pred):
    return jnp.sum(jnp.where(pred, 1.0, 0.0), axis=-1, keepdims=True)


def _select_topk(scores, poss, n_sel):
    tq = scores[0].shape[0]
    n_sel = float(n_sel)

    def key_to_float(key):
        return lax.bitcast_convert_type(jnp.where(key < 0, key ^ 0x7FFFFFFF, key), F32)

    def thr_step(i, t):
        cand_bits = t | lax.shift_left(jnp.int32(1), 31 - i)
        cand = key_to_float(cand_bits ^ INT_MIN)
        cnt = sum(_count(s >= cand) for s in scores)
        return jnp.where(cnt >= n_sel, cand_bits, t)

    t = lax.fori_loop(0, 32, thr_step, jnp.zeros((tq, 1), I32))
    thr_key = t ^ INT_MIN
    thr, above = key_to_float(thr_key), key_to_float(thr_key + 1)
    above = jnp.where(thr == 0.0, F32_MIN_NORMAL, above)
    few = sum(_count(s > NEG_INF) for s in scores) <= n_sel
    thr = jnp.where(few, F32_LOWEST, thr)
    above = jnp.where(few, F32_LOWEST, above)
    gts = [s >= above for s in scores]
    eqs = [(s >= thr) & jnp.logical_not(g) for s, g in zip(scores, gts)]
    need = n_sel - sum(_count(g) for g in gts)
    tied = [jnp.where(e, 1.0, 0.0) for e in eqs]
    n_tied = sum(jnp.sum(e, axis=-1, keepdims=True) for e in tied)

    def tie_step(i, j):
        cand = j | lax.shift_left(jnp.int32(1), POS_BITS - 1 - i)
        cnt = sum(jnp.sum(jnp.where(p < cand, e, 0.0), axis=-1, keepdims=True)
                  for e, p in zip(tied, poss))
        return jnp.where(cnt < need, cand, j)

    j = lax.cond(jnp.max(n_tied - need) > 0.0,
                 lambda: lax.fori_loop(0, POS_BITS, tie_step, jnp.zeros((tq, 1), I32)),
                 lambda: jnp.full((tq, 1), (1 << POS_BITS) - 1, I32))
    return [g | (e & (p <= j)) for g, e, p in zip(gts, eqs, poss)]


def _dsa_body(*refs, tq, seg_lens, seg_pos0, q_pos0, causal, n_sel, bucket):
    nseg = len(seg_lens)
    n_in = 4 + 3 * nseg
    assert max(seg_pos0[i] + seg_lens[i] for i in range(nseg)) <= 1 << POS_BITS
    if bucket is None:
        _dsa_run(refs, tq, seg_lens, seg_pos0, q_pos0, causal, n_sel, n_in)
        return
    assert causal and nseg == 1 and seg_pos0[0] == 0 and q_pos0 == 0 and bucket % tq == 0
    last = (pl.program_id(1) * tq) // bucket
    for v in range(seg_lens[0] // bucket):
        pl.when(last == v)(functools.partial(_dsa_run, refs, tq, ((v + 1) * bucket,), seg_pos0,
                                             q_pos0, causal, n_sel, n_in))


def _dsa_run(refs, tq, seg_lens, seg_pos0, q_pos0, causal, n_sel, n_in):
    nseg = len(seg_lens)
    qi_ref, wi_ref, q_ref, g_ref = refs[:4]
    kid_refs = refs[4:4 + nseg]
    k_refs = refs[4 + nseg:4 + 2 * nseg]
    v_refs = refs[4 + 2 * nseg:4 + 3 * nseg]
    o_ref = refs[n_in]
    bias_refs = refs[n_in + 1:]
    n_rep = H_B // KV_B

    @pl.when(pl.program_id(2) == 0)
    def _():
        wi = wi_ref[...]
        qpos = q_pos0 + pl.program_id(1) * tq
        scores, poss, adms = [], [], []
        for si in range(nseg):
            sl = seg_lens[si]
            kid = kid_refs[si][0:sl, :].astype(BF16)
            sc = jnp.zeros((tq, sl), F32)
            for p in range(H_IDX // 2):
                qp = qi_ref[:, p * LANES:(p + 1) * LANES]
                lane = lax.broadcasted_iota(I32, qp.shape, 1)
                qs = jnp.concatenate([jnp.where(lane < D_IDX, qp, 0.0),
                                      jnp.where(lane >= D_IDX, qp, 0.0)], axis=0).astype(BF16)
                lg = jnp.maximum(_dot_nt(qs, kid), 0.0)
                sc = sc + wi[:, 2 * p:2 * p + 1] * lg[:tq] + wi[:, 2 * p + 1:2 * p + 2] * lg[tq:]
            kpos = seg_pos0[si] + lax.broadcasted_iota(I32, (tq, sl), 1)
            if causal:
                qrow = qpos + lax.broadcasted_iota(I32, (tq, sl), 0)
                adm = _chunk_of(kpos) <= _chunk_of(qrow)
                sc = jnp.where(adm, sc, NEG_INF)
            else:
                adm = None
            scores.append(sc)
            poss.append(kpos)
            adms.append(adm)
        sels = _select_topk(scores, poss, n_sel)
        for si in range(nseg):
            bias = jnp.where(sels[si], 0.0, NEG_INF)
            if adms[si] is not None:
                bias = jnp.where(adms[si], bias, NEG_INF)
            bias_refs[si][:, 0:seg_lens[si]] = bias

    q = q_ref[...]
    qg = jnp.concatenate([q[:, r * LANES:(r + 1) * LANES] for r in range(n_rep)],
                         axis=0).astype(BF16)
    ss = []
    for si in range(nseg):
        b = bias_refs[si][:, 0:seg_lens[si]]
        s = _dot_nt(qg, k_refs[si][0:seg_lens[si], :].astype(BF16))
        ss.append(s + jnp.concatenate([b] * n_rep, axis=0))
    m = functools.reduce(jnp.maximum, [jnp.max(s, axis=-1, keepdims=True) for s in ss])
    l = 0.0
    acc = 0.0
    for si in range(nseg):
        p = jnp.exp(ss[si] - m)
        l = l + jnp.sum(p, axis=-1, keepdims=True)
        acc = acc + jnp.dot(p.astype(BF16), v_refs[si][0:seg_lens[si], :].astype(BF16),
                            preferred_element_type=F32)
    o = acc / l
    o = jnp.concatenate([o[r * tq:(r + 1) * tq] for r in range(n_rep)], axis=1)
    o_ref[...] = (o * _silu(g_ref[...])).astype(o_ref.dtype)


def _dsa_prompt(q_rot, k_p, v_p, qi_rot, kid, wi, proj, tq=256):
    nq = SEQ // tq
    gw = KV_B * HD_B
    row = lambda w, c0: pl.BlockSpec((tq, w), lambda b, i, g, c0=c0: (b * nq + i, c0 + g))
    row0 = lambda w: pl.BlockSpec((tq, w), lambda b, i, g: (b * nq + i, 0))
    seq = lambda c0: pl.BlockSpec((SEQ, LANES), lambda b, i, g, c0=c0: (b, c0 + g))
    n_sel = min(TOPK_MAX, SEQ // 4)
    return pl.pallas_call(
        functools.partial(_dsa_body, tq=tq, seg_lens=(SEQ,), seg_pos0=(0,), q_pos0=0,
                          causal=True, n_sel=n_sel, bucket=DSA_BUCKET),
        grid=(BATCH, nq, KV_B),
        in_specs=[row0(H_IDX * D_IDX), row0(LANES), row(gw, 0), row(gw, B_G0 // gw),
                  pl.BlockSpec((SEQ, LANES), lambda b, i, g: (b, 0)),
                  seq(0), seq(0)],
        out_specs=row(gw, 0),
        out_shape=jax.ShapeDtypeStruct((N_PROMPT, D_MODEL), OG_DTYPE),
        scratch_shapes=[pltpu.VMEM((tq, SEQ), F32)],
        compiler_params=_cparams(("parallel", "parallel", "arbitrary")),
        name="dsa_prompt",
    )(qi_rot, wi, q_rot, proj, kid, k_p, v_p)


def _dsa_sample(q_rot, k_s, v_s, qi_rot, kid, wi, proj, k_past, v_past, kid_past, layer):
    tq = DEC_SEQ
    r0 = N_PROMPT // tq
    gw = KV_B * HD_B
    row = lambda w, c0: pl.BlockSpec((tq, w), lambda b, i, g, c0=c0: (r0 + b, c0 + g))
    row0 = lambda w: pl.BlockSpec((tq, w), lambda b, i, g: (r0 + b, 0))
    past = pl.BlockSpec((PAST_LEN, LANES), lambda b, i, g: (layer * DEC_BATCH + b, g))
    own = pl.BlockSpec((tq, LANES), lambda b, i, g: (b, g))
    s_all = PAST_LEN + DEC_SEQ
    n_sel = min(TOPK_MAX, s_all // 4)
    return pl.pallas_call(
        functools.partial(_dsa_body, tq=tq, seg_lens=(PAST_LEN, DEC_SEQ),
                          seg_pos0=(0, PAST_LEN), q_pos0=PAST_LEN, causal=False, n_sel=n_sel,
                          bucket=None),
        grid=(DEC_BATCH, 1, KV_B),
        in_specs=[row0(H_IDX * D_IDX), row0(LANES), row(gw, 0), row(gw, B_G0 // gw),
                  pl.BlockSpec((PAST_LEN, LANES), lambda b, i, g: (layer * DEC_BATCH + b, 0)),
                  row0(LANES),
                  past, own,
                  past, own],
        out_specs=pl.BlockSpec((tq, gw), lambda b, i, g: (b, g)),
        out_shape=jax.ShapeDtypeStruct((N_SAMPLE, D_MODEL), OG_DTYPE),
        scratch_shapes=[pltpu.VMEM((tq, PAST_LEN), F32), pltpu.VMEM((tq, DEC_SEQ), F32)],
        compiler_params=_cparams(("parallel", "arbitrary", "arbitrary")),
        name="dsa_sample",
    )(qi_rot, wi, q_rot, proj, kid_past, kid, k_past, k_s, v_past, v_s)


C_TM = 64
C_TILES_PER_SEQ = SEQ // C_TM
C_PROMPT_TILES = N_PROMPT // C_TM


def _c_seq_id(i):
    return jnp.where(i < C_PROMPT_TILES, i // C_TILES_PER_SEQ, i - C_PROMPT_TILES + BATCH)


def _c_prep_body(x_ref, g_ref, mu_ref, sh_ref, l_ref, hl_ref, carry):
    i = pl.program_id(0)

    @pl.when(i == 0)
    def _():
        carry[...] = jnp.zeros_like(carry)

    h = _rms(x_ref[...], g_ref[...])
    start = jnp.logical_or(i >= C_PROMPT_TILES, i % C_TILES_PER_SEQ == 0)
    first = jnp.where(start, sh_ref[0], carry[...])
    row = lax.broadcasted_iota(I32, h.shape, 0)
    prev = jnp.where(row == 0, first, pltpu.roll(h, 1, 0))
    last = h[C_TM - 1:C_TM, :]
    carry[...] = last
    hl_ref[0] = last
    d = prev - h
    for n in range(6):
        l_ref[n] = (h + d * mu_ref[n:n + 1, :]).astype(BF16)


def _c_prep(x, g, mu, shift0):
    m, d = x.shape
    nseq = shift0.shape[0]
    return pl.pallas_call(
        _c_prep_body,
        grid=(m // C_TM,),
        in_specs=[pl.BlockSpec((C_TM, d), lambda i: (i, 0)),
                  pl.BlockSpec((1, d), lambda i: (0, 0)),
                  pl.BlockSpec((6, d), lambda i: (0, 0)),
                  pl.BlockSpec((1, 1, d), lambda i: (_c_seq_id(i), 0, 0))],
        out_specs=[pl.BlockSpec((6, C_TM, d), lambda i: (0, i, 0)),
                   pl.BlockSpec((1, 1, d), lambda i: (_c_seq_id(i), 0, 0))],
        out_shape=[jax.ShapeDtypeStruct((6, m, d), BF16),
                   jax.ShapeDtypeStruct((nseq, 1, d), F32)],
        scratch_shapes=[pltpu.VMEM((1, d), F32)],
        compiler_params=_cparams(("arbitrary",)),
        name="c_prep",
    )(x, g.reshape(1, d), mu, shift0.reshape(nseq, 1, d))


def _bmm_body(l_ref, w_ref, o_ref):
    o_ref[0] = jnp.dot(l_ref[0], w_ref[0], preferred_element_type=F32)


def _c_bmm(lerp, w, layer, tm=PROJ_TM, tn=1024):
    _, m, d = lerp.shape
    nb, n = 4, w.shape[2]
    return pl.pallas_call(
        _bmm_body,
        grid=(nb, m // tm, n // tn),
        in_specs=[pl.BlockSpec((1, tm, d), lambda b, i, j: (b, i, 0)),
                  pl.BlockSpec((1, d, tn), lambda b, i, j: (layer * nb + b, 0, j))],
        out_specs=pl.BlockSpec((1, tm, tn), lambda b, i, j: (b, i, j)),
        out_shape=jax.ShapeDtypeStruct((nb, m, n), F32),
        compiler_params=_cparams(("parallel", "parallel", "arbitrary")),
        name="c_bmm",
    )(lerp, w)


def _c_lora_body(l4_ref, l5_ref, wla_ref, wlb_ref, ala_ref, alb_ref, w0_ref, a0_ref,
                 wl_ref, al_ref):
    tw = jnp.tanh(jnp.dot(l4_ref[0], wla_ref[...], preferred_element_type=F32))
    wl_ref[...] = w0_ref[...] + jnp.dot(tw.astype(BF16), wlb_ref[...],
                                        preferred_element_type=F32)
    ta = jnp.dot(l5_ref[0], ala_ref[...], preferred_element_type=F32)
    al_ref[...] = a0_ref[...] + jnp.dot(ta.astype(BF16), alb_ref[...],
                                        preferred_element_type=F32)


def _c_lora(lerp, w_la, w_lb, a_la, a_lb, w0, a0, tm=512):
    _, m, d = lerp.shape
    pad_in = lambda w: jnp.pad(w, ((0, 0), (0, LANES - w.shape[1]))).astype(BF16)
    pad_out = lambda w: jnp.pad(w, ((0, LANES - w.shape[0]), (0, 0))).astype(BF16)
    lin = lambda n: pl.BlockSpec((1, tm, d), lambda i, n=n: (n, i, 0))
    win = pl.BlockSpec((d, LANES), lambda i: (0, 0))
    wout = pl.BlockSpec((LANES, d), lambda i: (0, 0))
    vec = pl.BlockSpec((1, d), lambda i: (0, 0))
    out = pl.BlockSpec((tm, d), lambda i: (i, 0))
    return pl.pallas_call(
        _c_lora_body,
        grid=(m // tm,),
        in_specs=[lin(4), lin(5), win, wout, win, wout, vec, vec],
        out_specs=[out, out],
        out_shape=[jax.ShapeDtypeStruct((m, d), F32)] * 2,
        compiler_params=_cparams(("parallel",)),
        name="c_lora",
    )(lerp, lerp, pad_in(w_la), pad_out(w_lb), pad_in(a_la), pad_out(a_lb),
      w0.reshape(1, d), a0.reshape(1, d))


C_HB = 32
DIMS_NN = (((1,), (0,)), ((), ()))
DIMS_NT = (((1,), (1,)), ((), ()))
DIMS_TN = (((0,), (0,)), ((), ()))


def _dot1(a, b, dims):
    return lax.dot_general(a.astype(BF16), b.astype(BF16), dims, preferred_element_type=F32)


def _dot3(a, b, dims):
    (ca,), (cb,) = dims[0]
    ah = a.astype(BF16).astype(F32)
    bh = b.astype(BF16).astype(F32)
    sa = jnp.concatenate([ah, a - ah, ah], axis=ca).astype(BF16)
    sb = jnp.concatenate([bh, bh, b - bh], axis=cb).astype(BF16)
    return lax.dot_general(sa, sb, dims, preferred_element_type=F32)


def _cumsum_rows(x):
    row = lax.broadcasted_iota(I32, x.shape, 0)
    sh = 1
    while sh < x.shape[0]:
        x = x + jnp.where(row >= sh, pltpu.roll(x, sh, 0), 0.0)
        sh *= 2
    return x


def _rwkv_body(*refs, has_init):
    if has_init:
        (r_ref, k_ref, v_ref, g_ref, wl_ref, al_ref, kk_ref, ka_ref, rk_ref, lnw_ref, lnb_ref,
         s0_ref, og_ref, so_ref, s_scr) = refs
    else:
        (r_ref, k_ref, v_ref, g_ref, wl_ref, al_ref, kk_ref, ka_ref, rk_ref, lnw_ref, lnb_ref,
         og_ref, so_ref, s_scr) = refs
    c = pl.program_id(2)
    n = CHUNK
    hs = HS_C

    @pl.when(c == 0)
    def _():
        if has_init:
            s_scr[...] = s0_ref[0]
        else:
            s_scr[...] = jnp.zeros_like(s_scr)

    r, k, v, g = r_ref[0], k_ref[0], v_ref[0], g_ref[0]
    logw = -math.exp(-0.5) * _sigmoid(wl_ref[...])
    a = _sigmoid(al_ref[...])
    kkr = k * kk_ref[...]
    k2 = k * (1.0 + (a - 1.0) * ka_ref[...])
    bonus_in = r * k2 * rk_ref[...]

    cs_all = _cumsum_rows(logw)
    ec_all, eci_all, ecp_all = jnp.exp(cs_all), jnp.exp(-cs_all), jnp.exp(cs_all - logw)

    ti = lax.broadcasted_iota(I32, (n, n), 0)
    tj = lax.broadcasted_iota(I32, (n, n), 1)
    eye = jnp.where(tj == ti, 1.0, 0.0)
    gi = lax.broadcasted_iota(I32, (2 * n, 2 * n), 0)
    gj = lax.broadcasted_iota(I32, (2 * n, 2 * n), 1) & (n - 1)
    gmask = ((gi < n) & (gj < gi)) | ((gi >= n) & (gj <= gi - n))
    lvl_masks = []
    bs = 1
    while bs < n:
        sh = bs.bit_length()
        lvl_masks.append((jnp.right_shift(ti, sh) == jnp.right_shift(tj, sh))
                         & ((ti & bs) != 0) & ((tj & bs) == 0))
        bs *= 2

    heads = range(C_HB)
    sls = [slice(j * hs, (j + 1) * hs) for j in heads]
    vs = [v[:, sl] for sl in sls]
    zero = jnp.zeros_like(vs[0])
    gw = 4 * hs
    ei = lax.broadcasted_iota(I32, (gw, gw), 0)
    ej = lax.broadcasted_iota(I32, (gw, gw), 1)
    lg = hs.bit_length() - 1
    seg = jnp.where(jnp.right_shift(ei, lg) == jnp.right_shift(ej, lg), 1.0, 0.0).astype(BF16)
    sq = kkr * kkr
    sq_hi = sq.astype(BF16)
    sq_lo = (sq - sq_hi.astype(F32)).astype(BF16)
    ssq = jnp.concatenate(
        [jnp.dot(sq_hi[:, c:c + gw], seg, preferred_element_type=F32)
         + jnp.dot(sq_lo[:, c:c + gw], seg, preferred_element_type=F32)
         for c in range(0, C_HB * hs, gw)], axis=1)
    kkn = kkr / jnp.maximum(jnp.sqrt(ssq), 1e-12)
    p_all, q_all = kkn * ecp_all, kkn * a * eci_all
    kt_all, rt_all = k2 * eci_all, r * ec_all
    p_, q_ = [p_all[:, sl] for sl in sls], [q_all[:, sl] for sl in sls]
    kt, rt = [kt_all[:, sl] for sl in sls], [rt_all[:, sl] for sl in sls]
    gm = [jnp.where(gmask, _dot1(jnp.concatenate([p_[j], rt[j]], axis=0),
                                 jnp.concatenate([q_[j], kt[j]], axis=0), DIMS_NT), 0.0)
          for j in heads]
    av = [_dot1(gm[j][:n], jnp.concatenate([zero, vs[j]], axis=0), DIMS_NN)
          for j in heads]
    a_qp = [gm[j][:n, :n] for j in heads]
    x = [eye - jnp.where(lvl_masks[0], a_qp[j], 0.0) for j in heads]
    for msk in lvl_masks[1:]:
        ax = [_dot1(jnp.where(msk, a_qp[j], 0.0), x[j], DIMS_NN) for j in heads]
        x = [x[j] - _dot1(x[j], ax[j], DIMS_NN) for j in heads]
    xpw = [_dot1(x[j], jnp.concatenate([p_[j], av[j]], axis=1), DIMS_NN)
           for j in heads]
    low = [jnp.concatenate([zero, vs[j]], axis=1) for j in heads]
    tb = [_dot3(jnp.concatenate([xpw[j], low[j]], axis=0),
                jnp.concatenate([-q_[j], kt[j]], axis=0), DIMS_TN) for j in heads]
    ro = [_dot1(gm[j][n:], jnp.concatenate([-xpw[j], low[j]], axis=0), DIMS_NN) for j in heads]
    s0 = [s_scr[j] for j in heads]
    o = [_dot1(rt[j] + ro[j][:, :hs], s0[j], DIMS_NT) + ro[j][:, hs:] for j in heads]
    for j in heads:
        ecl = ec_all[n - 1:n, sls[j]]
        s_scr[j] = _dot3(s0[j], (eye + tb[j][:hs]) * ecl, DIMS_NN) + tb[j][hs:] * ecl
    outs = []
    for j in heads:
        mean = jnp.mean(o[j], axis=-1, keepdims=True)
        var = jnp.mean(jnp.square(o[j] - mean), axis=-1, keepdims=True)
        on = (o[j] - mean) * lax.rsqrt(var + GN_EPS) * lnw_ref[:, sls[j]] + lnb_ref[:, sls[j]]
        bonus = jnp.sum(bonus_in[:, sls[j]], axis=-1, keepdims=True) * vs[j]
        outs.append(on + bonus)
    og_ref[...] = (jnp.concatenate(outs, axis=1) * _silu(g)).astype(og_ref.dtype)

    @pl.when(c == pl.num_programs(2) - 1)
    def _():
        so_ref[0] = s_scr[...]


def _rwkv(rkvg, wl, al, k_k, k_a, r_k, ln_w, ln_b, s0, *, nseq, nchunk, row0):
    w = C_HB * HS_C
    tok = lambda n: pl.BlockSpec((1, CHUNK, w), lambda s, h, c, n=n: (n, row0 + s * nchunk + c, h))
    tok2 = pl.BlockSpec((CHUNK, w), lambda s, h, c: (row0 + s * nchunk + c, h))
    vec = pl.BlockSpec((1, w), lambda s, h, c: (0, h))
    st = pl.BlockSpec((1, C_HB, HS_C, HS_C), lambda s, h, c: (s, h, 0, 0))
    has_init = s0 is not None
    ins = [rkvg, rkvg, rkvg, rkvg, wl, al] + [p.reshape(1, D_MODEL) for p in (k_k, k_a, r_k, ln_w, ln_b)]
    specs = [tok(0), tok(1), tok(2), tok(3), tok2, tok2] + [vec] * 5
    if has_init:
        ins.append(s0)
        specs.append(st)
    return pl.pallas_call(
        functools.partial(_rwkv_body, has_init=has_init),
        grid=(nseq, H_C // C_HB, nchunk),
        in_specs=specs,
        out_specs=[pl.BlockSpec((CHUNK, w), lambda s, h, c: (s * nchunk + c, h)), st],
        out_shape=[jax.ShapeDtypeStruct((nseq * nchunk * CHUNK, D_MODEL), OG_DTYPE),
                   jax.ShapeDtypeStruct((nseq, H_C, HS_C, HS_C), F32)],
        scratch_shapes=[pltpu.VMEM((C_HB, HS_C, HS_C), F32)],
        compiler_params=_cparams(("parallel", "parallel", "arbitrary")),
        name="rwkv_scan",
    )(*ins)


def kernel(x_prompt, x_sample, cache_a_k, cache_a_v, cache_b_k, cache_b_v, cache_b_kidx, state_c_wkv, state_c_shift, norm_g, final_g, w_out, a_w_in, a_lam, a_subln_g, b_w_in, c_mu, c_w_rkvg, c_w0, c_w_la, c_w_lb, c_a0, c_a_la, c_a_lb, c_k_k, c_k_a, c_r_k, c_ln_w, c_ln_b):
    x = jnp.concatenate([x_prompt.reshape(N_PROMPT, D_MODEL),
                         x_sample.reshape(N_SAMPLE, D_MODEL)], axis=0)
    cos1, sn1, cos2, sn2 = _rope_tables()
    a_w = _to_bf16(a_w_in)
    c_w = _to_bf16(c_w_rkvg).reshape(-1, D_MODEL, D_MODEL)
    o_w = _to_bf16(w_out)
    outs = {n: [] for n in ("akp", "avp", "aks", "avs", "bkp", "bvp", "bip", "bks", "bvs", "bis",
                            "cwp", "chp", "cws", "chs")}
    for i in range(DEPTH):
        kind, j = i % N_MIXERS, i // N_MIXERS
        if kind == 0:
            lam_init = 0.8 - 0.6 * math.exp(-0.3 * i)
            qkvg = _norm_proj(x, norm_g[i], a_w, j, PROJ_TM, 1024)
            q_rot, kp, ks, vp, vs = _rope_a(qkvg, cos1, sn1)
            og_p = _attn_a_prompt(q_rot, kp, vp, qkvg, a_lam[j], a_subln_g[j], lam_init)
            og_s = _attn_a_sample(q_rot, ks, vs, qkvg,
                                  cache_a_k.reshape(-1, LANES), cache_a_v.reshape(-1, LANES), j,
                                  a_lam[j], a_subln_g[j], lam_init)
            outs["akp"].append(kp.reshape(BATCH, SEQ, H_A, 2 * DH_A))
            outs["avp"].append(vp.reshape(BATCH, SEQ, H_A, 2 * DH_A))
            outs["aks"].append(ks.reshape(DEC_BATCH, DEC_SEQ, H_A, 2 * DH_A))
            outs["avs"].append(vs.reshape(DEC_BATCH, DEC_SEQ, H_A, 2 * DH_A))
        elif kind == 1:
            w = b_w_in[j]
            w = jnp.concatenate([w[:, :4096], w[:, 4176:], w[:, 4096:4176],
                                 jnp.zeros((D_MODEL, B_COLS - w.shape[1]), w.dtype)], axis=1)
            proj = _norm_proj(x, norm_g[i], w.astype(BF16), 0, PROJ_TM, 896)
            q_rot, qi_rot, kid, wi, kp, ks, vp, vs, ip, is_ = _rope_b(proj, cos1, sn1, cos2, sn2)
            og_p = _dsa_prompt(q_rot, kp, vp, qi_rot, kid, wi, proj)
            kidx_past = cache_b_kidx.reshape(-1, D_IDX)
            og_s = _dsa_sample(q_rot, ks, vs, qi_rot, kid, wi, proj,
                               cache_b_k.reshape(-1, KV_B * HD_B), cache_b_v.reshape(-1, KV_B * HD_B),
                               jnp.concatenate([kidx_past, kidx_past], axis=1), j)
            outs["bkp"].append(kp.reshape(BATCH, SEQ, KV_B, HD_B))
            outs["bvp"].append(vp.reshape(BATCH, SEQ, KV_B, HD_B))
            outs["bip"].append(ip.reshape(BATCH, SEQ, D_IDX))
            outs["bks"].append(ks.reshape(DEC_BATCH, DEC_SEQ, KV_B, HD_B))
            outs["bvs"].append(vs.reshape(DEC_BATCH, DEC_SEQ, KV_B, HD_B))
            outs["bis"].append(is_.reshape(DEC_BATCH, DEC_SEQ, D_IDX))
        else:
            shift0 = jnp.concatenate([jnp.zeros((BATCH, D_MODEL), F32), state_c_shift[j]], axis=0)
            lerp, hlast = _c_prep(x, norm_g[i], c_mu[j], shift0)
            rkvg = _c_bmm(lerp, c_w, j)
            wl, al = _c_lora(lerp, c_w_la[j], c_w_lb[j], c_a_la[j], c_a_lb[j], c_w0[j], c_a0[j])
            par = (c_k_k[j], c_k_a[j], c_r_k[j], c_ln_w[j], c_ln_b[j])
            og_p, st_p = _rwkv(rkvg, wl, al, *par, None, nseq=BATCH, nchunk=SEQ // CHUNK, row0=0)
            og_s, st_s = _rwkv(rkvg, wl, al, *par, state_c_wkv[j], nseq=DEC_BATCH, nchunk=1,
                               row0=N_PROMPT // CHUNK)
            outs["cwp"].append(st_p)
            outs["chp"].append(hlast[:BATCH, 0])
            outs["cws"].append(st_s)
            outs["chs"].append(hlast[BATCH:, 0])
        if i == DEPTH - 1:
            yp, ys = _proj_out(og_p, og_s, x, o_w, i, final_g)
        else:
            x = _proj_out(og_p, og_s, x, o_w, i)
    st = lambda n: jnp.stack(outs[n])
    return (yp.reshape(BATCH, SEQ, D_MODEL), ys.reshape(DEC_BATCH, DEC_SEQ, D_MODEL),
            st("akp"), st("avp"), st("aks"), st("avs"),
            st("bkp"), st("bvp"), st("bip"), st("bks"), st("bvs"), st("bis"),
            st("cwp"), st("chp"), st("cws"), st("chs"))
```
